```python
import math
import jax, jax.numpy as jnp
from jax import lax
import numpy as np

D_MODEL = 1024
BATCH = 4
SEQ = 8192
DEPTH = 4

CHUNK = 64
N_HEADS = 8
HEAD_DIM = 64
ATTN_W = N_HEADS * HEAD_DIM
KV_DIM = HEAD_DIM
IDX_HEADS = 4
IDX_DIM = 64
TOPK_MAX = 256
Q_BLOCK = 128
SSM_W = D_MODEL - ATTN_W
SSM_GROUP = 16
SSM_GROUPS = SSM_W // SSM_GROUP
SSM_STATE = 64
SCAN_BLOCK = 128
DT_MIN = 1e-3
DT_MAX = 1e-1
ROPE_THETA = 500000.0
ROT_DIM = HEAD_DIM // 4
D_FF = 2816
MIX_W = ATTN_W + SSM_W
EPS = 1e-6
N_MOD = 9
IN_COLS = ATTN_W + 2 * KV_DIM + IDX_HEADS * IDX_DIM + IDX_DIM + IDX_HEADS + SSM_W
IN_SPLITS = (
    ATTN_W,
    ATTN_W + KV_DIM,
    ATTN_W + 2 * KV_DIM,
    ATTN_W + 2 * KV_DIM + IDX_HEADS * IDX_DIM,
    ATTN_W + 2 * KV_DIM + IDX_HEADS * IDX_DIM + IDX_DIM,
    ATTN_W + 2 * KV_DIM + IDX_HEADS * IDX_DIM + IDX_DIM + IDX_HEADS,
)

kernel_name = 'hybrid_dsa_s5_macaron_adaln'


def rms_norm(x, g):
    xf = x.astype(jnp.float32)
    y = xf * lax.rsqrt(jnp.mean(xf * xf, axis=-1, keepdims=True) + EPS)
    return (y * g.astype(jnp.float32)).astype(x.dtype)


def modulate(h, shift, scale):
    return h * (1.0 + scale[:, None, :]) + shift[:, None, :]


def swiglu(h, w1, w3, w2):
    return (jax.nn.silu(h @ w1) * (h @ w3)) @ w2


def rope_tables(positions):
    inv_freq = 1.0 / (ROPE_THETA ** (jnp.arange(0, ROT_DIM, 2, dtype=jnp.float32) / ROT_DIM))
    ang = positions.astype(jnp.float32)[..., None] * inv_freq
    return jnp.cos(ang), jnp.sin(ang)


def apply_partial_rope(t, cos, sin):
    half = ROT_DIM // 2
    tf = t.astype(jnp.float32)
    t1 = tf[..., :half]
    t2 = tf[..., half:ROT_DIM]
    cs = cos[:, :, None, :]
    sn = sin[:, :, None, :]
    out = jnp.concatenate([t1 * cs - t2 * sn, t1 * sn + t2 * cs, tf[..., ROT_DIM:]], axis=-1)
    return out.astype(t.dtype)


def dsa_attention(q, k, v, qi, ki, wi):
    bsz, seq = q.shape[0], q.shape[1]
    topk = min(TOPK_MAX, seq // 4)
    nb = seq // Q_BLOCK
    key_chunk = jnp.arange(seq) // CHUNK
    idx_scale = IDX_DIM ** -0.5 * IDX_HEADS ** -0.5
    att_scale = HEAD_DIM ** -0.5
    gather = jax.vmap(lambda t, i: t[i])

    def to_blocks(t):
        return jnp.moveaxis(t.reshape((bsz, nb, Q_BLOCK) + t.shape[2:]), 1, 0)

    def block_fn(args):
        start, qb, qib, wb = args
        q_chunk = (start + jnp.arange(Q_BLOCK)) // CHUNK
        rel = jax.nn.relu(jnp.einsum('bqhd,bsd->bqhs', qib, ki, preferred_element_type=jnp.float32))
        score = jnp.einsum('bqhs,bqh->bqs', rel, wb.astype(jnp.float32)) * idx_scale
        admissible = key_chunk[None, :] <= q_chunk[:, None]
        score = jnp.where(admissible[None], score, -jnp.inf)
        vals, idx = lax.top_k(score, topk)
        k_sel = gather(k, idx)
        v_sel = gather(v, idx)
        logits = jnp.einsum('bqhd,bqkd->bqhk', qb, k_sel, preferred_element_type=jnp.float32) * att_scale
        logits = jnp.where(jnp.isfinite(vals)[:, :, None, :], logits, -jnp.inf)
        p = jax.nn.softmax(logits, axis=-1)
        return jnp.einsum('bqhk,bqkd->bqhd', p.astype(v_sel.dtype), v_sel)

    starts = jnp.arange(nb, dtype=jnp.int32) * Q_BLOCK
    out = lax.map(block_fn, (starts, to_blocks(q), to_blocks(qi), to_blocks(wi)))
    out = jnp.moveaxis(out, 0, 1).reshape(bsz, seq, N_HEADS * HEAD_DIM)
    return out


def s5_ssm(u, a_re, a_im, log_dt, b_re, b_im, c_re, c_im, d_skip):
    bsz, seq, _ = u.shape
    nb = seq // SCAN_BLOCK
    uf = u.astype(jnp.float32).reshape(bsz, nb, SCAN_BLOCK, SSM_GROUPS, SSM_GROUP)
    uf = uf.transpose(1, 2, 0, 3, 4)
    a = lax.complex(a_re.astype(jnp.float32), a_im.astype(jnp.float32))
    dt = jnp.exp(log_dt.astype(jnp.float32))[:, None]
    a_bar = jnp.exp(a * dt)
    b_bar = ((a_bar - 1.0) / a)[:, :, None] * lax.complex(b_re.astype(jnp.float32), b_im.astype(jnp.float32))
    c_mat = lax.complex(c_re.astype(jnp.float32), c_im.astype(jnp.float32))
    d_f = d_skip.astype(jnp.float32)
    a_elems = jnp.broadcast_to(a_bar, (SCAN_BLOCK, 1, SSM_GROUPS, SSM_STATE))

    def combine(e1, e2):
        a1, b1 = e1
        a2, b2 = e2
        return a1 * a2, a2 * b1 + b2

    def step(h, ub):
        bu = jnp.einsum('gpc,sbgc->sbgp', b_bar, ub.astype(jnp.complex64))
        a_cum, b_scan = lax.associative_scan(combine, (a_elems, bu), axis=0)
        states = a_cum * h[None] + b_scan
        y = jnp.einsum('gcp,sbgp->sbgc', c_mat, states).real + d_f * ub
        return states[-1], y

    h0 = jnp.zeros((bsz, SSM_GROUPS, SSM_STATE), jnp.complex64)
    _, ys = lax.scan(step, h0, uf)
    return ys.transpose(2, 0, 1, 3, 4).reshape(bsz, seq, SSM_W).astype(u.dtype)


def hybrid_mixer(h, cos, sin, w_in, w_out, attn_gain, ssm_gain, a_re, a_im, log_dt,
                 b_re, b_im, c_re, c_im, d_skip, glu_w, glu_b):
    bsz, seq, _ = h.shape
    proj = h @ w_in
    q, k, v, qi, ki, wi, u = jnp.split(proj, IN_SPLITS, axis=-1)
    q = apply_partial_rope(q.reshape(bsz, seq, N_HEADS, HEAD_DIM), cos, sin)
    k = apply_partial_rope(k[:, :, None, :], cos, sin)[:, :, 0, :]
    qi = apply_partial_rope(qi.reshape(bsz, seq, IDX_HEADS, IDX_DIM), cos, sin)
    ki = apply_partial_rope(ki[:, :, None, :], cos, sin)[:, :, 0, :]
    attn = dsa_attention(q, k, v, qi, ki, wi)
    y = s5_ssm(u, a_re, a_im, log_dt, b_re, b_im, c_re, c_im, d_skip)
    y = jax.nn.gelu(y)
    y = y * jax.nn.sigmoid(y @ glu_w + glu_b)
    mixed = jnp.concatenate([rms_norm(attn, attn_gain), rms_norm(y, ssm_gain)], axis=-1)
    return mixed @ w_out


def setup_inputs(seed: int = 0) -> dict:
    key = jax.random.key(seed)
    ks = jax.random.split(key, 32)
    f32 = jnp.float32

    def nrm(k, shape, scale):
        return jax.random.normal(k, shape, f32) * scale

    x = nrm(ks[0], (BATCH, SEQ, D_MODEL), 1.0)
    c = nrm(ks[1], (BATCH, D_MODEL), 1.0)
    offsets = jax.random.randint(ks[2], (BATCH,), 0, 64, dtype=jnp.int32) * CHUNK
    positions = jnp.arange(SEQ, dtype=jnp.int32)[None, :] + offsets[:, None]
    mod_w = nrm(ks[3], (DEPTH, D_MODEL, N_MOD * D_MODEL), 0.5 * D_MODEL ** -0.5)
    mod_b = nrm(ks[4], (DEPTH, N_MOD * D_MODEL), 0.02)
    norm_g = 1.0 + nrm(ks[5], (DEPTH, 3, D_MODEL), 0.02)
    ffn1_w1 = nrm(ks[6], (DEPTH, D_MODEL, D_FF), D_MODEL ** -0.5)
    ffn1_w3 = nrm(ks[7], (DEPTH, D_MODEL, D_FF), D_MODEL ** -0.5)
    ffn1_w2 = nrm(ks[8], (DEPTH, D_FF, D_MODEL), D_FF ** -0.5)
    ffn2_w1 = nrm(ks[9], (DEPTH, D_MODEL, D_FF), D_MODEL ** -0.5)
    ffn2_w3 = nrm(ks[10], (DEPTH, D_MODEL, D_FF), D_MODEL ** -0.5)
    ffn2_w2 = nrm(ks[11], (DEPTH, D_FF, D_MODEL), D_FF ** -0.5)
    w_in = nrm(ks[12], (DEPTH, D_MODEL, IN_COLS), D_MODEL ** -0.5)
    w_out = nrm(ks[13], (DEPTH, MIX_W, D_MODEL), MIX_W ** -0.5)
    attn_gain = 1.0 + nrm(ks[14], (DEPTH, ATTN_W), 0.02)
    ssm_gain = 1.0 + nrm(ks[15], (DEPTH, SSM_W), 0.02)
    gp = (DEPTH, SSM_GROUPS, SSM_STATE)
    ssm_a_re = -0.5 * jnp.exp(nrm(ks[16], gp, 0.02))
    ssm_a_im = math.pi * jnp.arange(SSM_STATE, dtype=f32)[None, None, :] + nrm(ks[17], gp, 0.01)
    ssm_log_dt = jax.random.uniform(ks[18], (DEPTH, SSM_GROUPS), f32, math.log(DT_MIN), math.log(DT_MAX))
    ssm_b_re = nrm(ks[19], (DEPTH, SSM_GROUPS, SSM_STATE, SSM_GROUP), (2 * SSM_GROUP) ** -0.5)
    ssm_b_im = nrm(ks[20], (DEPTH, SSM_GROUPS, SSM_STATE, SSM_GROUP), (2 * SSM_GROUP) ** -0.5)
    ssm_c_re = nrm(ks[21], (DEPTH, SSM_GROUPS, SSM_GROUP, SSM_STATE), SSM_STATE ** -0.5)
    ssm_c_im = nrm(ks[22], (DEPTH, SSM_GROUPS, SSM_GROUP, SSM_STATE), SSM_STATE ** -0.5)
    ssm_d = nrm(ks[23], (DEPTH, SSM_GROUPS, SSM_GROUP), 1.0)
    glu_w = nrm(ks[24], (DEPTH, SSM_W, SSM_W), SSM_W ** -0.5)
    glu_b = nrm(ks[25], (DEPTH, SSM_W), 0.02)
    final_g = 1.0 + nrm(ks[26], (D_MODEL,), 0.02)
    return {
        'x': x, 'c': c, 'positions': positions,
        'mod_w': mod_w, 'mod_b': mod_b, 'norm_g': norm_g,
        'ffn1_w1': ffn1_w1, 'ffn1_w3': ffn1_w3, 'ffn1_w2': ffn1_w2,
        'ffn2_w1': ffn2_w1, 'ffn2_w3': ffn2_w3, 'ffn2_w2': ffn2_w2,
        'w_in': w_in, 'w_out': w_out, 'attn_gain': attn_gain, 'ssm_gain': ssm_gain,
        'ssm_a_re': ssm_a_re, 'ssm_a_im': ssm_a_im, 'ssm_log_dt': ssm_log_dt,
        'ssm_b_re': ssm_b_re, 'ssm_b_im': ssm_b_im, 'ssm_c_re': ssm_c_re, 'ssm_c_im': ssm_c_im,
        'ssm_d': ssm_d, 'glu_w': glu_w, 'glu_b': glu_b, 'final_g': final_g,
    }


def reference(x, c, positions, mod_w, mod_b, norm_g, ffn1_w1, ffn1_w3, ffn1_w2,
              ffn2_w1, ffn2_w3, ffn2_w2, w_in, w_out, attn_gain, ssm_gain,
              ssm_a_re, ssm_a_im, ssm_log_dt, ssm_b_re, ssm_b_im, ssm_c_re, ssm_c_im,
              ssm_d, glu_w, glu_b, final_g):
    cos, sin = rope_tables(positions)
    for l in range(DEPTH):
        mod = jax.nn.silu(c) @ mod_w[l] + mod_b[l]
        sh1, sc1, g1, sh2, sc2, g2, sh3, sc3, g3 = jnp.split(mod, N_MOD, axis=-1)
        h = modulate(rms_norm(x, norm_g[l, 0]), sh1, sc1)
        x = x + 0.5 * g1[:, None, :] * swiglu(h, ffn1_w1[l], ffn1_w3[l], ffn1_w2[l])
        h = modulate(rms_norm(x, norm_g[l, 1]), sh2, sc2)
        x = x + g2[:, None, :] * hybrid_mixer(
            h, cos, sin, w_in[l], w_out[l], attn_gain[l], ssm_gain[l],
            ssm_a_re[l], ssm_a_im[l], ssm_log_dt[l], ssm_b_re[l], ssm_b_im[l],
            ssm_c_re[l], ssm_c_im[l], ssm_d[l], glu_w[l], glu_b[l])
        h = modulate(rms_norm(x, norm_g[l, 2]), sh3, sc3)
        x = x + 0.5 * g3[:, None, :] * swiglu(h, ffn2_w1[l], ffn2_w3[l], ffn2_w2[l])
    return rms_norm(x, final_g)
```

```python
import functools
import math

import jax
import jax.numpy as jnp
from jax import lax
from jax.experimental import pallas as pl
from jax.experimental.pallas import tpu as pltpu

F32 = jnp.float32
MXU_DTYPE = jnp.bfloat16

CHUNK = 64
N_HEADS = 8
HEAD_DIM = 64
ATTN_W = N_HEADS * HEAD_DIM
IDX_HEADS = 4
IDX_DIM = 64
TOPK_MAX = 256
SSM_GROUP = 16
SSM_STATE = 64
SCAN_L = 128
ROPE_THETA = 500000.0
ROT_DIM = HEAD_DIM // 4
EPS = 1e-6
N_MOD = 9

LANES = 128
VMEM_LIMIT_BYTES = 56 * 1024 * 1024

INT_MIN = -(2 ** 31)
NEG_BIAS = -1e30

ROW_TILE = 512
Q_TILE = 128
K_TILE = 512
SSM_SUB = 16


def _params(*sem):
    return pltpu.CompilerParams(dimension_semantics=sem, vmem_limit_bytes=VMEM_LIMIT_BYTES)


def _dot(a, b):
    return jnp.dot(a, b, preferred_element_type=F32)


def _rms(x, g):
    return x * lax.rsqrt(jnp.mean(x * x, axis=-1, keepdims=True) + EPS) * g


def _modulated_norm(x, g, mod_ref, base):
    shift = mod_ref[base:base + 1, :]
    scale = mod_ref[base + 1:base + 2, :]
    return _rms(x, g) * (1.0 + scale) + shift


def _mod_kernel(c_ref, w_ref, b_ref, o_ref):
    c = c_ref[...]
    o_ref[...] = _dot(c * jax.nn.sigmoid(c), w_ref[...]) + b_ref[...]


def _mod_call(c, mod_w, mod_b):
    depth, d, nd = mod_w.shape
    b = c.shape[0]
    rows = -(-b // 8) * 8
    c_pad = jnp.zeros((rows, d), F32).at[:b].set(c)
    out = pl.pallas_call(
        _mod_kernel,
        out_shape=jax.ShapeDtypeStruct((depth, rows, nd), F32),
        grid=(depth, nd // d),
        in_specs=[pl.BlockSpec((rows, d), lambda l, j: (0, 0)),
                  pl.BlockSpec((None, d, d), lambda l, j: (l, 0, j)),
                  pl.BlockSpec((None, 1, d), lambda l, j: (l, 0, j))],
        out_specs=pl.BlockSpec((None, rows, d), lambda l, j: (l, 0, j)),
        compiler_params=_params("parallel", "parallel"),
        name="adaln_mod",
    )(c_pad, mod_w, mod_b.reshape(depth, 1, nd))
    return out[:, :b].reshape(depth, b, N_MOD, d)


def _ffn_kernel(x_ref, mod_ref, g_ref, w1_ref, w3_ref, w2_ref, o_ref, *, base, ff_chunk):
    x = x_ref[...]
    h = _modulated_norm(x, g_ref[...], mod_ref, base).astype(MXU_DTYPE)
    acc = jnp.zeros(x.shape, F32)
    for c0 in range(0, w1_ref.shape[1], ff_chunk):
        a = _dot(h, w1_ref[:, c0:c0 + ff_chunk])
        b = _dot(h, w3_ref[:, c0:c0 + ff_chunk])
        t = (a * jax.nn.sigmoid(a) * b).astype(MXU_DTYPE)
        acc = acc + _dot(t, w2_ref[c0:c0 + ff_chunk, :])
    o_ref[...] = x + 0.5 * mod_ref[base + 2:base + 3, :] * acc


def _resident(shape, index_map):
    return pl.BlockSpec(shape, index_map, pipeline_mode=pl.Buffered(1))


def _ffn_call(x, mod_l, g, w1, w3, w2, layer, base):
    b, s, d = x.shape
    ff = w1.shape[2]
    tm = min(ROW_TILE, s)
    ff_chunk = ff // 2 if (ff // 2) % LANES == 0 else ff
    wspec_in = _resident((None, d, ff), lambda bi, i: (layer, 0, 0))
    wspec_out = _resident((None, ff, d), lambda bi, i: (layer, 0, 0))
    return pl.pallas_call(
        functools.partial(_ffn_kernel, base=base, ff_chunk=ff_chunk),
        out_shape=jax.ShapeDtypeStruct(x.shape, F32),
        grid=(b, s // tm),
        in_specs=[pl.BlockSpec((None, tm, d), lambda bi, i: (bi, i, 0)),
                  pl.BlockSpec((None, N_MOD, d), lambda bi, i: (bi, 0, 0)),
                  pl.BlockSpec((1, d), lambda bi, i: (0, 0)),
                  wspec_in, wspec_in, wspec_out],
        out_specs=pl.BlockSpec((None, tm, d), lambda bi, i: (bi, i, 0)),
        compiler_params=_params("parallel", "parallel"),
        name="ffn",
    )(x, mod_l, g.reshape(1, d), w1, w3, w2)


def _rope128(t, cos, s_lo, s_hi):
    half = ROT_DIM // 2
    return t * cos + pltpu.roll(t, LANES - half, 1) * s_lo + pltpu.roll(t, half, 1) * s_hi


def _inproj_kernel(x_ref, mod_ref, g_ref, wq_ref, wqi_ref, wm_ref, wu_ref, cos_ref, slo_ref, shi_ref,
                   q_ref, qi_ref, m_ref, u_ref):
    h = _modulated_norm(x_ref[...], g_ref[...], mod_ref, 3).astype(MXU_DTYPE)
    cos, s_lo, s_hi = cos_ref[...], slo_ref[...], shi_ref[...]
    q = _dot(h, wq_ref[...])
    for j in range(0, q.shape[1], LANES):
        r = _rope128(q[:, j:j + LANES], cos, s_lo, s_hi)
        q_ref[:, j:j + LANES] = (r * HEAD_DIM ** -0.5).astype(q_ref.dtype)
    qi = _dot(h, wqi_ref[...])
    for j in range(0, qi.shape[1], LANES):
        qi_ref[:, j:j + LANES] = _rope128(qi[:, j:j + LANES], cos, s_lo, s_hi).astype(qi_ref.dtype)
    m = _dot(h, wm_ref[...])
    m_ref[:, :LANES] = _rope128(m[:, :LANES], cos, s_lo, s_hi)
    m_ref[:, LANES:] = m[:, LANES:]
    u_ref[...] = _dot(h, wu_ref[...]).astype(u_ref.dtype)


def _inproj_call(x, mod_l, g, wq, wqi, wm, wu, tables, layer):
    b, s, d = x.shape
    tm = min(ROW_TILE, s)
    row = lambda w: pl.BlockSpec((None, tm, w), lambda bi, i: (bi, i, 0))
    wres = lambda w: _resident((None, d, w), lambda bi, i: (layer, 0, 0))
    ssm_w = wu.shape[2]
    return pl.pallas_call(
        _inproj_kernel,
        out_shape=(jax.ShapeDtypeStruct((b, s, ATTN_W), MXU_DTYPE),
                   jax.ShapeDtypeStruct((b, s, IDX_HEADS * IDX_DIM), MXU_DTYPE),
                   jax.ShapeDtypeStruct((b, s, 2 * LANES), F32),
                   jax.ShapeDtypeStruct((b, s, ssm_w), MXU_DTYPE)),
        grid=(b, s // tm),
        in_specs=[row(d),
                  pl.BlockSpec((None, N_MOD, d), lambda bi, i: (bi, 0, 0)),
                  pl.BlockSpec((1, d), lambda bi, i: (0, 0)),
                  wres(ATTN_W), wres(IDX_HEADS * IDX_DIM), wres(2 * LANES), wres(ssm_w),
                  row(LANES), row(LANES), row(LANES)],
        out_specs=(row(ATTN_W), row(IDX_HEADS * IDX_DIM), row(2 * LANES), row(ssm_w)),
        compiler_params=_params("parallel", "parallel"),
        name="mixer_inproj",
    )(x, mod_l, g.reshape(1, d), wq, wqi, wm, wu, *tables)


def _rope_tables(positions):
    inv_freq = 1.0 / (ROPE_THETA ** (jnp.arange(0, ROT_DIM, 2, dtype=F32) / ROT_DIM))
    ang = positions.astype(F32)[..., None] * inv_freq
    cos, sin = jnp.cos(ang), jnp.sin(ang)
    half = ROT_DIM // 2
    rest = HEAD_DIM - ROT_DIM
    pad = lambda *parts: jnp.tile(jnp.concatenate(parts, axis=-1), (1, 1, LANES // HEAD_DIM))
    zeros = lambda n: jnp.zeros(cos.shape[:-1] + (n,), F32)
    ones = jnp.ones(cos.shape[:-1] + (rest,), F32)
    return (pad(cos, cos, ones),
            pad(-sin, zeros(half), zeros(rest)),
            pad(zeros(half), sin, zeros(rest)))


def _dsa_kernel(q_ref, qi_ref, w_ref, kt_ref, kit_ref, v_ref, o_ref,
                key_ref, bias_ref, m_ref, l_ref, acc_ref, *, topk, idx_bits):
    tq = q_ref.shape[0]
    tk = kt_ref.shape[2]
    nslab = tk // LANES
    i = pl.program_id(1)
    n_tiles = ((i + 1) * tq + tk - 1) // tk
    row = lax.broadcasted_iota(jnp.int32, (tq, 1), 0) + i * tq
    q_lim = (row // CHUNK + 1) * CHUNK
    lane_tile = lax.broadcasted_iota(jnp.int32, (tq, tk), 1)
    lane_slab = lax.broadcasted_iota(jnp.int32, (tq, LANES), 1)

    w = w_ref[...] * (IDX_DIM ** -0.5 * IDX_HEADS ** -0.5)
    qi = qi_ref[...]
    qi_heads = [qi[:, h * IDX_DIM:(h + 1) * IDX_DIM] for h in range(IDX_HEADS)]

    def score_body(t, carry):
        kit = kit_ref[t]
        s = jnp.zeros((tq, tk), F32)
        for h in range(IDX_HEADS):
            s = s + jnp.maximum(_dot(qi_heads[h], kit), 0.0) * w[:, h:h + 1]
        s = s + 0.0
        bits = pltpu.bitcast(s, jnp.int32)
        key = jnp.where(bits < 0, bits ^ jnp.int32(0x7FFFFFFF), bits)
        key_ref[t] = jnp.where(lane_tile + t * tk < q_lim, key, jnp.int32(INT_MIN))
        return carry

    lax.fori_loop(0, n_tiles, score_body, 0)

    def count(pred):
        def body(t, acc):
            key = key_ref[t]
            for j in range(nslab):
                hit = pred(key[:, j * LANES:(j + 1) * LANES], lane_slab + (t * tk + j * LANES))
                acc = acc + jnp.where(hit, 1.0, 0.0)
            return acc
        acc = lax.fori_loop(0, n_tiles, body, jnp.zeros((tq, LANES), F32))
        return jnp.sum(acc, axis=1, keepdims=True)

    def count_ge(thr):
        thr_b = jnp.broadcast_to(thr, (tq, LANES))
        return count(lambda key, kidx: key >= thr_b)

    k_f = float(topk)
    tau0 = jnp.where(count_ge(jnp.zeros((tq, 1), jnp.int32)) >= k_f, jnp.int32(0), jnp.int32(INT_MIN))

    def tau_body(it, tau):
        trial = tau | lax.shift_left(jnp.int32(1), 30 - it)
        return jnp.where(count_ge(trial) >= k_f, trial, tau)

    tau = lax.fori_loop(0, 31, tau_body, tau0)
    tau_b = jnp.broadcast_to(tau, (tq, LANES))
    need = k_f - count(lambda key, kidx: key > tau_b)

    def cut_body(it, cut):
        trial = cut | lax.shift_left(jnp.int32(1), idx_bits - 1 - it)
        trial_b = jnp.broadcast_to(trial, (tq, LANES))
        below = count(lambda key, kidx: (key == tau_b) & (kidx < trial_b))
        return jnp.where(below <= need - 1.0, trial, cut)

    cut = lax.fori_loop(0, idx_bits, cut_body, jnp.zeros((tq, 1), jnp.int32))

    def bias_body(t, carry):
        key = key_ref[t]
        kidx = lane_tile + t * tk
        sel = (key > tau) | ((key == tau) & (kidx <= cut))
        sel = sel & (key > jnp.int32(INT_MIN))
        bias_ref[t] = jnp.where(sel, 0.0, NEG_BIAS)
        return carry

    lax.fori_loop(0, n_tiles, bias_body, 0)

    m_ref[...] = jnp.full(m_ref.shape, NEG_BIAS, F32)
    l_ref[...] = jnp.zeros(l_ref.shape, F32)
    acc_ref[...] = jnp.zeros(acc_ref.shape, F32)
    q = q_ref[...]
    q_heads = [q[:, h * HEAD_DIM:(h + 1) * HEAD_DIM] for h in range(N_HEADS)]

    def att_body(t, carry):
        kt = kt_ref[t]
        v = v_ref[t]
        bias = bias_ref[t]
        for h in range(N_HEADS):
            s = _dot(q_heads[h], kt) + bias
            m_prev = m_ref[h]
            m_new = jnp.maximum(m_prev, jnp.max(s, axis=1, keepdims=True))
            alpha = jnp.exp(m_prev - m_new)
            p = jnp.exp(s - m_new)
            l_ref[h] = alpha * l_ref[h] + jnp.sum(p, axis=1, keepdims=True)
            acc_ref[h] = alpha * acc_ref[h] + _dot(p.astype(v.dtype), v)
            m_ref[h] = m_new
        return carry

    lax.fori_loop(0, n_tiles, att_body, 0)
    for h in range(N_HEADS):
        o_ref[:, h * HEAD_DIM:(h + 1) * HEAD_DIM] = acc_ref[h] / l_ref[h]


def _dsa_call(q, qi, w, k, ki, v):
    b, s, _ = q.shape
    tq = min(Q_TILE, s)
    tk = min(K_TILE, s)
    n_kt = s // tk
    topk = min(TOPK_MAX, s // 4)
    idx_bits = max(1, (s - 1).bit_length())
    to_tiles_t = lambda a: a.astype(MXU_DTYPE).reshape(b, n_kt, tk, a.shape[-1]).transpose(0, 1, 3, 2)
    kt, kit = to_tiles_t(k), to_tiles_t(ki)
    vt = v.astype(MXU_DTYPE).reshape(b, n_kt, tk, HEAD_DIM)
    per_batch = lambda shape: pl.BlockSpec((None,) + shape, lambda bi, i: (bi, 0, 0, 0))
    qrow = lambda width: pl.BlockSpec((None, tq, width), lambda bi, i: (bi, i, 0))
    return pl.pallas_call(
        functools.partial(_dsa_kernel, topk=topk, idx_bits=idx_bits),
        out_shape=jax.ShapeDtypeStruct((b, s, ATTN_W), F32),
        grid=(b, s // tq),
        in_specs=[qrow(ATTN_W), qrow(IDX_HEADS * IDX_DIM), qrow(IDX_HEADS),
                  per_batch((n_kt, HEAD_DIM, tk)), per_batch((n_kt, IDX_DIM, tk)),
                  per_batch((n_kt, tk, HEAD_DIM))],
        out_specs=qrow(ATTN_W),
        scratch_shapes=[pltpu.VMEM((n_kt, tq, tk), jnp.int32),
                        pltpu.VMEM((n_kt, tq, tk), F32),
                        pltpu.VMEM((N_HEADS, tq, 1), F32),
                        pltpu.VMEM((N_HEADS, tq, 1), F32),
                        pltpu.VMEM((N_HEADS, tq, HEAD_DIM), F32)],
        compiler_params=_params("parallel", "parallel"),
        name="dsa_attention",
    )(q, qi, w, kt, kit, vt)


def _ssm_kernel(u_ref, toep_ref, win_ref, wout_ref, apow_ref, y_ref, hloc_ref, hprev_ref, *, batch):
    rows, width = u_ref.shape
    sub = toep_ref.shape[1]
    n_sub = width // sub
    hloc_ref[...] = _dot(u_ref[...], win_ref[...])
    a_same, a_swap = apow_ref[0:1, :], apow_ref[1:2, :]
    h = jnp.zeros((batch, 2 * SSM_STATE), F32)
    for c in range(rows // batch):
        hprev_ref[c * batch:(c + 1) * batch, :] = h
        h = a_same * h + a_swap * pltpu.roll(h, SSM_STATE, 1) + hloc_ref[c * batch:(c + 1) * batch, :]
    hprev = hprev_ref[...].astype(MXU_DTYPE)
    for tb in range(n_sub):
        acc = _dot(hprev, wout_ref[:, tb * sub:(tb + 1) * sub])
        for sb in range(tb + 1):
            acc = acc + _dot(u_ref[:, sb * sub:(sb + 1) * sub], toep_ref[tb - sb])
        y_ref[:, tb * sub:(tb + 1) * sub] = acc


def _ssm_call(u_g, toep, win, wout, apow, layer, batch):
    g, rows, width = u_g.shape
    n_sub, sub = toep.shape[2], toep.shape[3]
    per_group = lambda *shape: pl.BlockSpec((None, None) + shape, lambda gi: (layer, gi) + (0,) * len(shape))
    return pl.pallas_call(
        functools.partial(_ssm_kernel, batch=batch),
        out_shape=jax.ShapeDtypeStruct((g, rows, width), F32),
        grid=(g,),
        in_specs=[pl.BlockSpec((None, rows, width), lambda gi: (gi, 0, 0)),
                  per_group(n_sub, sub, sub), per_group(width, 2 * SSM_STATE),
                  per_group(2 * SSM_STATE, width), per_group(2, 2 * SSM_STATE)],
        out_specs=pl.BlockSpec((None, rows, width), lambda gi: (gi, 0, 0)),
        scratch_shapes=[pltpu.VMEM((rows, 2 * SSM_STATE), F32), pltpu.VMEM((rows, 2 * SSM_STATE), F32)],
        compiler_params=_params("parallel"),
        name="s5_scan",
    )(u_g, toep, win, wout, apow)


def _ssm_operators(a_re, a_im, log_dt, b_re, b_im, c_re, c_im, d_skip, chunk):
    a = lax.complex(a_re, a_im)
    a_dt = a * jnp.exp(log_dt)[..., None]
    a_bar = jnp.exp(a_dt)
    b_bar = ((a_bar - 1.0) / a)[..., None] * lax.complex(b_re, b_im)
    c_mat = lax.complex(c_re, c_im)
    lags = jnp.arange(chunk + 1, dtype=F32)
    a_pow = jnp.exp(a_dt[:, :, None, :] * lags[None, None, :, None])
    depth, g, p = a_re.shape
    gc = b_re.shape[-1]
    ca = c_mat[:, :, None] * a_pow[:, :, :chunk, None, :]
    taps = (jnp.einsum('dgtop,dgpi->dgtoi', ca.real, b_bar.real, precision=lax.Precision.HIGHEST)
            - jnp.einsum('dgtop,dgpi->dgtoi', ca.imag, b_bar.imag, precision=lax.Precision.HIGHEST))
    taps = taps.at[:, :, 0].add(d_skip[..., None] * jnp.eye(gc, dtype=F32))
    n_sub = chunk // SSM_SUB
    taps = jnp.concatenate([jnp.zeros((depth, g, SSM_SUB - 1, gc, gc), F32), taps], axis=2)
    blk = jnp.arange(n_sub)[:, None, None] * SSM_SUB
    lag_idx = blk + jnp.arange(SSM_SUB)[None, None, :] - jnp.arange(SSM_SUB)[None, :, None] + (SSM_SUB - 1)
    toep = taps[:, :, lag_idx]
    toep = toep.transpose(0, 1, 2, 3, 6, 4, 5).reshape(depth, g, n_sub, SSM_SUB * gc, SSM_SUB * gc)
    w_in = a_pow[:, :, chunk - 1::-1][:, :, :chunk, :, None] * b_bar[:, :, None]
    w_in = jnp.concatenate([w_in.real, w_in.imag], axis=3)
    w_in = w_in.transpose(0, 1, 2, 4, 3).reshape(depth, g, chunk * gc, 2 * p)
    w_out = c_mat[:, :, None] * a_pow[:, :, 1:, None, :]
    w_out = jnp.concatenate([w_out.real, -w_out.imag], axis=4)
    w_out = w_out.transpose(0, 1, 4, 2, 3).reshape(depth, g, 2 * p, chunk * gc)
    a_l = a_pow[:, :, chunk]
    apow = jnp.stack([jnp.concatenate([a_l.real, a_l.real], -1),
                      jnp.concatenate([-a_l.imag, a_l.imag], -1)], axis=2)
    return toep.astype(MXU_DTYPE), w_in.astype(MXU_DTYPE), w_out.astype(MXU_DTYPE), apow


def _outproj_kernel(x_ref, attn_ref, y_ref, mod_ref, ag_ref, sg_ref, gw_ref, gb_ref, woa_ref, wos_ref, o_ref):
    a = _rms(attn_ref[...], ag_ref[...]).astype(MXU_DTYPE)
    y = y_ref[...]
    y = 0.5 * y * (1.0 + jnp.tanh(math.sqrt(2.0 / math.pi) * (y + 0.044715 * (y * y * y))))
    y = y * jax.nn.sigmoid(_dot(y.astype(MXU_DTYPE), gw_ref[...]) + gb_ref[...])
    y = _rms(y, sg_ref[...]).astype(MXU_DTYPE)
    mixed = _dot(a, woa_ref[...]) + _dot(y, wos_ref[...])
    o_ref[...] = x_ref[...] + mod_ref[5:6, :] * mixed


def _outproj_call(x, attn, y, mod_l, attn_gain, ssm_gain, glu_w, glu_b, wo_a, wo_s, layer):
    b, s, d = x.shape
    tm = min(ROW_TILE, s)
    aw, sw = attn.shape[2], y.shape[2]
    row = lambda w: pl.BlockSpec((None, tm, w), lambda bi, i: (bi, i, 0))
    vec = lambda w: pl.BlockSpec((None, 1, w), lambda bi, i: (layer, 0, 0))
    wres = lambda r, c: _resident((None, r, c), lambda bi, i: (layer, 0, 0))
    return pl.pallas_call(
        _outproj_kernel,
        out_shape=jax.ShapeDtypeStruct(x.shape, F32),
        grid=(b, s // tm),
        in_specs=[row(d), row(aw), row(sw),
                  pl.BlockSpec((None, N_MOD, d), lambda bi, i: (bi, 0, 0)),
                  vec(aw), vec(sw), wres(sw, sw), vec(sw), wres(aw, d), wres(sw, d)],
        out_specs=row(d),
        compiler_params=_params("parallel", "parallel"),
        name="mixer_outproj",
    )(x, attn, y, mod_l, attn_gain[:, None, :], ssm_gain[:, None, :], glu_w, glu_b[:, None, :], wo_a, wo_s)


def _final_norm_kernel(x_ref, g_ref, o_ref):
    o_ref[...] = _rms(x_ref[...], g_ref[...])


def _final_norm_call(x, g):
    b, s, d = x.shape
    tm = min(ROW_TILE, s)
    return pl.pallas_call(
        _final_norm_kernel,
        out_shape=jax.ShapeDtypeStruct(x.shape, F32),
        grid=(b, s // tm),
        in_specs=[pl.BlockSpec((None, tm, d), lambda bi, i: (bi, i, 0)),
                  pl.BlockSpec((1, d), lambda bi, i: (0, 0))],
        out_specs=pl.BlockSpec((None, tm, d), lambda bi, i: (bi, i, 0)),
        compiler_params=_params("parallel", "parallel"),
        name="final_norm",
    )(x, g.reshape(1, d))


def kernel(x, c, positions, mod_w, mod_b, norm_g, ffn1_w1, ffn1_w3, ffn1_w2, ffn2_w1, ffn2_w3, ffn2_w2, w_in, w_out, attn_gain, ssm_gain, ssm_a_re, ssm_a_im, ssm_log_dt, ssm_b_re, ssm_b_im, ssm_c_re, ssm_c_im, ssm_d, glu_w, glu_b, final_g):
    b, s, d = x.shape
    depth = mod_w.shape[0]
    ssm_w = d - ATTN_W
    groups = ssm_w // SSM_GROUP
    chunk = min(SCAN_L, s)
    n_chunks = s // chunk
    cast = lambda a: a.astype(MXU_DTYPE)

    mod = _mod_call(c, mod_w, mod_b)
    tables = _rope_tables(positions)
    ffn1 = (cast(ffn1_w1), cast(ffn1_w3), cast(ffn1_w2))
    ffn2 = (cast(ffn2_w1), cast(ffn2_w3), cast(ffn2_w2))
    o_k = ATTN_W
    o_v = o_k + HEAD_DIM
    o_qi = o_v + HEAD_DIM
    o_ki = o_qi + IDX_HEADS * IDX_DIM
    o_wi = o_ki + IDX_DIM
    o_u = o_wi + IDX_HEADS
    wq = cast(w_in[:, :, :o_k])
    wqi = cast(w_in[:, :, o_qi:o_ki])
    wm = jnp.concatenate([w_in[:, :, o_k:o_v], w_in[:, :, o_ki:o_wi], w_in[:, :, o_v:o_qi], w_in[:, :, o_wi:o_u],
                          jnp.zeros((depth, d, 2 * LANES - (2 * HEAD_DIM + IDX_DIM + IDX_HEADS)), w_in.dtype)],
                         axis=2)
    wm = cast(wm)
    wu = cast(w_in[:, :, o_u:])
    wo_a, wo_s = cast(w_out[:, :ATTN_W]), cast(w_out[:, ATTN_W:])
    glu_wc = cast(glu_w)
    toep, s_win, s_wout, apow = _ssm_operators(ssm_a_re, ssm_a_im, ssm_log_dt, ssm_b_re, ssm_b_im,
                                               ssm_c_re, ssm_c_im, ssm_d, chunk)

    for l in range(depth):
        x = _ffn_call(x, mod[l], norm_g[l, 0], *ffn1, l, 0)
        q, qi, misc, u = _inproj_call(x, mod[l], norm_g[l, 1], wq, wqi, wm, wu, tables, l)
        k = misc[..., :HEAD_DIM]
        ki = misc[..., HEAD_DIM:2 * HEAD_DIM]
        v = misc[..., 2 * HEAD_DIM:3 * HEAD_DIM]
        wi = misc[..., 3 * HEAD_DIM:3 * HEAD_DIM + IDX_HEADS]
        attn = _dsa_call(q, qi, wi, k, ki, v)
        u_g = u.reshape(b, n_chunks, chunk, groups, SSM_GROUP).transpose(3, 1, 0, 2, 4)
        u_g = u_g.reshape(groups, n_chunks * b, chunk * SSM_GROUP)
        y_g = _ssm_call(u_g, toep, s_win, s_wout, apow, l, b)
        y = y_g.reshape(groups, n_chunks, b, chunk, SSM_GROUP).transpose(2, 1, 3, 0, 4).reshape(b, s, ssm_w)
        x = _outproj_call(x, attn, y, mod[l], attn_gain, ssm_gain, glu_wc, glu_b, wo_a, wo_s, l)
        x = _ffn_call(x, mod[l], norm_g[l, 2], *ffn2, l, 6)
    return _final_norm_call(x, final_g)
```

```python
import functools
import math

import jax
import jax.numpy as jnp
from jax import lax
from jax.experimental import pallas as pl
from jax.experimental.pallas import tpu as pltpu

F32 = jnp.float32
MXU_DTYPE = jnp.bfloat16

CHUNK = 64
N_HEADS = 8
HEAD_DIM = 64
ATTN_W = N_HEADS * HEAD_DIM
IDX_HEADS = 4
IDX_DIM = 64
TOPK_MAX = 256
SSM_GROUP = 16
SSM_STATE = 64
SCAN_L = 128
ROPE_THETA = 500000.0
ROT_DIM = HEAD_DIM // 4
EPS = 1e-6
N_MOD = 9

LANES = 128
VMEM_LIMIT_BYTES = 56 * 1024 * 1024

INT_MIN = -(2 ** 31)
NEG_BIAS = -1e30

ROW_TILE = 512
Q_TILE = 128
K_TILE = 512
SSM_SUB = 16


def _params(*sem):
    return pltpu.CompilerParams(dimension_semantics=sem, vmem_limit_bytes=VMEM_LIMIT_BYTES)


def _dot(a, b):
    return jnp.dot(a, b, preferred_element_type=F32)


def _rms(x, g):
    return x * lax.rsqrt(jnp.mean(x * x, axis=-1, keepdims=True) + EPS) * g


def _modulated_norm(x, g, mod_ref, base):
    shift = mod_ref[base:base + 1, :]
    scale = mod_ref[base + 1:base + 2, :]
    return _rms(x, g) * (1.0 + scale) + shift


def _mod_kernel(c_ref, w_ref, b_ref, o_ref):
    c = c_ref[...]
    o_ref[...] = _dot(c * jax.nn.sigmoid(c), w_ref[...]) + b_ref[...]


def _mod_call(c, mod_w, mod_b):
    depth, d, nd = mod_w.shape
    b = c.shape[0]
    rows = -(-b // 8) * 8
    c_pad = jnp.zeros((rows, d), F32).at[:b].set(c)
    out = pl.pallas_call(
        _mod_kernel,
        out_shape=jax.ShapeDtypeStruct((depth, rows, nd), F32),
        grid=(depth, nd // d),
        in_specs=[pl.BlockSpec((rows, d), lambda l, j: (0, 0)),
                  pl.BlockSpec((None, d, d), lambda l, j: (l, 0, j)),
                  pl.BlockSpec((None, 1, d), lambda l, j: (l, 0, j))],
        out_specs=pl.BlockSpec((None, rows, d), lambda l, j: (l, 0, j)),
        compiler_params=_params("parallel", "parallel"),
        name="adaln_mod",
    )(c_pad, mod_w, mod_b.reshape(depth, 1, nd))
    return out[:, :b].reshape(depth, b, N_MOD, d)


def _ffn_kernel(x_ref, mod_ref, g_ref, w1_ref, w3_ref, w2_ref, o_ref, *, base, ff_chunk):
    x = x_ref[...]
    h = _modulated_norm(x, g_ref[...], mod_ref, base).astype(MXU_DTYPE)
    acc = jnp.zeros(x.shape, F32)
    for c0 in range(0, w1_ref.shape[1], ff_chunk):
        a = _dot(h, w1_ref[:, c0:c0 + ff_chunk])
        b = _dot(h, w3_ref[:, c0:c0 + ff_chunk])
        t = (a * jax.nn.sigmoid(a) * b).astype(MXU_DTYPE)
        acc = acc + _dot(t, w2_ref[c0:c0 + ff_chunk, :])
    o_ref[...] = x + 0.5 * mod_ref[base + 2:base + 3, :] * acc


def _resident(shape, index_map):
    return pl.BlockSpec(shape, index_map, pipeline_mode=pl.Buffered(1))


def _ffn_call(x, mod_l, g, w1, w3, w2, layer, base):
    b, s, d = x.shape
    ff = w1.shape[2]
    tm = min(ROW_TILE, s)
    ff_chunk = ff // 2 if (ff // 2) % LANES == 0 else ff
    wspec_in = _resident((None, d, ff), lambda bi, i: (layer, 0, 0))
    wspec_out = _resident((None, ff, d), lambda bi, i: (layer, 0, 0))
    return pl.pallas_call(
        functools.partial(_ffn_kernel, base=base, ff_chunk=ff_chunk),
        out_shape=jax.ShapeDtypeStruct(x.shape, F32),
        grid=(b, s // tm),
        in_specs=[pl.BlockSpec((None, tm, d), lambda bi, i: (bi, i, 0)),
                  pl.BlockSpec((None, N_MOD, d), lambda bi, i: (bi, 0, 0)),
                  pl.BlockSpec((1, d), lambda bi, i: (0, 0)),
                  wspec_in, wspec_in, wspec_out],
        out_specs=pl.BlockSpec((None, tm, d), lambda bi, i: (bi, i, 0)),
        compiler_params=_params("parallel", "parallel"),
        name="ffn",
    )(x, mod_l, g.reshape(1, d), w1, w3, w2)


def _rope128(t, cos, s_lo, s_hi):
    half = ROT_DIM // 2
    return t * cos + pltpu.roll(t, LANES - half, 1) * s_lo + pltpu.roll(t, half, 1) * s_hi


def _inproj_kernel(x_ref, mod_ref, g_ref, wq_ref, wqi_ref, wm_ref, wu_ref, cos_ref, slo_ref, shi_ref,
                   q_ref, qi_ref, m_ref, u_ref):
    h = _modulated_norm(x_ref[...], g_ref[...], mod_ref, 3).astype(MXU_DTYPE)
    cos, s_lo, s_hi = cos_ref[...], slo_ref[...], shi_ref[...]
    q = _dot(h, wq_ref[...])
    for j in range(0, q.shape[1], LANES):
        r = _rope128(q[:, j:j + LANES], cos, s_lo, s_hi)
        q_ref[:, j:j + LANES] = (r * HEAD_DIM ** -0.5).astype(q_ref.dtype)
    qi = _dot(h, wqi_ref[...])
    for j in range(0, qi.shape[1], LANES):
        qi_ref[:, j:j + LANES] = _rope128(qi[:, j:j + LANES], cos, s_lo, s_hi).astype(qi_ref.dtype)
    m = _dot(h, wm_ref[...])
    m_ref[:, :LANES] = _rope128(m[:, :LANES], cos, s_lo, s_hi)
    m_ref[:, LANES:] = m[:, LANES:]
    u_ref[...] = _dot(h, wu_ref[...]).astype(u_ref.dtype)


def _inproj_call(x, mod_l, g, wq, wqi, wm, wu, tables, layer):
    b, s, d = x.shape
    tm = min(ROW_TILE, s)
    row = lambda w: pl.BlockSpec((None, tm, w), lambda bi, i: (bi, i, 0))
    wres = lambda w: _resident((None, d, w), lambda bi, i: (layer, 0, 0))
    ssm_w = wu.shape[2]
    return pl.pallas_call(
        _inproj_kernel,
        out_shape=(jax.ShapeDtypeStruct((b, s, ATTN_W), MXU_DTYPE),
                   jax.ShapeDtypeStruct((b, s, IDX_HEADS * IDX_DIM), MXU_DTYPE),
                   jax.ShapeDtypeStruct((b, s, 2 * LANES), F32),
                   jax.ShapeDtypeStruct((b, s, ssm_w), MXU_DTYPE)),
        grid=(b, s // tm),
        in_specs=[row(d),
                  pl.BlockSpec((None, N_MOD, d), lambda bi, i: (bi, 0, 0)),
                  pl.BlockSpec((1, d), lambda bi, i: (0, 0)),
                  wres(ATTN_W), wres(IDX_HEADS * IDX_DIM), wres(2 * LANES), wres(ssm_w),
                  row(LANES), row(LANES), row(LANES)],
        out_specs=(row(ATTN_W), row(IDX_HEADS * IDX_DIM), row(2 * LANES), row(ssm_w)),
        compiler_params=_params("parallel", "parallel"),
        name="mixer_inproj",
    )(x, mod_l, g.reshape(1, d), wq, wqi, wm, wu, *tables)


def _rope_tables(positions):
    inv_freq = 1.0 / (ROPE_THETA ** (jnp.arange(0, ROT_DIM, 2, dtype=F32) / ROT_DIM))
    ang = positions.astype(F32)[..., None] * inv_freq
    cos, sin = jnp.cos(ang), jnp.sin(ang)
    half = ROT_DIM // 2
    rest = HEAD_DIM - ROT_DIM
    pad = lambda *parts: jnp.tile(jnp.concatenate(parts, axis=-1), (1, 1, LANES // HEAD_DIM))
    zeros = lambda n: jnp.zeros(cos.shape[:-1] + (n,), F32)
    ones = jnp.ones(cos.shape[:-1] + (rest,), F32)
    return (pad(cos, cos, ones),
            pad(-sin, zeros(half), zeros(rest)),
            pad(zeros(half), sin, zeros(rest)))


def _dsa_kernel(q_ref, qi_ref, w_ref, kt_ref, kit_ref, v_ref, o_ref,
                key_ref, bias_ref, m_ref, acc_ref, *, topk):
    tq = q_ref.shape[0]
    tk = kt_ref.shape[2]
    nslab = tk // LANES
    i = pl.program_id(1)
    n_tiles = ((i + 1) * tq + tk - 1) // tk
    row = lax.broadcasted_iota(jnp.int32, (tq, 1), 0) + i * tq
    q_lim = (row // CHUNK + 1) * CHUNK
    lane_tile = lax.broadcasted_iota(jnp.int32, (tq, tk), 1)
    lane_slab = lax.broadcasted_iota(jnp.int32, (tq, LANES), 1)

    w = w_ref[...] * (IDX_DIM ** -0.5 * IDX_HEADS ** -0.5)
    qi = qi_ref[...]
    qi_heads = [qi[:, h * IDX_DIM:(h + 1) * IDX_DIM] for h in range(IDX_HEADS)]

    def score_body(t, carry):
        kit = kit_ref[t]
        s = jnp.zeros((tq, tk), F32)
        for h in range(IDX_HEADS):
            s = s + jnp.maximum(_dot(qi_heads[h], kit), 0.0) * w[:, h:h + 1]
        s = s + 0.0
        bits = pltpu.bitcast(s, jnp.int32)
        key = jnp.where(bits < 0, bits ^ jnp.int32(0x7FFFFFFF), bits)
        key_ref[t] = jnp.where(lane_tile + t * tk < q_lim, key, jnp.int32(INT_MIN))
        return carry

    lax.fori_loop(0, n_tiles, score_body, 0)

    def count(pred):
        def body(t, acc):
            key = key_ref[t]
            for j in range(nslab):
                hit = pred(key[:, j * LANES:(j + 1) * LANES], lane_slab + (t * tk + j * LANES))
                acc = acc + jnp.where(hit, 1.0, 0.0)
            return acc
        acc = lax.fori_loop(0, n_tiles, body, jnp.zeros((tq, LANES), F32))
        return jnp.sum(acc, axis=1, keepdims=True)

    def count_ge(thr):
        thr_b = jnp.broadcast_to(thr, (tq, LANES))
        return count(lambda key, kidx: key >= thr_b)

    k_f = float(topk)
    tau0 = jnp.where(count_ge(jnp.zeros((tq, 1), jnp.int32)) >= k_f, jnp.int32(0), jnp.int32(INT_MIN))

    def tau_body(it, tau):
        trial = tau | lax.shift_left(jnp.int32(1), 30 - it)
        return jnp.where(count_ge(trial) >= k_f, trial, tau)

    tau = lax.fori_loop(0, 31, tau_body, tau0)
    tau_b = jnp.broadcast_to(tau, (tq, LANES))
    need = k_f - count(lambda key, kidx: key > tau_b)

    upper = (lax.broadcasted_iota(jnp.int32, (tk, tk), 0)
             <= lax.broadcasted_iota(jnp.int32, (tk, tk), 1)).astype(MXU_DTYPE)

    def bias_body(t, ties_before):
        key = key_ref[t]
        tie = jnp.where(key == tau, 1.0, 0.0)
        rank = ties_before + _dot(tie.astype(MXU_DTYPE), upper)
        sel = (key > tau) | ((key == tau) & (rank <= need))
        sel = sel & (key > jnp.int32(INT_MIN))
        bias_ref[t] = jnp.where(sel, 0.0, NEG_BIAS)
        return ties_before + jnp.sum(tie, axis=1, keepdims=True)

    lax.fori_loop(0, n_tiles, bias_body, jnp.zeros((tq, 1), F32))

    q = q_ref[...]
    q_all = jnp.concatenate([q[:, h * HEAD_DIM:(h + 1) * HEAD_DIM] for h in range(N_HEADS)], axis=0)
    rows = N_HEADS * tq

    def logits(t):
        return _dot(q_all, kt_ref[t]).reshape(N_HEADS, tq, tk) + bias_ref[t][None]

    m_ref[...] = jnp.full(m_ref.shape, NEG_BIAS, F32)

    def max_body(t, carry):
        s = logits(t)
        mx = s[:, :, :LANES]
        for j in range(1, nslab):
            mx = jnp.maximum(mx, s[:, :, j * LANES:(j + 1) * LANES])
        m_ref[...] = jnp.maximum(m_ref[...], mx.reshape(rows, LANES))
        return carry

    lax.fori_loop(0, n_tiles, max_body, 0)
    m_ref[...] = jnp.broadcast_to(jnp.max(m_ref[...], axis=1, keepdims=True), (rows, LANES))
    acc_ref[...] = jnp.zeros(acc_ref.shape, F32)

    def att_body(t, carry):
        m_all = jnp.concatenate([m_ref[...]] * nslab, axis=1).reshape(N_HEADS, tq, tk)
        p = jnp.exp(logits(t) - m_all)
        acc_ref[...] += _dot(p.astype(MXU_DTYPE).reshape(rows, tk), v_ref[t])
        return carry

    lax.fori_loop(0, n_tiles, att_body, 0)
    acc = acc_ref[...]
    out = acc[:, :HEAD_DIM] / acc[:, HEAD_DIM:HEAD_DIM + 1]
    for h in range(N_HEADS):
        o_ref[:, h * HEAD_DIM:(h + 1) * HEAD_DIM] = out[h * tq:(h + 1) * tq]


def _dsa_call(q, qi, w, k, ki, v):
    b, s, _ = q.shape
    tq = min(Q_TILE, s)
    tk = min(K_TILE, s)
    n_kt = s // tk
    topk = min(TOPK_MAX, s // 4)
    to_tiles_t = lambda a: a.astype(MXU_DTYPE).reshape(b, n_kt, tk, a.shape[-1]).transpose(0, 1, 3, 2)
    kt, kit = to_tiles_t(k), to_tiles_t(ki)
    v_aug = jnp.concatenate([v, jnp.ones((b, s, 1), v.dtype), jnp.zeros((b, s, LANES - HEAD_DIM - 1), v.dtype)],
                            axis=-1)
    vt = v_aug.astype(MXU_DTYPE).reshape(b, n_kt, tk, LANES)
    per_batch = lambda shape: pl.BlockSpec((None,) + shape, lambda bi, i: (bi, 0, 0, 0))
    qrow = lambda width: pl.BlockSpec((None, tq, width), lambda bi, i: (bi, i, 0))
    return pl.pallas_call(
        functools.partial(_dsa_kernel, topk=topk),
        out_shape=jax.ShapeDtypeStruct((b, s, ATTN_W), F32),
        grid=(b, s // tq),
        in_specs=[qrow(ATTN_W), qrow(IDX_HEADS * IDX_DIM), qrow(IDX_HEADS),
                  per_batch((n_kt, HEAD_DIM, tk)), per_batch((n_kt, IDX_DIM, tk)),
                  per_batch((n_kt, tk, LANES))],
        out_specs=qrow(ATTN_W),
        scratch_shapes=[pltpu.VMEM((n_kt, tq, tk), jnp.int32),
                        pltpu.VMEM((n_kt, tq, tk), F32),
                        pltpu.VMEM((N_HEADS * tq, LANES), F32),
                        pltpu.VMEM((N_HEADS * tq, LANES), F32)],
        compiler_params=_params("parallel", "parallel"),
        name="dsa_attention",
    )(q, qi, w, kt, kit, vt)


def _ssm_kernel(u_ref, toep_ref, win_ref, wout_ref, apow_ref, y_ref, hloc_ref, hprev_ref, *, batch):
    rows, width = u_ref.shape
    sub = toep_ref.shape[1]
    n_sub = width // sub
    hloc_ref[...] = _dot(u_ref[...], win_ref[...])
    a_same, a_swap = apow_ref[0:1, :], apow_ref[1:2, :]
    h = jnp.zeros((batch, 2 * SSM_STATE), F32)
    for c in range(rows // batch):
        hprev_ref[c * batch:(c + 1) * batch, :] = h
        h = a_same * h + a_swap * pltpu.roll(h, SSM_STATE, 1) + hloc_ref[c * batch:(c + 1) * batch, :]
    hprev = hprev_ref[...].astype(MXU_DTYPE)
    for tb in range(n_sub):
        acc = _dot(hprev, wout_ref[:, tb * sub:(tb + 1) * sub])
        for sb in range(tb + 1):
            acc = acc + _dot(u_ref[:, sb * sub:(sb + 1) * sub], toep_ref[tb - sb])
        y_ref[:, tb * sub:(tb + 1) * sub] = acc


def _ssm_call(u_g, toep, win, wout, apow, layer, batch):
    g, rows, width = u_g.shape
    n_sub, sub = toep.shape[2], toep.shape[3]
    per_group = lambda *shape: pl.BlockSpec((None, None) + shape, lambda gi: (layer, gi) + (0,) * len(shape))
    return pl.pallas_call(
        functools.partial(_ssm_kernel, batch=batch),
        out_shape=jax.ShapeDtypeStruct((g, rows, width), F32),
        grid=(g,),
        in_specs=[pl.BlockSpec((None, rows, width), lambda gi: (gi, 0, 0)),
                  per_group(n_sub, sub, sub), per_group(width, 2 * SSM_STATE),
                  per_group(2 * SSM_STATE, width), per_group(2, 2 * SSM_STATE)],
        out_specs=pl.BlockSpec((None, rows, width), lambda gi: (gi, 0, 0)),
        scratch_shapes=[pltpu.VMEM((rows, 2 * SSM_STATE), F32), pltpu.VMEM((rows, 2 * SSM_STATE), F32)],
        compiler_params=_params("parallel"),
        name="s5_scan",
    )(u_g, toep, win, wout, apow)


def _ssm_operators(a_re, a_im, log_dt, b_re, b_im, c_re, c_im, d_skip, chunk):
    cmul = lambda xr, xi, yr, yi: (xr * yr - xi * yi, xr * yi + xi * yr)
    dt = jnp.exp(log_dt)[..., None]
    lam_re, lam_im = a_re * dt, a_im * dt
    lags = jnp.arange(chunk + 1, dtype=F32)[None, None, :, None]
    mag = jnp.exp(lam_re[:, :, None, :] * lags)
    ang = lam_im[:, :, None, :] * lags
    ap_re, ap_im = mag * jnp.cos(ang), mag * jnp.sin(ang)
    num_re, num_im = ap_re[:, :, 1] - 1.0, ap_im[:, :, 1]
    den = a_re * a_re + a_im * a_im
    coef_re, coef_im = (num_re * a_re + num_im * a_im) / den, (num_im * a_re - num_re * a_im) / den
    bb_re, bb_im = cmul(coef_re[..., None], coef_im[..., None], b_re, b_im)
    depth, g, p = a_re.shape
    gc = b_re.shape[-1]
    ca_re, ca_im = cmul(c_re[:, :, None], c_im[:, :, None], ap_re[:, :, :chunk, None, :], ap_im[:, :, :chunk, None, :])
    taps = (jnp.einsum('dgtop,dgpi->dgtoi', ca_re, bb_re, precision=lax.Precision.HIGHEST)
            - jnp.einsum('dgtop,dgpi->dgtoi', ca_im, bb_im, precision=lax.Precision.HIGHEST))
    taps = taps.at[:, :, 0].add(d_skip[..., None] * jnp.eye(gc, dtype=F32))
    n_sub = chunk // SSM_SUB
    taps = jnp.concatenate([jnp.zeros((depth, g, SSM_SUB - 1, gc, gc), F32), taps], axis=2)
    blk = jnp.arange(n_sub)[:, None, None] * SSM_SUB
    lag_idx = blk + jnp.arange(SSM_SUB)[None, None, :] - jnp.arange(SSM_SUB)[None, :, None] + (SSM_SUB - 1)
    toep = taps[:, :, lag_idx]
    toep = toep.transpose(0, 1, 2, 3, 6, 4, 5).reshape(depth, g, n_sub, SSM_SUB * gc, SSM_SUB * gc)
    rev_re, rev_im = ap_re[:, :, chunk - 1::-1][:, :, :chunk], ap_im[:, :, chunk - 1::-1][:, :, :chunk]
    wi_re, wi_im = cmul(rev_re[..., None], rev_im[..., None], bb_re[:, :, None], bb_im[:, :, None])
    w_in = jnp.concatenate([wi_re, wi_im], axis=3)
    w_in = w_in.transpose(0, 1, 2, 4, 3).reshape(depth, g, chunk * gc, 2 * p)
    wo_re, wo_im = cmul(c_re[:, :, None], c_im[:, :, None], ap_re[:, :, 1:, None, :], ap_im[:, :, 1:, None, :])
    w_out = jnp.concatenate([wo_re, -wo_im], axis=4)
    w_out = w_out.transpose(0, 1, 4, 2, 3).reshape(depth, g, 2 * p, chunk * gc)
    al_re, al_im = ap_re[:, :, chunk], ap_im[:, :, chunk]
    apow = jnp.stack([jnp.concatenate([al_re, al_re], -1),
                      jnp.concatenate([-al_im, al_im], -1)], axis=2)
    return toep.astype(MXU_DTYPE), w_in.astype(MXU_DTYPE), w_out.astype(MXU_DTYPE), apow


def _outproj_kernel(x_ref, attn_ref, y_ref, mod_ref, ag_ref, sg_ref, gw_ref, gb_ref, woa_ref, wos_ref, o_ref):
    a = _rms(attn_ref[...], ag_ref[...]).astype(MXU_DTYPE)
    y = y_ref[...]
    y = 0.5 * y * (1.0 + jnp.tanh(math.sqrt(2.0 / math.pi) * (y + 0.044715 * (y * y * y))))
    y = y * jax.nn.sigmoid(_dot(y.astype(MXU_DTYPE), gw_ref[...]) + gb_ref[...])
    y = _rms(y, sg_ref[...]).astype(MXU_DTYPE)
    mixed = _dot(a, woa_ref[...]) + _dot(y, wos_ref[...])
    o_ref[...] = x_ref[...] + mod_ref[5:6, :] * mixed


def _outproj_call(x, attn, y, mod_l, attn_gain, ssm_gain, glu_w, glu_b, wo_a, wo_s, layer):
    b, s, d = x.shape
    tm = min(ROW_TILE, s)
    aw, sw = attn.shape[2], y.shape[2]
    row = lambda w: pl.BlockSpec((None, tm, w), lambda bi, i: (bi, i, 0))
    vec = lambda w: pl.BlockSpec((None, 1, w), lambda bi, i: (layer, 0, 0))
    wres = lambda r, c: _resident((None, r, c), lambda bi, i: (layer, 0, 0))
    return pl.pallas_call(
        _outproj_kernel,
        out_shape=jax.ShapeDtypeStruct(x.shape, F32),
        grid=(b, s // tm),
        in_specs=[row(d), row(aw), row(sw),
                  pl.BlockSpec((None, N_MOD, d), lambda bi, i: (bi, 0, 0)),
                  vec(aw), vec(sw), wres(sw, sw), vec(sw), wres(aw, d), wres(sw, d)],
        out_specs=row(d),
        compiler_params=_params("parallel", "parallel"),
        name="mixer_outproj",
    )(x, attn, y, mod_l, attn_gain[:, None, :], ssm_gain[:, None, :], glu_w, glu_b[:, None, :], wo_a, wo_s)


def _final_norm_kernel(x_ref, g_ref, o_ref):
    o_ref[...] = _rms(x_ref[...], g_ref[...])


def _final_norm_call(x, g):
    b, s, d = x.shape
    tm = min(ROW_TILE, s)
    return pl.pallas_call(
        _final_norm_kernel,
        out_shape=jax.ShapeDtypeStruct(x.shape, F32),
        grid=(b, s // tm),
        in_specs=[pl.BlockSpec((None, tm, d), lambda bi, i: (bi, i, 0)),
                  pl.BlockSpec((1, d), lambda bi, i: (0, 0))],
        out_specs=pl.BlockSpec((None, tm, d), lambda bi, i: (bi, i, 0)),
        compiler_params=_params("parallel", "parallel"),
        name="final_norm",
    )(x, g.reshape(1, d))


def kernel(x, c, positions, mod_w, mod_b, norm_g, ffn1_w1, ffn1_w3, ffn1_w2, ffn2_w1, ffn2_w3, ffn2_w2, w_in, w_out, attn_gain, ssm_gain, ssm_a_re, ssm_a_im, ssm_log_dt, ssm_b_re, ssm_b_im, ssm_c_re, ssm_c_im, ssm_d, glu_w, glu_b, final_g):
    b, s, d = x.shape
    depth = mod_w.shape[0]
    ssm_w = d - ATTN_W
    groups = ssm_w // SSM_GROUP
    chunk = min(SCAN_L, s)
    n_chunks = s // chunk
    cast = lambda a: a.astype(MXU_DTYPE)

    mod = _mod_call(c, mod_w, mod_b)
    tables = _rope_tables(positions)
    ffn1 = (cast(ffn1_w1), cast(ffn1_w3), cast(ffn1_w2))
    ffn2 = (cast(ffn2_w1), cast(ffn2_w3), cast(ffn2_w2))
    o_k = ATTN_W
    o_v = o_k + HEAD_DIM
    o_qi = o_v + HEAD_DIM
    o_ki = o_qi + IDX_HEADS * IDX_DIM
    o_wi = o_ki + IDX_DIM
    o_u = o_wi + IDX_HEADS
    wq = cast(w_in[:, :, :o_k])
    wqi = cast(w_in[:, :, o_qi:o_ki])
    wm = jnp.concatenate([w_in[:, :, o_k:o_v], w_in[:, :, o_ki:o_wi], w_in[:, :, o_v:o_qi], w_in[:, :, o_wi:o_u],
                          jnp.zeros((depth, d, 2 * LANES - (2 * HEAD_DIM + IDX_DIM + IDX_HEADS)), w_in.dtype)],
                         axis=2)
    wm = cast(wm)
    wu = cast(w_in[:, :, o_u:])
    wo_a, wo_s = cast(w_out[:, :ATTN_W]), cast(w_out[:, ATTN_W:])
    glu_wc = cast(glu_w)
    toep, s_win, s_wout, apow = _ssm_operators(ssm_a_re, ssm_a_im, ssm_log_dt, ssm_b_re, ssm_b_im,
                                               ssm_c_re, ssm_c_im, ssm_d, chunk)

    for l in range(depth):
        x = _ffn_call(x, mod[l], norm_g[l, 0], *ffn1, l, 0)
        q, qi, misc, u = _inproj_call(x, mod[l], norm_g[l, 1], wq, wqi, wm, wu, tables, l)
        k = misc[..., :HEAD_DIM]
        ki = misc[..., HEAD_DIM:2 * HEAD_DIM]
        v = misc[..., 2 * HEAD_DIM:3 * HEAD_DIM]
        wi = misc[..., 3 * HEAD_DIM:3 * HEAD_DIM + IDX_HEADS]
        attn = _dsa_call(q, qi, wi, k, ki, v)
        u_g = u.reshape(b, n_chunks, chunk, groups, SSM_GROUP).transpose(3, 1, 0, 2, 4)
        u_g = u_g.reshape(groups, n_chunks * b, chunk * SSM_GROUP)
        y_g = _ssm_call(u_g, toep, s_win, s_wout, apow, l, b)
        y = y_g.reshape(groups, n_chunks, b, chunk, SSM_GROUP).transpose(2, 1, 3, 0, 4).reshape(b, s, ssm_w)
        x = _outproj_call(x, attn, y, mod[l], attn_gain, ssm_gain, glu_wc, glu_b, wo_a, wo_s, l)
        x = _ffn_call(x, mod[l], norm_g[l, 2], *ffn2, l, 6)
    return _final_norm_call(x, final_g)
```

```python
import functools
import math

import jax
import jax.numpy as jnp
from jax import lax
from jax.experimental import pallas as pl
from jax.experimental.pallas import tpu as pltpu

F32 = jnp.float32
MXU_DTYPE = jnp.bfloat16

CHUNK = 64
N_HEADS = 8
HEAD_DIM = 64
ATTN_W = N_HEADS * HEAD_DIM
IDX_HEADS = 4
IDX_DIM = 64
TOPK_MAX = 256
SSM_GROUP = 16
SSM_STATE = 64
SCAN_L = 128
ROPE_THETA = 500000.0
ROT_DIM = HEAD_DIM // 4
EPS = 1e-6
N_MOD = 9

LANES = 128
VMEM_LIMIT_BYTES = 56 * 1024 * 1024

INT_MIN = -(2 ** 31)
NEG_BIAS = -1e30

ROW_TILE = 512
Q_TILE = 256
K_TILE = 512
LOGIT_SAFE = 40.0
TAU_PROBE_BIT = 16
SSM_SUB = 16


def _params(*sem):
    return pltpu.CompilerParams(dimension_semantics=sem, vmem_limit_bytes=VMEM_LIMIT_BYTES)


def _dot(a, b):
    return jnp.dot(a, b, preferred_element_type=F32)


def _rms(x, g):
    return x * lax.rsqrt(jnp.mean(x * x, axis=-1, keepdims=True) + EPS) * g


def _modulated_norm(x, g, mod_ref, base):
    shift = mod_ref[base:base + 1, :]
    scale = mod_ref[base + 1:base + 2, :]
    return _rms(x, g) * (1.0 + scale) + shift


def _mod_kernel(c_ref, w_ref, b_ref, o_ref):
    c = c_ref[...]
    o_ref[...] = _dot(c * jax.nn.sigmoid(c), w_ref[...]) + b_ref[...]


def _mod_call(c, mod_w, mod_b):
    depth, d, nd = mod_w.shape
    b = c.shape[0]
    rows = -(-b // 8) * 8
    c_pad = jnp.zeros((rows, d), F32).at[:b].set(c)
    out = pl.pallas_call(
        _mod_kernel,
        out_shape=jax.ShapeDtypeStruct((depth, rows, nd), F32),
        grid=(depth, nd // d),
        in_specs=[pl.BlockSpec((rows, d), lambda l, j: (0, 0)),
                  pl.BlockSpec((None, d, d), lambda l, j: (l, 0, j)),
                  pl.BlockSpec((None, 1, d), lambda l, j: (l, 0, j))],
        out_specs=pl.BlockSpec((None, rows, d), lambda l, j: (l, 0, j)),
        compiler_params=_params("parallel", "parallel"),
        name="adaln_mod",
    )(c_pad, mod_w, mod_b.reshape(depth, 1, nd))
    return out[:, :b].reshape(depth, b, N_MOD, d)


def _ffn_kernel(x_ref, mod_ref, g_ref, w1_ref, w3_ref, w2_ref, o_ref, *, base, ff_chunk):
    x = x_ref[...]
    h = _modulated_norm(x, g_ref[...], mod_ref, base).astype(MXU_DTYPE)
    acc = jnp.zeros(x.shape, F32)
    for c0 in range(0, w1_ref.shape[1], ff_chunk):
        a = _dot(h, w1_ref[:, c0:c0 + ff_chunk])
        b = _dot(h, w3_ref[:, c0:c0 + ff_chunk])
        t = (a * jax.nn.sigmoid(a) * b).astype(MXU_DTYPE)
        acc = acc + _dot(t, w2_ref[c0:c0 + ff_chunk, :])
    o_ref[...] = x + 0.5 * mod_ref[base + 2:base + 3, :] * acc


def _resident(shape, index_map):
    return pl.BlockSpec(shape, index_map, pipeline_mode=pl.Buffered(1))


def _ffn_call(x, mod_l, g, w1, w3, w2, layer, base):
    b, s, d = x.shape
    ff = w1.shape[2]
    tm = min(ROW_TILE, s)
    ff_chunk = ff // 2 if (ff // 2) % LANES == 0 else ff
    wspec_in = _resident((None, d, ff), lambda bi, i: (layer, 0, 0))
    wspec_out = _resident((None, ff, d), lambda bi, i: (layer, 0, 0))
    return pl.pallas_call(
        functools.partial(_ffn_kernel, base=base, ff_chunk=ff_chunk),
        out_shape=jax.ShapeDtypeStruct(x.shape, F32),
        grid=(b, s // tm),
        in_specs=[pl.BlockSpec((None, tm, d), lambda bi, i: (bi, i, 0)),
                  pl.BlockSpec((None, N_MOD, d), lambda bi, i: (bi, 0, 0)),
                  pl.BlockSpec((1, d), lambda bi, i: (0, 0)),
                  wspec_in, wspec_in, wspec_out],
        out_specs=pl.BlockSpec((None, tm, d), lambda bi, i: (bi, i, 0)),
        compiler_params=_params("parallel", "parallel"),
        name="ffn",
    )(x, mod_l, g.reshape(1, d), w1, w3, w2)


def _rope128(t, cos, s_lo, s_hi):
    half = ROT_DIM // 2
    return t * cos + pltpu.roll(t, LANES - half, 1) * s_lo + pltpu.roll(t, half, 1) * s_hi


def _inproj_kernel(x_ref, mod_ref, g_ref, wq_ref, wqi_ref, wm_ref, wu_ref, cos_ref, slo_ref, shi_ref,
                   q_ref, qi_ref, m_ref, u_ref):
    h = _modulated_norm(x_ref[...], g_ref[...], mod_ref, 3).astype(MXU_DTYPE)
    cos, s_lo, s_hi = cos_ref[...], slo_ref[...], shi_ref[...]
    q = _dot(h, wq_ref[...])
    for j in range(0, q.shape[1], LANES):
        r = _rope128(q[:, j:j + LANES], cos, s_lo, s_hi)
        q_ref[:, j:j + LANES] = (r * HEAD_DIM ** -0.5).astype(q_ref.dtype)
    qi = _dot(h, wqi_ref[...])
    for j in range(0, qi.shape[1], LANES):
        qi_ref[:, j:j + LANES] = _rope128(qi[:, j:j + LANES], cos, s_lo, s_hi).astype(qi_ref.dtype)
    m = _dot(h, wm_ref[...])
    m_ref[:, :LANES] = _rope128(m[:, :LANES], cos, s_lo, s_hi)
    m_ref[:, LANES:] = m[:, LANES:]
    u_ref[...] = _dot(h, wu_ref[...]).astype(u_ref.dtype)


def _inproj_call(x, mod_l, g, wq, wqi, wm, wu, tables, layer):
    b, s, d = x.shape
    tm = min(ROW_TILE, s)
    row = lambda w: pl.BlockSpec((None, tm, w), lambda bi, i: (bi, i, 0))
    wres = lambda w: _resident((None, d, w), lambda bi, i: (layer, 0, 0))
    ssm_w = wu.shape[2]
    return pl.pallas_call(
        _inproj_kernel,
        out_shape=(jax.ShapeDtypeStruct((b, s, ATTN_W), MXU_DTYPE),
                   jax.ShapeDtypeStruct((b, s, IDX_HEADS * IDX_DIM), MXU_DTYPE),
                   jax.ShapeDtypeStruct((b, s, 2 * LANES), F32),
                   jax.ShapeDtypeStruct((b, s, ssm_w), MXU_DTYPE)),
        grid=(b, s // tm),
        in_specs=[row(d),
                  pl.BlockSpec((None, N_MOD, d), lambda bi, i: (bi, 0, 0)),
                  pl.BlockSpec((1, d), lambda bi, i: (0, 0)),
                  wres(ATTN_W), wres(IDX_HEADS * IDX_DIM), wres(2 * LANES), wres(ssm_w),
                  row(LANES), row(LANES), row(LANES)],
        out_specs=(row(ATTN_W), row(IDX_HEADS * IDX_DIM), row(2 * LANES), row(ssm_w)),
        compiler_params=_params("parallel", "parallel"),
        name="mixer_inproj",
    )(x, mod_l, g.reshape(1, d), wq, wqi, wm, wu, *tables)


def _rope_tables(positions):
    inv_freq = 1.0 / (ROPE_THETA ** (jnp.arange(0, ROT_DIM, 2, dtype=F32) / ROT_DIM))
    ang = positions.astype(F32)[..., None] * inv_freq
    cos, sin = jnp.cos(ang), jnp.sin(ang)
    half = ROT_DIM // 2
    rest = HEAD_DIM - ROT_DIM
    pad = lambda *parts: jnp.tile(jnp.concatenate(parts, axis=-1), (1, 1, LANES // HEAD_DIM))
    zeros = lambda n: jnp.zeros(cos.shape[:-1] + (n,), F32)
    ones = jnp.ones(cos.shape[:-1] + (rest,), F32)
    return (pad(cos, cos, ones),
            pad(-sin, zeros(half), zeros(rest)),
            pad(zeros(half), sin, zeros(rest)))


def _dsa_kernel(q_ref, qi_ref, w_ref, qn_ref, kn_ref, kt_ref, kit_ref, v_ref, o_ref,
                key_ref, bias_ref, m_ref, acc_ref, thr_ref, *, topk):
    tq = q_ref.shape[0]
    tk = kt_ref.shape[2]
    nslab = tk // LANES
    i = pl.program_id(1)
    n_tiles = ((i + 1) * tq + tk - 1) // tk
    row = lax.broadcasted_iota(jnp.int32, (tq, 1), 0) + i * tq
    q_lim = (row // CHUNK + 1) * CHUNK
    lane_tile = lax.broadcasted_iota(jnp.int32, (tq, tk), 1)

    w = w_ref[...] * (IDX_DIM ** -0.5 * IDX_HEADS ** -0.5)
    qi = qi_ref[...]
    qi_heads = [qi[:, h * IDX_DIM:(h + 1) * IDX_DIM] for h in range(IDX_HEADS)]

    def write_keys(t, masked):
        kit = kit_ref[t]
        s = jnp.zeros((tq, tk), F32)
        for h in range(IDX_HEADS):
            s = s + jnp.maximum(_dot(qi_heads[h], kit), 0.0) * w[:, h:h + 1]
        s = s + 0.0
        bits = pltpu.bitcast(s, jnp.int32)
        key = jnp.where(bits < 0, bits ^ jnp.int32(0x7FFFFFFF), bits)
        if masked:
            key = jnp.where(lane_tile + t * tk < q_lim, key, jnp.int32(INT_MIN))
        key_ref[t] = key

    def score_body(t, carry):
        write_keys(t, False)
        return carry

    lax.fori_loop(0, n_tiles - 1, score_body, 0)
    write_keys(n_tiles - 1, True)

    def count(thr, strict):
        thr_ref[...] = jnp.broadcast_to(thr, (tq, LANES))

        def body(t, acc):
            parts = []
            for r in range(0, tq, 8):
                th = thr_ref[r:r + 8, :]
                a = acc[r:r + 8]
                for j in range(nslab):
                    key = key_ref[t, r:r + 8, j * LANES:(j + 1) * LANES]
                    a = a + jnp.where(key > th if strict else key >= th, 1.0, 0.0)
                parts.append(a)
            return jnp.concatenate(parts, axis=0)

        acc = lax.fori_loop(0, n_tiles, body, jnp.zeros((tq, LANES), F32))
        return jnp.sum(acc, axis=1, keepdims=True)

    k_f = float(topk)
    n0 = count(jnp.zeros((tq, 1), jnp.int32), False)
    nonneg = n0 >= k_f
    tau0 = jnp.where(nonneg, jnp.int32(0), jnp.int32(INT_MIN))
    cnt0 = jnp.where(nonneg, n0, (n_tiles * tk).astype(F32))

    def refine(it, tau, cnt):
        trial = tau | lax.shift_left(jnp.int32(1), 30 - it)
        c = count(trial, False)
        ok = c >= k_f
        return jnp.where(ok, trial, tau), jnp.where(ok, c, cnt)

    tau, cnt = lax.fori_loop(0, TAU_PROBE_BIT, lambda it, tc: refine(it, *tc), (tau0, cnt0))
    locked = count(tau + 1, False) < k_f

    def unsettled(tau_cnt):
        return jnp.max(jnp.where(locked | (tau_cnt[1] == k_f), 0.0, 1.0))

    def tail_cond(carry):
        it, _, _, open_rows = carry
        return jnp.logical_and(it < 31, open_rows > 0.0)

    def tail_body(carry):
        it, tau, cnt, _ = carry
        tau, cnt = refine(it, tau, cnt)
        return it + 1, tau, cnt, unsettled((tau, cnt))

    _, tau, _, _ = lax.while_loop(tail_cond, tail_body,
                                  (jnp.int32(TAU_PROBE_BIT), tau, cnt, unsettled((tau, cnt))))
    need = k_f - count(tau, True)

    upper = (lax.broadcasted_iota(jnp.int32, (tk, tk), 0)
             <= lax.broadcasted_iota(jnp.int32, (tk, tk), 1)).astype(MXU_DTYPE)

    def bias_body(t, ties_before):
        key = key_ref[t]
        tie = jnp.where(key == tau, 1.0, 0.0)
        rank = ties_before + _dot(tie.astype(MXU_DTYPE), upper)
        sel = (key > tau) | ((key == tau) & (rank <= need))
        sel = sel & (key > jnp.int32(INT_MIN))
        bias_ref[t] = jnp.where(sel, 0.0, NEG_BIAS)
        return ties_before + jnp.sum(tie, axis=1, keepdims=True)

    lax.fori_loop(0, n_tiles, bias_body, jnp.zeros((tq, 1), F32))

    q = q_ref[...]
    q_all = jnp.concatenate([q[:, h * HEAD_DIM:(h + 1) * HEAD_DIM] for h in range(N_HEADS)], axis=0)
    rows = N_HEADS * tq

    def logits(t):
        return _dot(q_all, kt_ref[t]).reshape(N_HEADS, tq, tk) + bias_ref[t][None]

    safe = jnp.max(qn_ref[...] * kn_ref[...]) <= LOGIT_SAFE * LOGIT_SAFE

    @pl.when(safe)
    def _():
        m_ref[...] = jnp.zeros(m_ref.shape, F32)

    @pl.when(jnp.logical_not(safe))
    def _():
        m_ref[...] = jnp.full(m_ref.shape, NEG_BIAS, F32)

        def max_body(t, carry):
            s = logits(t)
            mx = s[:, :, :LANES]
            for j in range(1, nslab):
                mx = jnp.maximum(mx, s[:, :, j * LANES:(j + 1) * LANES])
            m_ref[...] = jnp.maximum(m_ref[...], mx.reshape(rows, LANES))
            return carry

        lax.fori_loop(0, n_tiles, max_body, 0)
        m_ref[...] = jnp.broadcast_to(jnp.max(m_ref[...], axis=1, keepdims=True), (rows, LANES))

    acc_ref[...] = jnp.zeros(acc_ref.shape, F32)

    def att_body(t, carry):
        m_all = jnp.concatenate([m_ref[...]] * nslab, axis=1).reshape(N_HEADS, tq, tk)
        p = jnp.exp(logits(t) - m_all)
        acc_ref[...] += _dot(p.astype(MXU_DTYPE).reshape(rows, tk), v_ref[t])
        return carry

    lax.fori_loop(0, n_tiles, att_body, 0)
    acc = acc_ref[...]
    out = acc[:, :HEAD_DIM] / acc[:, HEAD_DIM:HEAD_DIM + 1]
    for h in range(N_HEADS):
        o_ref[:, h * HEAD_DIM:(h + 1) * HEAD_DIM] = out[h * tq:(h + 1) * tq]


def _dsa_call(q, qi, w, k, ki, v):
    b, s, _ = q.shape
    tq = min(Q_TILE, s)
    tk = min(K_TILE, s)
    n_kt = s // tk
    topk = min(TOPK_MAX, s // 4)
    to_tiles_t = lambda a: a.astype(MXU_DTYPE).reshape(b, n_kt, tk, a.shape[-1]).transpose(0, 1, 3, 2)
    kt, kit = to_tiles_t(k), to_tiles_t(ki)
    v_aug = jnp.concatenate([v, jnp.ones((b, s, 1), v.dtype), jnp.zeros((b, s, LANES - HEAD_DIM - 1), v.dtype)],
                            axis=-1)
    vt = v_aug.astype(MXU_DTYPE).reshape(b, n_kt, tk, LANES)
    assert tk % tq == 0
    qf = q.astype(F32).reshape(b, s // tq, tq, N_HEADS, HEAD_DIM)
    qn2 = jnp.max(jnp.sum(qf * qf, axis=-1), axis=(2, 3)).reshape(b, s // tq, 1, 1)
    kf = k.astype(MXU_DTYPE).astype(F32)
    kn2 = jnp.max(jnp.sum(kf * kf, axis=-1), axis=1).reshape(b, 1, 1)
    per_batch = lambda shape: pl.BlockSpec((None,) + shape, lambda bi, i: (bi, 0, 0, 0))
    qrow = lambda width: pl.BlockSpec((None, tq, width), lambda bi, i: (bi, i, 0))
    return pl.pallas_call(
        functools.partial(_dsa_kernel, topk=topk),
        out_shape=jax.ShapeDtypeStruct((b, s, ATTN_W), F32),
        grid=(b, s // tq),
        in_specs=[qrow(ATTN_W), qrow(IDX_HEADS * IDX_DIM), qrow(IDX_HEADS),
                  pl.BlockSpec((None, None, 1, 1), lambda bi, i: (bi, i, 0, 0)),
                  pl.BlockSpec((None, 1, 1), lambda bi, i: (bi, 0, 0)),
                  per_batch((n_kt, HEAD_DIM, tk)), per_batch((n_kt, IDX_DIM, tk)),
                  per_batch((n_kt, tk, LANES))],
        out_specs=qrow(ATTN_W),
        scratch_shapes=[pltpu.VMEM((n_kt, tq, tk), jnp.int32),
                        pltpu.VMEM((n_kt, tq, tk), F32),
                        pltpu.VMEM((N_HEADS * tq, LANES), F32),
                        pltpu.VMEM((N_HEADS * tq, LANES), F32),
                        pltpu.VMEM((tq, LANES), jnp.int32)],
        compiler_params=_params("parallel", "parallel"),
        name="dsa_attention",
    )(q, qi, w, qn2, kn2, kt, kit, vt)


def _ssm_kernel(u_ref, toep_ref, win_ref, wout_ref, apow_ref, y_ref, hloc_ref, hprev_ref, *, batch):
    rows, width = u_ref.shape
    sub = toep_ref.shape[1]
    n_sub = width // sub
    hloc_ref[...] = _dot(u_ref[...], win_ref[...])
    a_same, a_swap = apow_ref[0:1, :], apow_ref[1:2, :]
    h = jnp.zeros((batch, 2 * SSM_STATE), F32)
    for c in range(rows // batch):
        hprev_ref[c * batch:(c + 1) * batch, :] = h
        h = a_same * h + a_swap * pltpu.roll(h, SSM_STATE, 1) + hloc_ref[c * batch:(c + 1) * batch, :]
    hprev = hprev_ref[...].astype(MXU_DTYPE)
    for tb in range(n_sub):
        acc = _dot(hprev, wout_ref[:, tb * sub:(tb + 1) * sub])
        for sb in range(tb + 1):
            acc = acc + _dot(u_ref[:, sb * sub:(sb + 1) * sub], toep_ref[tb - sb])
        y_ref[:, tb * sub:(tb + 1) * sub] = acc


def _ssm_call(u_g, toep, win, wout, apow, layer, batch):
    g, rows, width = u_g.shape
    n_sub, sub = toep.shape[2], toep.shape[3]
    per_group = lambda *shape: pl.BlockSpec((None, None) + shape, lambda gi: (layer, gi) + (0,) * len(shape))
    return pl.pallas_call(
        functools.partial(_ssm_kernel, batch=batch),
        out_shape=jax.ShapeDtypeStruct((g, rows, width), F32),
        grid=(g,),
        in_specs=[pl.BlockSpec((None, rows, width), lambda gi: (gi, 0, 0)),
                  per_group(n_sub, sub, sub), per_group(width, 2 * SSM_STATE),
                  per_group(2 * SSM_STATE, width), per_group(2, 2 * SSM_STATE)],
        out_specs=pl.BlockSpec((None, rows, width), lambda gi: (gi, 0, 0)),
        scratch_shapes=[pltpu.VMEM((rows, 2 * SSM_STATE), F32), pltpu.VMEM((rows, 2 * SSM_STATE), F32)],
        compiler_params=_params("parallel"),
        name="s5_scan",
    )(u_g, toep, win, wout, apow)


def _ssm_operators(a_re, a_im, log_dt, b_re, b_im, c_re, c_im, d_skip, chunk):
    cmul = lambda xr, xi, yr, yi: (xr * yr - xi * yi, xr * yi + xi * yr)
    dt = jnp.exp(log_dt)[..., None]
    lam_re, lam_im = a_re * dt, a_im * dt
    lags = jnp.arange(chunk + 1, dtype=F32)[None, None, :, None]
    mag = jnp.exp(lam_re[:, :, None, :] * lags)
    ang = lam_im[:, :, None, :] * lags
    ap_re, ap_im = mag * jnp.cos(ang), mag * jnp.sin(ang)
    num_re, num_im = ap_re[:, :, 1] - 1.0, ap_im[:, :, 1]
    den = a_re * a_re + a_im * a_im
    coef_re, coef_im = (num_re * a_re + num_im * a_im) / den, (num_im * a_re - num_re * a_im) / den
    bb_re, bb_im = cmul(coef_re[..., None], coef_im[..., None], b_re, b_im)
    depth, g, p = a_re.shape
    gc = b_re.shape[-1]
    ca_re, ca_im = cmul(c_re[:, :, None], c_im[:, :, None], ap_re[:, :, :chunk, None, :], ap_im[:, :, :chunk, None, :])
    taps = (jnp.einsum('dgtop,dgpi->dgtoi', ca_re, bb_re, precision=lax.Precision.HIGHEST)
            - jnp.einsum('dgtop,dgpi->dgtoi', ca_im, bb_im, precision=lax.Precision.HIGHEST))
    taps = taps.at[:, :, 0].add(d_skip[..., None] * jnp.eye(gc, dtype=F32))
    n_sub = chunk // SSM_SUB
    taps = jnp.concatenate([jnp.zeros((depth, g, SSM_SUB - 1, gc, gc), F32), taps], axis=2)
    blk = jnp.arange(n_sub)[:, None, None] * SSM_SUB
    lag_idx = blk + jnp.arange(SSM_SUB)[None, None, :] - jnp.arange(SSM_SUB)[None, :, None] + (SSM_SUB - 1)
    toep = taps[:, :, lag_idx]
    toep = toep.transpose(0, 1, 2, 3, 6, 4, 5).reshape(depth, g, n_sub, SSM_SUB * gc, SSM_SUB * gc)
    rev_re, rev_im = ap_re[:, :, chunk - 1::-1][:, :, :chunk], ap_im[:, :, chunk - 1::-1][:, :, :chunk]
    wi_re, wi_im = cmul(rev_re[..., None], rev_im[..., None], bb_re[:, :, None], bb_im[:, :, None])
    w_in = jnp.concatenate([wi_re, wi_im], axis=3)
    w_in = w_in.transpose(0, 1, 2, 4, 3).reshape(depth, g, chunk * gc, 2 * p)
    wo_re, wo_im = cmul(c_re[:, :, None], c_im[:, :, None], ap_re[:, :, 1:, None, :], ap_im[:, :, 1:, None, :])
    w_out = jnp.concatenate([wo_re, -wo_im], axis=4)
    w_out = w_out.transpose(0, 1, 4, 2, 3).reshape(depth, g, 2 * p, chunk * gc)
    al_re, al_im = ap_re[:, :, chunk], ap_im[:, :, chunk]
    apow = jnp.stack([jnp.concatenate([al_re, al_re], -1),
                      jnp.concatenate([-al_im, al_im], -1)], axis=2)
    return toep.astype(MXU_DTYPE), w_in.astype(MXU_DTYPE), w_out.astype(MXU_DTYPE), apow


def _outproj_kernel(x_ref, attn_ref, y_ref, mod_ref, ag_ref, sg_ref, gw_ref, gb_ref, woa_ref, wos_ref, o_ref):
    a = _rms(attn_ref[...], ag_ref[...]).astype(MXU_DTYPE)
    y = y_ref[...]
    y = 0.5 * y * (1.0 + jnp.tanh(math.sqrt(2.0 / math.pi) * (y + 0.044715 * (y * y * y))))
    y = y * jax.nn.sigmoid(_dot(y.astype(MXU_DTYPE), gw_ref[...]) + gb_ref[...])
    y = _rms(y, sg_ref[...]).astype(MXU_DTYPE)
    mixed = _dot(a, woa_ref[...]) + _dot(y, wos_ref[...])
    o_ref[...] = x_ref[...] + mod_ref[5:6, :] * mixed


def _outproj_call(x, attn, y, mod_l, attn_gain, ssm_gain, glu_w, glu_b, wo_a, wo_s, layer):
    b, s, d = x.shape
    tm = min(ROW_TILE, s)
    aw, sw = attn.shape[2], y.shape[2]
    row = lambda w: pl.BlockSpec((None, tm, w), lambda bi, i: (bi, i, 0))
    vec = lambda w: pl.BlockSpec((None, 1, w), lambda bi, i: (layer, 0, 0))
    wres = lambda r, c: _resident((None, r, c), lambda bi, i: (layer, 0, 0))
    return pl.pallas_call(
        _outproj_kernel,
        out_shape=jax.ShapeDtypeStruct(x.shape, F32),
        grid=(b, s // tm),
        in_specs=[row(d), row(aw), row(sw),
                  pl.BlockSpec((None, N_MOD, d), lambda bi, i: (bi, 0, 0)),
                  vec(aw), vec(sw), wres(sw, sw), vec(sw), wres(aw, d), wres(sw, d)],
        out_specs=row(d),
        compiler_params=_params("parallel", "parallel"),
        name="mixer_outproj",
    )(x, attn, y, mod_l, attn_gain[:, None, :], ssm_gain[:, None, :], glu_w, glu_b[:, None, :], wo_a, wo_s)


def _final_norm_kernel(x_ref, g_ref, o_ref):
    o_ref[...] = _rms(x_ref[...], g_ref[...])


def _final_norm_call(x, g):
    b, s, d = x.shape
    tm = min(ROW_TILE, s)
    return pl.pallas_call(
        _final_norm_kernel,
        out_shape=jax.ShapeDtypeStruct(x.shape, F32),
        grid=(b, s // tm),
        in_specs=[pl.BlockSpec((None, tm, d), lambda bi, i: (bi, i, 0)),
                  pl.BlockSpec((1, d), lambda bi, i: (0, 0))],
        out_specs=pl.BlockSpec((None, tm, d), lambda bi, i: (bi, i, 0)),
        compiler_params=_params("parallel", "parallel"),
        name="final_norm",
    )(x, g.reshape(1, d))


def kernel(x, c, positions, mod_w, mod_b, norm_g, ffn1_w1, ffn1_w3, ffn1_w2, ffn2_w1, ffn2_w3, ffn2_w2, w_in, w_out, attn_gain, ssm_gain, ssm_a_re, ssm_a_im, ssm_log_dt, ssm_b_re, ssm_b_im, ssm_c_re, ssm_c_im, ssm_d, glu_w, glu_b, final_g):
    b, s, d = x.shape
    depth = mod_w.shape[0]
    ssm_w = d - ATTN_W
    groups = ssm_w // SSM_GROUP
    chunk = min(SCAN_L, s)
    n_chunks = s // chunk
    cast = lambda a: a.astype(MXU_DTYPE)

    mod = _mod_call(c, mod_w, mod_b)
    tables = _rope_tables(positions)
    ffn1 = (cast(ffn1_w1), cast(ffn1_w3), cast(ffn1_w2))
    ffn2 = (cast(ffn2_w1), cast(ffn2_w3), cast(ffn2_w2))
    o_k = ATTN_W
    o_v = o_k + HEAD_DIM
    o_qi = o_v + HEAD_DIM
    o_ki = o_qi + IDX_HEADS * IDX_DIM
    o_wi = o_ki + IDX_DIM
    o_u = o_wi + IDX_HEADS
    wq = cast(w_in[:, :, :o_k])
    wqi = cast(w_in[:, :, o_qi:o_ki])
    wm = jnp.concatenate([w_in[:, :, o_k:o_v], w_in[:, :, o_ki:o_wi], w_in[:, :, o_v:o_qi], w_in[:, :, o_wi:o_u],
                          jnp.zeros((depth, d, 2 * LANES - (2 * HEAD_DIM + IDX_DIM + IDX_HEADS)), w_in.dtype)],
                         axis=2)
    wm = cast(wm)
    wu = cast(w_in[:, :, o_u:])
    wo_a, wo_s = cast(w_out[:, :ATTN_W]), cast(w_out[:, ATTN_W:])
    glu_wc = cast(glu_w)
    toep, s_win, s_wout, apow = _ssm_operators(ssm_a_re, ssm_a_im, ssm_log_dt, ssm_b_re, ssm_b_im,
                                               ssm_c_re, ssm_c_im, ssm_d, chunk)

    for l in range(depth):
        x = _ffn_call(x, mod[l], norm_g[l, 0], *ffn1, l, 0)
        q, qi, misc, u = _inproj_call(x, mod[l], norm_g[l, 1], wq, wqi, wm, wu, tables, l)
        k = misc[..., :HEAD_DIM]
        ki = misc[..., HEAD_DIM:2 * HEAD_DIM]
        v = misc[..., 2 * HEAD_DIM:3 * HEAD_DIM]
        wi = misc[..., 3 * HEAD_DIM:3 * HEAD_DIM + IDX_HEADS]
        attn = _dsa_call(q, qi, wi, k, ki, v)
        u_g = u.reshape(b, n_chunks, chunk, groups, SSM_GROUP).transpose(3, 1, 0, 2, 4)
        u_g = u_g.reshape(groups, n_chunks * b, chunk * SSM_GROUP)
        y_g = _ssm_call(u_g, toep, s_win, s_wout, apow, l, b)
        y = y_g.reshape(groups, n_chunks, b, chunk, SSM_GROUP).transpose(2, 1, 3, 0, 4).reshape(b, s, ssm_w)
        x = _outproj_call(x, attn, y, mod[l], attn_gain, ssm_gain, glu_wc, glu_b, wo_a, wo_s, l)
        x = _ffn_call(x, mod[l], norm_g[l, 2], *ffn2, l, 6)
    return _final_norm_call(x, final_g)
```

```python
import functools
import math

import jax
import jax.numpy as jnp
import numpy as np
from jax import lax
from jax.experimental import pallas as pl
from jax.experimental.pallas import tpu as pltpu

F32 = jnp.float32
MXU_DTYPE = jnp.bfloat16

CHUNK = 64
N_HEADS = 8
HEAD_DIM = 64
ATTN_W = N_HEADS * HEAD_DIM
IDX_HEADS = 4
IDX_DIM = 64
TOPK_MAX = 256
SSM_GROUP = 16
SSM_STATE = 64
SCAN_L = 128
ROPE_THETA = 500000.0
ROT_DIM = HEAD_DIM // 4
EPS = 1e-6
N_MOD = 9

LANES = 128
VMEM_LIMIT_BYTES = 56 * 1024 * 1024

INT_MIN = -(2 ** 31)
NEG_BIAS = -1e30

ROW_TILE = 512
Q_TILE = 256
K_TILE = 512
LOGIT_SAFE = 40.0
TAU_PROBE_BIT = 16
SSM_SUB = 16


def _params(*sem):
    return pltpu.CompilerParams(dimension_semantics=sem, vmem_limit_bytes=VMEM_LIMIT_BYTES)


def _dot(a, b):
    return jnp.dot(a, b, preferred_element_type=F32)


def _rms(x, g):
    return x * lax.rsqrt(jnp.mean(x * x, axis=-1, keepdims=True) + EPS) * g


def _modulated_norm(x, g, mod_ref, base):
    shift = mod_ref[base:base + 1, :]
    scale = mod_ref[base + 1:base + 2, :]
    return _rms(x, g) * (1.0 + scale) + shift


def _mod_kernel(c_ref, w_ref, b_ref, o_ref):
    c = c_ref[...]
    o_ref[...] = _dot(c * jax.nn.sigmoid(c), w_ref[...]) + b_ref[...]


def _mod_call(c, mod_w, mod_b):
    depth, d, nd = mod_w.shape
    b = c.shape[0]
    rows = -(-b // 8) * 8
    c_pad = jnp.zeros((rows, d), F32).at[:b].set(c)
    out = pl.pallas_call(
        _mod_kernel,
        out_shape=jax.ShapeDtypeStruct((depth, rows, nd), F32),
        grid=(depth, nd // d),
        in_specs=[pl.BlockSpec((rows, d), lambda l, j: (0, 0)),
                  pl.BlockSpec((None, d, d), lambda l, j: (l, 0, j)),
                  pl.BlockSpec((None, 1, d), lambda l, j: (l, 0, j))],
        out_specs=pl.BlockSpec((None, rows, d), lambda l, j: (l, 0, j)),
        compiler_params=_params("parallel", "parallel"),
        name="adaln_mod",
    )(c_pad, mod_w, mod_b.reshape(depth, 1, nd))
    return out[:, :b].reshape(depth, b, N_MOD, d)


def _ffn_kernel(x_ref, mod_ref, g_ref, w1_ref, w3_ref, w2_ref, o_ref, *, base, ff_chunk, final_norm):
    x = x_ref[...]
    h = _modulated_norm(x, g_ref[0:1, :], mod_ref, base).astype(MXU_DTYPE)
    acc = jnp.zeros(x.shape, F32)
    for c0 in range(0, w1_ref.shape[1], ff_chunk):
        a = _dot(h, w1_ref[:, c0:c0 + ff_chunk])
        b = _dot(h, w3_ref[:, c0:c0 + ff_chunk])
        t = (a * jax.nn.sigmoid(a) * b).astype(MXU_DTYPE)
        acc = acc + _dot(t, w2_ref[c0:c0 + ff_chunk, :])
    y = x + 0.5 * mod_ref[base + 2:base + 3, :] * acc
    o_ref[...] = _rms(y, g_ref[1:2, :]) if final_norm else y


def _resident(shape, index_map):
    return pl.BlockSpec(shape, index_map, pipeline_mode=pl.Buffered(1))


def _ffn_call(x, mod_l, g, w1, w3, w2, layer, base, final_g=None):
    b, s, d = x.shape
    gains = jnp.stack([g, g if final_g is None else final_g])
    ff = w1.shape[2]
    tm = min(ROW_TILE, s)
    ff_chunk = ff // 2 if (ff // 2) % LANES == 0 else ff
    wspec_in = _resident((None, d, ff), lambda bi, i: (layer, 0, 0))
    wspec_out = _resident((None, ff, d), lambda bi, i: (layer, 0, 0))
    return pl.pallas_call(
        functools.partial(_ffn_kernel, base=base, ff_chunk=ff_chunk, final_norm=final_g is not None),
        out_shape=jax.ShapeDtypeStruct(x.shape, F32),
        grid=(b, s // tm),
        in_specs=[pl.BlockSpec((None, tm, d), lambda bi, i: (bi, i, 0)),
                  pl.BlockSpec((None, N_MOD, d), lambda bi, i: (bi, 0, 0)),
                  pl.BlockSpec((2, d), lambda bi, i: (0, 0)),
                  wspec_in, wspec_in, wspec_out],
        out_specs=pl.BlockSpec((None, tm, d), lambda bi, i: (bi, i, 0)),
        compiler_params=_params("parallel", "parallel"),
        name="ffn",
    )(x, mod_l, gains, w1, w3, w2)


def _rope128(t, cos, s_lo, s_hi):
    half = ROT_DIM // 2
    return t * cos + pltpu.roll(t, LANES - half, 1) * s_lo + pltpu.roll(t, half, 1) * s_hi


def _inproj_kernel(x_ref, mod_ref, g_ref, wq_ref, wqi_ref, wm_ref, wu_ref, cos_ref, slo_ref, shi_ref,
                   q_ref, qi_ref, kk_ref, va_ref, wv_ref, u_ref):
    h = _modulated_norm(x_ref[...], g_ref[...], mod_ref, 3).astype(MXU_DTYPE)
    cos, s_lo, s_hi = cos_ref[...], slo_ref[...], shi_ref[...]
    q = _dot(h, wq_ref[...])
    for j in range(0, q.shape[1], LANES):
        r = _rope128(q[:, j:j + LANES], cos, s_lo, s_hi)
        q_ref[:, j:j + LANES] = (r * HEAD_DIM ** -0.5).astype(q_ref.dtype)
    qi = _dot(h, wqi_ref[...])
    for j in range(0, qi.shape[1], LANES):
        qi_ref[:, j:j + LANES] = _rope128(qi[:, j:j + LANES], cos, s_lo, s_hi).astype(qi_ref.dtype)
    m = _dot(h, wm_ref[...])
    kk_ref[...] = _rope128(m[:, :LANES], cos, s_lo, s_hi).astype(kk_ref.dtype)
    vw = m[:, LANES:]
    wv_ref[...] = vw
    lane = lax.broadcasted_iota(jnp.int32, vw.shape, 1)
    va_ref[...] = jnp.where(lane < HEAD_DIM, vw, jnp.where(lane == HEAD_DIM, 1.0, 0.0)).astype(va_ref.dtype)
    u_ref[...] = _dot(h, wu_ref[...]).astype(u_ref.dtype)


def _inproj_call(x, mod_l, g, wq, wqi, wm, wu, tables, layer):
    b, s, d = x.shape
    tm = min(ROW_TILE, s)
    row = lambda w: pl.BlockSpec((None, tm, w), lambda bi, i: (bi, i, 0))
    wres = lambda w: _resident((None, d, w), lambda bi, i: (layer, 0, 0))
    ssm_w = wu.shape[2]
    return pl.pallas_call(
        _inproj_kernel,
        out_shape=(jax.ShapeDtypeStruct((b, s, ATTN_W), MXU_DTYPE),
                   jax.ShapeDtypeStruct((b, s, IDX_HEADS * IDX_DIM), MXU_DTYPE),
                   jax.ShapeDtypeStruct((b, s, LANES), MXU_DTYPE),
                   jax.ShapeDtypeStruct((b, s, LANES), MXU_DTYPE),
                   jax.ShapeDtypeStruct((b, s, LANES), F32),
                   jax.ShapeDtypeStruct((b, s, ssm_w), MXU_DTYPE)),
        grid=(b, s // tm),
        in_specs=[row(d),
                  pl.BlockSpec((None, N_MOD, d), lambda bi, i: (bi, 0, 0)),
                  pl.BlockSpec((1, d), lambda bi, i: (0, 0)),
                  wres(ATTN_W), wres(IDX_HEADS * IDX_DIM), wres(2 * LANES), wres(ssm_w),
                  row(LANES), row(LANES), row(LANES)],
        out_specs=(row(ATTN_W), row(IDX_HEADS * IDX_DIM), row(LANES), row(LANES), row(LANES), row(ssm_w)),
        compiler_params=_params("parallel", "parallel"),
        name="mixer_inproj",
    )(x, mod_l, g.reshape(1, d), wq, wqi, wm, wu, *tables)


def _rope_tables(positions):
    inv_freq = 1.0 / (ROPE_THETA ** (jnp.arange(0, ROT_DIM, 2, dtype=F32) / ROT_DIM))
    ang = positions.astype(F32)[..., None] * inv_freq
    cos, sin = jnp.cos(ang), jnp.sin(ang)
    half = ROT_DIM // 2
    rest = HEAD_DIM - ROT_DIM
    pad = lambda *parts: jnp.tile(jnp.concatenate(parts, axis=-1), (1, 1, LANES // HEAD_DIM))
    zeros = lambda n: jnp.zeros(cos.shape[:-1] + (n,), F32)
    ones = jnp.ones(cos.shape[:-1] + (rest,), F32)
    return (pad(cos, cos, ones),
            pad(-sin, zeros(half), zeros(rest)),
            pad(zeros(half), sin, zeros(rest)))


def _dsa_kernel(q_ref, qi_ref, wv_ref, qn_ref, kn_ref, kk_ref, v_ref, o_ref,
                key_ref, bias_ref, m_ref, acc_ref, thr_ref, *, topk):
    tq = q_ref.shape[0]
    tk = kk_ref.shape[2]
    nslab = tk // LANES
    i = pl.program_id(1)
    n_tiles = ((i + 1) * tq + tk - 1) // tk
    row = lax.broadcasted_iota(jnp.int32, (tq, 1), 0) + i * tq
    q_lim = (row // CHUNK + 1) * CHUNK
    lane_tile = lax.broadcasted_iota(jnp.int32, (tq, tk), 1)

    w = wv_ref[:, HEAD_DIM:HEAD_DIM + IDX_HEADS] * (IDX_DIM ** -0.5 * IDX_HEADS ** -0.5)
    qi = qi_ref[...]
    qi_heads = [qi[:, h * IDX_DIM:(h + 1) * IDX_DIM] for h in range(IDX_HEADS)]

    def write_keys(t, masked):
        kit = kk_ref[t, HEAD_DIM:, :]
        s = jnp.zeros((tq, tk), F32)
        for h in range(IDX_HEADS):
            s = s + jnp.maximum(_dot(qi_heads[h], kit), 0.0) * w[:, h:h + 1]
        s = s + 0.0
        bits = pltpu.bitcast(s, jnp.int32)
        key = jnp.where(bits < 0, bits ^ jnp.int32(0x7FFFFFFF), bits)
        if masked:
            key = jnp.where(lane_tile + t * tk < q_lim, key, jnp.int32(INT_MIN))
        key_ref[t] = key

    def score_body(t, carry):
        write_keys(t, False)
        return carry

    lax.fori_loop(0, n_tiles - 1, score_body, 0)
    write_keys(n_tiles - 1, True)

    def count(thr, strict):
        thr_ref[...] = jnp.broadcast_to(thr, (tq, LANES))

        def body(t, acc):
            parts = []
            for r in range(0, tq, 8):
                th = thr_ref[r:r + 8, :]
                a = acc[r:r + 8]
                for j in range(nslab):
                    key = key_ref[t, r:r + 8, j * LANES:(j + 1) * LANES]
                    a = a + jnp.where(key > th if strict else key >= th, 1.0, 0.0)
                parts.append(a)
            return jnp.concatenate(parts, axis=0)

        acc = lax.fori_loop(0, n_tiles, body, jnp.zeros((tq, LANES), F32))
        return jnp.sum(acc, axis=1, keepdims=True)

    k_f = float(topk)
    n0 = count(jnp.zeros((tq, 1), jnp.int32), False)
    nonneg = n0 >= k_f
    tau0 = jnp.where(nonneg, jnp.int32(0), jnp.int32(INT_MIN))
    cnt0 = jnp.where(nonneg, n0, (n_tiles * tk).astype(F32))

    def refine(it, tau, cnt):
        trial = tau | lax.shift_left(jnp.int32(1), 30 - it)
        c = count(trial, False)
        ok = c >= k_f
        return jnp.where(ok, trial, tau), jnp.where(ok, c, cnt)

    tau, cnt = lax.fori_loop(0, TAU_PROBE_BIT, lambda it, tc: refine(it, *tc), (tau0, cnt0))
    locked = count(tau + 1, False) < k_f

    def unsettled(tau_cnt):
        return jnp.max(jnp.where(locked | (tau_cnt[1] == k_f), 0.0, 1.0))

    def tail_cond(carry):
        it, _, _, open_rows = carry
        return jnp.logical_and(it < 31, open_rows > 0.0)

    def tail_body(carry):
        it, tau, cnt, _ = carry
        tau, cnt = refine(it, tau, cnt)
        return it + 1, tau, cnt, unsettled((tau, cnt))

    _, tau, _, _ = lax.while_loop(tail_cond, tail_body,
                                  (jnp.int32(TAU_PROBE_BIT), tau, cnt, unsettled((tau, cnt))))
    need = k_f - count(tau, True)

    upper = (lax.broadcasted_iota(jnp.int32, (tk, tk), 0)
             <= lax.broadcasted_iota(jnp.int32, (tk, tk), 1)).astype(MXU_DTYPE)

    def bias_body(t, ties_before):
        key = key_ref[t]
        tie = jnp.where(key == tau, 1.0, 0.0)
        rank = ties_before + _dot(tie.astype(MXU_DTYPE), upper)
        sel = (key > tau) | ((key == tau) & (rank <= need))
        sel = sel & (key > jnp.int32(INT_MIN))
        bias_ref[t] = jnp.where(sel, 0.0, NEG_BIAS)
        return ties_before + jnp.sum(tie, axis=1, keepdims=True)

    lax.fori_loop(0, n_tiles, bias_body, jnp.zeros((tq, 1), F32))

    q = q_ref[...]
    q_all = jnp.concatenate([q[:, h * HEAD_DIM:(h + 1) * HEAD_DIM] for h in range(N_HEADS)], axis=0)
    rows = N_HEADS * tq

    def logits(t):
        return _dot(q_all, kk_ref[t, :HEAD_DIM, :]).reshape(N_HEADS, tq, tk) + bias_ref[t][None]

    safe = jnp.max(qn_ref[...] * kn_ref[...]) <= LOGIT_SAFE * LOGIT_SAFE

    @pl.when(safe)
    def _():
        m_ref[...] = jnp.zeros(m_ref.shape, F32)

    @pl.when(jnp.logical_not(safe))
    def _():
        m_ref[...] = jnp.full(m_ref.shape, NEG_BIAS, F32)

        def max_body(t, carry):
            s = logits(t)
            mx = s[:, :, :LANES]
            for j in range(1, nslab):
                mx = jnp.maximum(mx, s[:, :, j * LANES:(j + 1) * LANES])
            m_ref[...] = jnp.maximum(m_ref[...], mx.reshape(rows, LANES))
            return carry

        lax.fori_loop(0, n_tiles, max_body, 0)
        m_ref[...] = jnp.broadcast_to(jnp.max(m_ref[...], axis=1, keepdims=True), (rows, LANES))

    acc_ref[...] = jnp.zeros(acc_ref.shape, F32)

    def att_body(t, carry):
        m_all = jnp.concatenate([m_ref[...]] * nslab, axis=1).reshape(N_HEADS, tq, tk)
        p = jnp.exp(logits(t) - m_all)
        acc_ref[...] += _dot(p.astype(MXU_DTYPE).reshape(rows, tk), v_ref[t])
        return carry

    lax.fori_loop(0, n_tiles, att_body, 0)
    acc = acc_ref[...]
    out = acc[:, :HEAD_DIM] / acc[:, HEAD_DIM:HEAD_DIM + 1]
    for h in range(N_HEADS):
        o_ref[:, h * HEAD_DIM:(h + 1) * HEAD_DIM] = out[h * tq:(h + 1) * tq].astype(o_ref.dtype)


def _dsa_call(q, qi, wv, kk, v_aug):
    b, s, _ = q.shape
    tq = min(Q_TILE, s)
    tk = min(K_TILE, s)
    n_kt = s // tk
    topk = min(TOPK_MAX, s // 4)
    kkt = kk.reshape(b, n_kt, tk, LANES).transpose(0, 1, 3, 2)
    vt = v_aug.reshape(b, n_kt, tk, LANES)
    assert tk % tq == 0
    qf = q.astype(F32).reshape(b, s // tq, tq, N_HEADS, HEAD_DIM)
    qn2 = jnp.max(jnp.sum(qf * qf, axis=-1), axis=(2, 3)).reshape(b, s // tq, 1, 1)
    kf = kk[..., :HEAD_DIM].astype(F32)
    kn2 = jnp.max(jnp.sum(kf * kf, axis=-1), axis=1).reshape(b, 1, 1)
    per_batch = lambda shape: pl.BlockSpec((None,) + shape, lambda bi, i: (bi, 0, 0, 0))
    qrow = lambda width: pl.BlockSpec((None, tq, width), lambda bi, i: (bi, i, 0))
    return pl.pallas_call(
        functools.partial(_dsa_kernel, topk=topk),
        out_shape=jax.ShapeDtypeStruct((b, s, ATTN_W), MXU_DTYPE),
        grid=(b, s // tq),
        in_specs=[qrow(ATTN_W), qrow(IDX_HEADS * IDX_DIM), qrow(LANES),
                  pl.BlockSpec((None, None, 1, 1), lambda bi, i: (bi, i, 0, 0)),
                  pl.BlockSpec((None, 1, 1), lambda bi, i: (bi, 0, 0)),
                  per_batch((n_kt, LANES, tk)), per_batch((n_kt, tk, LANES))],
        out_specs=qrow(ATTN_W),
        scratch_shapes=[pltpu.VMEM((n_kt, tq, tk), jnp.int32),
                        pltpu.VMEM((n_kt, tq, tk), F32),
                        pltpu.VMEM((N_HEADS * tq, LANES), F32),
                        pltpu.VMEM((N_HEADS * tq, LANES), F32),
                        pltpu.VMEM((tq, LANES), jnp.int32)],
        compiler_params=_params("parallel", "parallel"),
        name="dsa_attention",
    )(q, qi, wv, qn2, kn2, kkt, vt)


def _ssm_kernel(u_ref, toep_ref, win_ref, wout_ref, apow_ref, y_ref, hloc_ref, hprev_ref, *, batch):
    rows, width = u_ref.shape
    sub = toep_ref.shape[1]
    n_sub = width // sub
    hloc_ref[...] = _dot(u_ref[...], win_ref[...])
    a_same, a_swap = apow_ref[0:1, :], apow_ref[1:2, :]
    h = jnp.zeros((batch, 2 * SSM_STATE), F32)
    for c in range(rows // batch):
        hprev_ref[c * batch:(c + 1) * batch, :] = h
        h = a_same * h + a_swap * pltpu.roll(h, SSM_STATE, 1) + hloc_ref[c * batch:(c + 1) * batch, :]
    hprev = hprev_ref[...].astype(MXU_DTYPE)
    for tb in range(n_sub):
        acc = _dot(hprev, wout_ref[:, tb * sub:(tb + 1) * sub])
        for sb in range(tb + 1):
            acc = acc + _dot(u_ref[:, sb * sub:(sb + 1) * sub], toep_ref[tb - sb])
        y_ref[:, tb * sub:(tb + 1) * sub] = acc.astype(y_ref.dtype)


def _ssm_call(u_g, toep, win, wout, apow, layer, batch):
    g, rows, width = u_g.shape
    n_sub, sub = toep.shape[2], toep.shape[3]
    per_group = lambda *shape: pl.BlockSpec((None, None) + shape, lambda gi: (layer, gi) + (0,) * len(shape))
    return pl.pallas_call(
        functools.partial(_ssm_kernel, batch=batch),
        out_shape=jax.ShapeDtypeStruct((g, rows, width), MXU_DTYPE),
        grid=(g,),
        in_specs=[pl.BlockSpec((None, rows, width), lambda gi: (gi, 0, 0)),
                  per_group(n_sub, sub, sub), per_group(width, 2 * SSM_STATE),
                  per_group(2 * SSM_STATE, width), per_group(2, 2 * SSM_STATE)],
        out_specs=pl.BlockSpec((None, rows, width), lambda gi: (gi, 0, 0)),
        scratch_shapes=[pltpu.VMEM((rows, 2 * SSM_STATE), F32), pltpu.VMEM((rows, 2 * SSM_STATE), F32)],
        compiler_params=_params("parallel"),
        name="s5_scan",
    )(u_g, toep, win, wout, apow)


def _ssm_operators(a_re, a_im, log_dt, b_re, b_im, c_re, c_im, d_skip, chunk):
    cmul = lambda xr, xi, yr, yi: (xr * yr - xi * yi, xr * yi + xi * yr)
    dt = jnp.exp(log_dt)[..., None]
    lam_re, lam_im = a_re * dt, a_im * dt
    lags = jnp.arange(chunk + 1, dtype=F32)[None, None, :, None]
    mag = jnp.exp(lam_re[:, :, None, :] * lags)
    ang = lam_im[:, :, None, :] * lags
    ap_re, ap_im = mag * jnp.cos(ang), mag * jnp.sin(ang)
    num_re, num_im = ap_re[:, :, 1] - 1.0, ap_im[:, :, 1]
    den = a_re * a_re + a_im * a_im
    coef_re, coef_im = (num_re * a_re + num_im * a_im) / den, (num_im * a_re - num_re * a_im) / den
    bb_re, bb_im = cmul(coef_re[..., None], coef_im[..., None], b_re, b_im)
    depth, g, p = a_re.shape
    gc = b_re.shape[-1]
    ca_re, ca_im = cmul(c_re[:, :, None], c_im[:, :, None], ap_re[:, :, :chunk, None, :], ap_im[:, :, :chunk, None, :])
    taps = (jnp.einsum('dgtop,dgpi->dgtoi', ca_re, bb_re, precision=lax.Precision.HIGHEST)
            - jnp.einsum('dgtop,dgpi->dgtoi', ca_im, bb_im, precision=lax.Precision.HIGHEST))
    taps = taps.at[:, :, 0].add(d_skip[..., None] * jnp.eye(gc, dtype=F32))
    n_sub = chunk // SSM_SUB
    taps = jnp.concatenate([jnp.zeros((depth, g, SSM_SUB - 1, gc, gc), F32), taps], axis=2).astype(MXU_DTYPE)
    blk, s_in, c_in, t_out, c_out = np.ogrid[:n_sub, :SSM_SUB, :gc, :SSM_SUB, :gc]
    flat_idx = ((blk * SSM_SUB + t_out - s_in + (SSM_SUB - 1)) * gc + c_out) * gc + c_in
    toep = jnp.take(taps.reshape(depth, g, -1), jnp.asarray(flat_idx.reshape(-1), jnp.int32), axis=2)
    toep = toep.reshape(depth, g, n_sub, SSM_SUB * gc, SSM_SUB * gc)
    rev_re, rev_im = ap_re[:, :, chunk - 1::-1][:, :, :chunk], ap_im[:, :, chunk - 1::-1][:, :, :chunk]
    wi_re, wi_im = cmul(rev_re[..., None], rev_im[..., None], bb_re[:, :, None], bb_im[:, :, None])
    w_in = jnp.concatenate([wi_re, wi_im], axis=3)
    w_in = w_in.transpose(0, 1, 2, 4, 3).reshape(depth, g, chunk * gc, 2 * p)
    wo_re, wo_im = cmul(c_re[:, :, None], c_im[:, :, None], ap_re[:, :, 1:, None, :], ap_im[:, :, 1:, None, :])
    w_out = jnp.concatenate([wo_re, -wo_im], axis=4)
    w_out = w_out.transpose(0, 1, 4, 2, 3).reshape(depth, g, 2 * p, chunk * gc)
    al_re, al_im = ap_re[:, :, chunk], ap_im[:, :, chunk]
    apow = jnp.stack([jnp.concatenate([al_re, al_re], -1),
                      jnp.concatenate([-al_im, al_im], -1)], axis=2)
    return toep, w_in.astype(MXU_DTYPE), w_out.astype(MXU_DTYPE), apow


def _outproj_kernel(x_ref, attn_ref, y_ref, mod_ref, ag_ref, sg_ref, gw_ref, gb_ref, woa_ref, wos_ref, o_ref):
    a = _rms(attn_ref[...].astype(F32), ag_ref[...]).astype(MXU_DTYPE)
    y = y_ref[...].astype(F32)
    y = 0.5 * y * (1.0 + jnp.tanh(math.sqrt(2.0 / math.pi) * (y + 0.044715 * (y * y * y))))
    y = y * jax.nn.sigmoid(_dot(y.astype(MXU_DTYPE), gw_ref[...]) + gb_ref[...])
    y = _rms(y, sg_ref[...]).astype(MXU_DTYPE)
    mixed = _dot(a, woa_ref[...]) + _dot(y, wos_ref[...])
    o_ref[...] = x_ref[...] + mod_ref[5:6, :] * mixed


def _outproj_call(x, attn, y, mod_l, attn_gain, ssm_gain, glu_w, glu_b, wo_a, wo_s, layer):
    b, s, d = x.shape
    tm = min(ROW_TILE, s)
    aw, sw = attn.shape[2], y.shape[2]
    row = lambda w: pl.BlockSpec((None, tm, w), lambda bi, i: (bi, i, 0))
    vec = lambda w: pl.BlockSpec((None, 1, w), lambda bi, i: (layer, 0, 0))
    wres = lambda r, c: _resident((None, r, c), lambda bi, i: (layer, 0, 0))
    return pl.pallas_call(
        _outproj_kernel,
        out_shape=jax.ShapeDtypeStruct(x.shape, F32),
        grid=(b, s // tm),
        in_specs=[row(d), row(aw), row(sw),
                  pl.BlockSpec((None, N_MOD, d), lambda bi, i: (bi, 0, 0)),
                  vec(aw), vec(sw), wres(sw, sw), vec(sw), wres(aw, d), wres(sw, d)],
        out_specs=row(d),
        compiler_params=_params("parallel", "parallel"),
        name="mixer_outproj",
    )(x, attn, y, mod_l, attn_gain[:, None, :], ssm_gain[:, None, :], glu_w, glu_b[:, None, :], wo_a, wo_s)


def kernel(x, c, positions, mod_w, mod_b, norm_g, ffn1_w1, ffn1_w3, ffn1_w2, ffn2_w1, ffn2_w3, ffn2_w2, w_in, w_out, attn_gain, ssm_gain, ssm_a_re, ssm_a_im, ssm_log_dt, ssm_b_re, ssm_b_im, ssm_c_re, ssm_c_im, ssm_d, glu_w, glu_b, final_g):
    b, s, d = x.shape
    depth = mod_w.shape[0]
    ssm_w = d - ATTN_W
    groups = ssm_w // SSM_GROUP
    chunk = min(SCAN_L, s)
    n_chunks = s // chunk
    cast = lambda a: a.astype(MXU_DTYPE)

    mod = _mod_call(c, mod_w, mod_b)
    tables = _rope_tables(positions)
    ffn1 = (cast(ffn1_w1), cast(ffn1_w3), cast(ffn1_w2))
    ffn2 = (cast(ffn2_w1), cast(ffn2_w3), cast(ffn2_w2))
    o_k = ATTN_W
    o_v = o_k + HEAD_DIM
    o_qi = o_v + HEAD_DIM
    o_ki = o_qi + IDX_HEADS * IDX_DIM
    o_wi = o_ki + IDX_DIM
    o_u = o_wi + IDX_HEADS
    wq = cast(w_in[:, :, :o_k])
    wqi = cast(w_in[:, :, o_qi:o_ki])
    wm = jnp.concatenate([w_in[:, :, o_k:o_v], w_in[:, :, o_ki:o_wi], w_in[:, :, o_v:o_qi], w_in[:, :, o_wi:o_u],
                          jnp.zeros((depth, d, 2 * LANES - (2 * HEAD_DIM + IDX_DIM + IDX_HEADS)), w_in.dtype)],
                         axis=2)
    wm = cast(wm)
    wu = cast(w_in[:, :, o_u:])
    wo_a, wo_s = cast(w_out[:, :ATTN_W]), cast(w_out[:, ATTN_W:])
    glu_wc = cast(glu_w)
    toep, s_win, s_wout, apow = _ssm_operators(ssm_a_re, ssm_a_im, ssm_log_dt, ssm_b_re, ssm_b_im,
                                               ssm_c_re, ssm_c_im, ssm_d, chunk)

    for l in range(depth):
        x = _ffn_call(x, mod[l], norm_g[l, 0], *ffn1, l, 0)
        q, qi, kk, v_aug, wv, u = _inproj_call(x, mod[l], norm_g[l, 1], wq, wqi, wm, wu, tables, l)
        attn = _dsa_call(q, qi, wv, kk, v_aug)
        u_g = u.reshape(b, n_chunks, chunk, groups, SSM_GROUP).transpose(3, 1, 0, 2, 4)
        u_g = u_g.reshape(groups, n_chunks * b, chunk * SSM_GROUP)
        y_g = _ssm_call(u_g, toep, s_win, s_wout, apow, l, b)
        y = y_g.reshape(groups, n_chunks, b, chunk, SSM_GROUP).transpose(2, 1, 3, 0, 4).reshape(b, s, ssm_w)
        x = _outproj_call(x, attn, y, mod[l], attn_gain, ssm_gain, glu_wc, glu_b, wo_a, wo_s, l)
        x = _ffn_call(x, mod[l], norm_g[l, 2], *ffn2, l, 6, final_g=final_g if l == depth - 1 else None)
    return x
```

```python
import functools
import math

import jax
import jax.numpy as jnp
from jax import lax
from jax.experimental import pallas as pl
from jax.experimental.pallas import tpu as pltpu

F32 = jnp.float32
MXU_DTYPE = jnp.bfloat16

CHUNK = 64
N_HEADS = 8
HEAD_DIM = 64
ATTN_W = N_HEADS * HEAD_DIM
IDX_HEADS = 4
IDX_DIM = 64
TOPK_MAX = 256
SSM_GROUP = 16
SSM_STATE = 64
SCAN_L = 128
ROPE_THETA = 500000.0
ROT_DIM = HEAD_DIM // 4
EPS = 1e-6
N_MOD = 9

LANES = 128
VMEM_LIMIT_BYTES = 56 * 1024 * 1024

INT_MIN = -(2 ** 31)
NEG_BIAS = -1e30

ROW_TILE = 512
Q_TILE = 256
K_TILE = 512
LOGIT_SAFE = 40.0
TAU_PROBE_BIT = 16
SSM_SUB = 16


def _params(*sem):
    return pltpu.CompilerParams(dimension_semantics=sem, vmem_limit_bytes=VMEM_LIMIT_BYTES)


def _dot(a, b):
    return jnp.dot(a, b, preferred_element_type=F32)


def _rms(x, g):
    return x * lax.rsqrt(jnp.mean(x * x, axis=-1, keepdims=True) + EPS) * g


def _modulated_norm(x, g, mod_ref, base):
    shift = mod_ref[base:base + 1, :]
    scale = mod_ref[base + 1:base + 2, :]
    return _rms(x, g) * (1.0 + scale) + shift


def _mod_kernel(c_ref, w_ref, b_ref, o_ref):
    c = c_ref[...]
    o_ref[...] = _dot(c * jax.nn.sigmoid(c), w_ref[...]) + b_ref[...]


def _mod_call(c, mod_w, mod_b):
    depth, d, nd = mod_w.shape
    b = c.shape[0]
    rows = -(-b // 8) * 8
    c_pad = jnp.zeros((rows, d), F32).at[:b].set(c)
    out = pl.pallas_call(
        _mod_kernel,
        out_shape=jax.ShapeDtypeStruct((depth, rows, nd), F32),
        grid=(depth, nd // d),
        in_specs=[pl.BlockSpec((rows, d), lambda l, j: (0, 0)),
                  pl.BlockSpec((None, d, d), lambda l, j: (l, 0, j)),
                  pl.BlockSpec((None, 1, d), lambda l, j: (l, 0, j))],
        out_specs=pl.BlockSpec((None, rows, d), lambda l, j: (l, 0, j)),
        compiler_params=_params("parallel", "parallel"),
        name="adaln_mod",
    )(c_pad, mod_w, mod_b.reshape(depth, 1, nd))
    return out[:, :b].reshape(depth, b, N_MOD, d)


def _ffn_kernel(x_ref, mod_ref, g_ref, w1_ref, w3_ref, w2_ref, o_ref, *, base, ff_chunk, final_norm):
    x = x_ref[...]
    h = _modulated_norm(x, g_ref[0:1, :], mod_ref, base).astype(MXU_DTYPE)
    acc = jnp.zeros(x.shape, F32)
    for c0 in range(0, w1_ref.shape[1], ff_chunk):
        a = _dot(h, w1_ref[:, c0:c0 + ff_chunk])
        b = _dot(h, w3_ref[:, c0:c0 + ff_chunk])
        t = (a * jax.nn.sigmoid(a) * b).astype(MXU_DTYPE)
        acc = acc + _dot(t, w2_ref[c0:c0 + ff_chunk, :])
    y = x + 0.5 * mod_ref[base + 2:base + 3, :] * acc
    o_ref[...] = _rms(y, g_ref[1:2, :]) if final_norm else y


def _resident(shape, index_map):
    return pl.BlockSpec(shape, index_map, pipeline_mode=pl.Buffered(1))


def _ffn_call(x, mod_l, g, w1, w3, w2, layer, base, final_g=None):
    b, s, d = x.shape
    gains = jnp.stack([g, g if final_g is None else final_g])
    ff = w1.shape[2]
    tm = min(ROW_TILE, s)
    ff_chunk = ff // 2 if (ff // 2) % LANES == 0 else ff
    wspec_in = _resident((None, d, ff), lambda bi, i: (layer, 0, 0))
    wspec_out = _resident((None, ff, d), lambda bi, i: (layer, 0, 0))
    return pl.pallas_call(
        functools.partial(_ffn_kernel, base=base, ff_chunk=ff_chunk, final_norm=final_g is not None),
        out_shape=jax.ShapeDtypeStruct(x.shape, F32),
        grid=(b, s // tm),
        in_specs=[pl.BlockSpec((None, tm, d), lambda bi, i: (bi, i, 0)),
                  pl.BlockSpec((None, N_MOD, d), lambda bi, i: (bi, 0, 0)),
                  pl.BlockSpec((2, d), lambda bi, i: (0, 0)),
                  wspec_in, wspec_in, wspec_out],
        out_specs=pl.BlockSpec((None, tm, d), lambda bi, i: (bi, i, 0)),
        compiler_params=_params("parallel", "parallel"),
        name="ffn",
    )(x, mod_l, gains, w1, w3, w2)


def _rope128(t, cos, s_lo, s_hi):
    half = ROT_DIM // 2
    return t * cos + pltpu.roll(t, LANES - half, 1) * s_lo + pltpu.roll(t, half, 1) * s_hi


def _inproj_kernel(x_ref, mod_ref, g_ref, wq_ref, wqi_ref, wm_ref, wu_ref, cos_ref, slo_ref, shi_ref,
                   q_ref, qi_ref, kk_ref, va_ref, wv_ref, u_ref):
    h = _modulated_norm(x_ref[...], g_ref[...], mod_ref, 3).astype(MXU_DTYPE)
    cos, s_lo, s_hi = cos_ref[...], slo_ref[...], shi_ref[...]
    q = _dot(h, wq_ref[...])
    for j in range(0, q.shape[1], LANES):
        r = _rope128(q[:, j:j + LANES], cos, s_lo, s_hi)
        q_ref[:, j:j + LANES] = (r * HEAD_DIM ** -0.5).astype(q_ref.dtype)
    qi = _dot(h, wqi_ref[...])
    for j in range(0, qi.shape[1], LANES):
        qi_ref[:, j:j + LANES] = _rope128(qi[:, j:j + LANES], cos, s_lo, s_hi).astype(qi_ref.dtype)
    m = _dot(h, wm_ref[...])
    kk_ref[...] = _rope128(m[:, :LANES], cos, s_lo, s_hi).astype(kk_ref.dtype)
    vw = m[:, LANES:]
    wv_ref[...] = vw
    lane = lax.broadcasted_iota(jnp.int32, vw.shape, 1)
    va_ref[...] = jnp.where(lane < HEAD_DIM, vw, jnp.where(lane == HEAD_DIM, 1.0, 0.0)).astype(va_ref.dtype)
    u_ref[...] = _dot(h, wu_ref[...]).astype(u_ref.dtype)


def _inproj_call(x, mod_l, g, wq, wqi, wm, wu, tables, layer):
    b, s, d = x.shape
    tm = min(ROW_TILE, s)
    row = lambda w: pl.BlockSpec((None, tm, w), lambda bi, i: (bi, i, 0))
    wres = lambda w: _resident((None, d, w), lambda bi, i: (layer, 0, 0))
    ssm_w = wu.shape[2]
    return pl.pallas_call(
        _inproj_kernel,
        out_shape=(jax.ShapeDtypeStruct((b, s, ATTN_W), MXU_DTYPE),
                   jax.ShapeDtypeStruct((b, s, IDX_HEADS * IDX_DIM), MXU_DTYPE),
                   jax.ShapeDtypeStruct((b, s, LANES), MXU_DTYPE),
                   jax.ShapeDtypeStruct((b, s, LANES), MXU_DTYPE),
                   jax.ShapeDtypeStruct((b, s, LANES), F32),
                   jax.ShapeDtypeStruct((b, s, ssm_w), MXU_DTYPE)),
        grid=(b, s // tm),
        in_specs=[row(d),
                  pl.BlockSpec((None, N_MOD, d), lambda bi, i: (bi, 0, 0)),
                  pl.BlockSpec((1, d), lambda bi, i: (0, 0)),
                  wres(ATTN_W), wres(IDX_HEADS * IDX_DIM), wres(2 * LANES), wres(ssm_w),
                  row(LANES), row(LANES), row(LANES)],
        out_specs=(row(ATTN_W), row(IDX_HEADS * IDX_DIM), row(LANES), row(LANES), row(LANES), row(ssm_w)),
        compiler_params=_params("parallel", "parallel"),
        name="mixer_inproj",
    )(x, mod_l, g.reshape(1, d), wq, wqi, wm, wu, *tables)


def _rope_tables(positions):
    inv_freq = 1.0 / (ROPE_THETA ** (jnp.arange(0, ROT_DIM, 2, dtype=F32) / ROT_DIM))
    ang = positions.astype(F32)[..., None] * inv_freq
    cos, sin = jnp.cos(ang), jnp.sin(ang)
    half = ROT_DIM // 2
    rest = HEAD_DIM - ROT_DIM
    pad = lambda *parts: jnp.tile(jnp.concatenate(parts, axis=-1), (1, 1, LANES // HEAD_DIM))
    zeros = lambda n: jnp.zeros(cos.shape[:-1] + (n,), F32)
    ones = jnp.ones(cos.shape[:-1] + (rest,), F32)
    return (pad(cos, cos, ones),
            pad(-sin, zeros(half), zeros(rest)),
            pad(zeros(half), sin, zeros(rest)))


def _dsa_kernel(q_ref, qi_ref, wv_ref, qn_ref, kn_ref, kk_ref, v_ref, o_ref,
                key_ref, bias_ref, m_ref, acc_ref, thr_ref, *, topk):
    tq = q_ref.shape[0]
    tk = kk_ref.shape[2]
    nslab = tk // LANES
    i = pl.program_id(1)
    n_tiles = ((i + 1) * tq + tk - 1) // tk
    row = lax.broadcasted_iota(jnp.int32, (tq, 1), 0) + i * tq
    q_lim = (row // CHUNK + 1) * CHUNK
    lane_tile = lax.broadcasted_iota(jnp.int32, (tq, tk), 1)

    w = wv_ref[:, HEAD_DIM:HEAD_DIM + IDX_HEADS] * (IDX_DIM ** -0.5 * IDX_HEADS ** -0.5)
    qi = qi_ref[...]
    qi_heads = [qi[:, h * IDX_DIM:(h + 1) * IDX_DIM] for h in range(IDX_HEADS)]

    def write_keys(t, masked):
        kit = kk_ref[t, HEAD_DIM:, :]
        s = jnp.zeros((tq, tk), F32)
        for h in range(IDX_HEADS):
            s = s + jnp.maximum(_dot(qi_heads[h], kit), 0.0) * w[:, h:h + 1]
        s = s + 0.0
        bits = pltpu.bitcast(s, jnp.int32)
        key = jnp.where(bits < 0, bits ^ jnp.int32(0x7FFFFFFF), bits)
        if masked:
            key = jnp.where(lane_tile + t * tk < q_lim, key, jnp.int32(INT_MIN))
        key_ref[t] = key

    def score_body(t, carry):
        write_keys(t, False)
        return carry

    lax.fori_loop(0, n_tiles - 1, score_body, 0)
    write_keys(n_tiles - 1, True)

    def count(thr, strict):
        thr_ref[...] = jnp.broadcast_to(thr, (tq, LANES))

        def body(t, acc):
            parts = []
            for r in range(0, tq, 8):
                th = thr_ref[r:r + 8, :]
                a = acc[r:r + 8]
                for j in range(nslab):
                    key = key_ref[t, r:r + 8, j * LANES:(j + 1) * LANES]
                    a = a + jnp.where(key > th if strict else key >= th, 1.0, 0.0)
                parts.append(a)
            return jnp.concatenate(parts, axis=0)

        acc = lax.fori_loop(0, n_tiles, body, jnp.zeros((tq, LANES), F32))
        return jnp.sum(acc, axis=1, keepdims=True)

    k_f = float(topk)
    n0 = count(jnp.zeros((tq, 1), jnp.int32), False)
    nonneg = n0 >= k_f
    tau0 = jnp.where(nonneg, jnp.int32(0), jnp.int32(INT_MIN))
    cnt0 = jnp.where(nonneg, n0, (n_tiles * tk).astype(F32))

    def refine(it, tau, cnt):
        trial = tau | lax.shift_left(jnp.int32(1), 30 - it)
        c = count(trial, False)
        ok = c >= k_f
        return jnp.where(ok, trial, tau), jnp.where(ok, c, cnt)

    tau, cnt = lax.fori_loop(0, TAU_PROBE_BIT, lambda it, tc: refine(it, *tc), (tau0, cnt0))
    locked = count(tau + 1, False) < k_f

    def unsettled(tau_cnt):
        return jnp.max(jnp.where(locked | (tau_cnt[1] == k_f), 0.0, 1.0))

    def tail_cond(carry):
        it, _, _, open_rows = carry
        return jnp.logical_and(it < 31, open_rows > 0.0)

    def tail_body(carry):
        it, tau, cnt, _ = carry
        tau, cnt = refine(it, tau, cnt)
        return it + 1, tau, cnt, unsettled((tau, cnt))

    _, tau, _, _ = lax.while_loop(tail_cond, tail_body,
                                  (jnp.int32(TAU_PROBE_BIT), tau, cnt, unsettled((tau, cnt))))
    need = k_f - count(tau, True)

    upper = (lax.broadcasted_iota(jnp.int32, (tk, tk), 0)
             <= lax.broadcasted_iota(jnp.int32, (tk, tk), 1)).astype(MXU_DTYPE)

    def bias_body(t, ties_before):
        key = key_ref[t]
        tie = jnp.where(key == tau, 1.0, 0.0)
        rank = ties_before + _dot(tie.astype(MXU_DTYPE), upper)
        sel = (key > tau) | ((key == tau) & (rank <= need))
        sel = sel & (key > jnp.int32(INT_MIN))
        bias_ref[t] = jnp.where(sel, 0.0, NEG_BIAS)
        return ties_before + jnp.sum(tie, axis=1, keepdims=True)

    lax.fori_loop(0, n_tiles, bias_body, jnp.zeros((tq, 1), F32))

    q = q_ref[...]
    q_all = jnp.concatenate([q[:, h * HEAD_DIM:(h + 1) * HEAD_DIM] for h in range(N_HEADS)], axis=0)
    rows = N_HEADS * tq

    def logits(t):
        return _dot(q_all, kk_ref[t, :HEAD_DIM, :]).reshape(N_HEADS, tq, tk) + bias_ref[t][None]

    safe = jnp.max(qn_ref[...] * kn_ref[...]) <= LOGIT_SAFE * LOGIT_SAFE

    @pl.when(safe)
    def _():
        m_ref[...] = jnp.zeros(m_ref.shape, F32)

    @pl.when(jnp.logical_not(safe))
    def _():
        m_ref[...] = jnp.full(m_ref.shape, NEG_BIAS, F32)

        def max_body(t, carry):
            s = logits(t)
            mx = s[:, :, :LANES]
            for j in range(1, nslab):
                mx = jnp.maximum(mx, s[:, :, j * LANES:(j + 1) * LANES])
            m_ref[...] = jnp.maximum(m_ref[...], mx.reshape(rows, LANES))
            return carry

        lax.fori_loop(0, n_tiles, max_body, 0)
        m_ref[...] = jnp.broadcast_to(jnp.max(m_ref[...], axis=1, keepdims=True), (rows, LANES))

    acc_ref[...] = jnp.zeros(acc_ref.shape, F32)

    def att_body(t, carry):
        m_all = jnp.concatenate([m_ref[...]] * nslab, axis=1).reshape(N_HEADS, tq, tk)
        p = jnp.exp(logits(t) - m_all)
        acc_ref[...] += _dot(p.astype(MXU_DTYPE).reshape(rows, tk), v_ref[t])
        return carry

    lax.fori_loop(0, n_tiles, att_body, 0)
    acc = acc_ref[...]
    out = acc[:, :HEAD_DIM] / acc[:, HEAD_DIM:HEAD_DIM + 1]
    for h in range(N_HEADS):
        o_ref[:, h * HEAD_DIM:(h + 1) * HEAD_DIM] = out[h * tq:(h + 1) * tq].astype(o_ref.dtype)


def _dsa_call(q, qi, wv, kk, v_aug):
    b, s, _ = q.shape
    tq = min(Q_TILE, s)
    tk = min(K_TILE, s)
    n_kt = s // tk
    topk = min(TOPK_MAX, s // 4)
    kkt = kk.reshape(b, n_kt, tk, LANES).transpose(0, 1, 3, 2)
    vt = v_aug.reshape(b, n_kt, tk, LANES)
    assert tk % tq == 0
    qf = q.astype(F32).reshape(b, s // tq, tq, N_HEADS, HEAD_DIM)
    qn2 = jnp.max(jnp.sum(qf * qf, axis=-1), axis=(2, 3)).reshape(b, s // tq, 1, 1)
    kf = kk[..., :HEAD_DIM].astype(F32)
    kn2 = jnp.max(jnp.sum(kf * kf, axis=-1), axis=1).reshape(b, 1, 1)
    per_batch = lambda shape: pl.BlockSpec((None,) + shape, lambda bi, i: (bi, 0, 0, 0))
    qrow = lambda width: pl.BlockSpec((None, tq, width), lambda bi, i: (bi, i, 0))
    return pl.pallas_call(
        functools.partial(_dsa_kernel, topk=topk),
        out_shape=jax.ShapeDtypeStruct((b, s, ATTN_W), MXU_DTYPE),
        grid=(b, s // tq),
        in_specs=[qrow(ATTN_W), qrow(IDX_HEADS * IDX_DIM), qrow(LANES),
                  pl.BlockSpec((None, None, 1, 1), lambda bi, i: (bi, i, 0, 0)),
                  pl.BlockSpec((None, 1, 1), lambda bi, i: (bi, 0, 0)),
                  per_batch((n_kt, LANES, tk)), per_batch((n_kt, tk, LANES))],
        out_specs=qrow(ATTN_W),
        scratch_shapes=[pltpu.VMEM((n_kt, tq, tk), jnp.int32),
                        pltpu.VMEM((n_kt, tq, tk), F32),
                        pltpu.VMEM((N_HEADS * tq, LANES), F32),
                        pltpu.VMEM((N_HEADS * tq, LANES), F32),
                        pltpu.VMEM((tq, LANES), jnp.int32)],
        compiler_params=_params("parallel", "parallel"),
        name="dsa_attention",
    )(q, qi, wv, qn2, kn2, kkt, vt)


def _ssm_kernel(u_ref, toep_ref, win_ref, wout_ref, apow_ref, y_ref, hloc_ref, hprev_ref, *, batch):
    rows, width = u_ref.shape
    sub = toep_ref.shape[0]
    n_sub = width // sub
    rot = lambda h, r: apow_ref[r:r + 1, :] * h + apow_ref[r + 1:r + 2, :] * pltpu.roll(h, SSM_STATE, 1)
    h = jnp.zeros((rows, 2 * SSM_STATE), F32)
    local = []
    for j in range(n_sub):
        h = rot(h, 0) + _dot(u_ref[:, j * sub:(j + 1) * sub], win_ref[...])
        local.append(h.astype(MXU_DTYPE))
    hloc_ref[...] = h
    h = jnp.zeros((batch, 2 * SSM_STATE), F32)
    for c in range(rows // batch):
        hprev_ref[c * batch:(c + 1) * batch, :] = h
        h = rot(h, 2) + hloc_ref[c * batch:(c + 1) * batch, :]
    hprev = hprev_ref[...].astype(MXU_DTYPE)
    for j in range(n_sub):
        acc = _dot(hprev, wout_ref[:, j * sub:(j + 1) * sub]) + _dot(u_ref[:, j * sub:(j + 1) * sub], toep_ref[...])
        if j > 0:
            acc = acc + _dot(local[j - 1], wout_ref[:, :sub])
        y_ref[:, j * sub:(j + 1) * sub] = acc.astype(y_ref.dtype)


def _ssm_call(u_g, toep, win, wout, apow, layer, batch):
    g, rows, width = u_g.shape
    sub = toep.shape[2]
    per_group = lambda *shape: pl.BlockSpec((None, None) + shape, lambda gi: (layer, gi) + (0,) * len(shape))
    return pl.pallas_call(
        functools.partial(_ssm_kernel, batch=batch),
        out_shape=jax.ShapeDtypeStruct((g, rows, width), MXU_DTYPE),
        grid=(g,),
        in_specs=[pl.BlockSpec((None, rows, width), lambda gi: (gi, 0, 0)),
                  per_group(sub, sub), per_group(sub, 2 * SSM_STATE),
                  per_group(2 * SSM_STATE, width), per_group(4, 2 * SSM_STATE)],
        out_specs=pl.BlockSpec((None, rows, width), lambda gi: (gi, 0, 0)),
        scratch_shapes=[pltpu.VMEM((rows, 2 * SSM_STATE), F32), pltpu.VMEM((rows, 2 * SSM_STATE), F32)],
        compiler_params=_params("parallel"),
        name="s5_scan",
    )(u_g, toep, win, wout, apow)


def _ssm_operators(a_re, a_im, log_dt, b_re, b_im, c_re, c_im, d_skip, chunk):
    cmul = lambda xr, xi, yr, yi: (xr * yr - xi * yi, xr * yi + xi * yr)
    dt = jnp.exp(log_dt)[..., None]
    lam_re, lam_im = a_re * dt, a_im * dt
    lags = jnp.arange(chunk + 1, dtype=F32)[None, None, :, None]
    mag = jnp.exp(lam_re[:, :, None, :] * lags)
    ang = lam_im[:, :, None, :] * lags
    ap_re, ap_im = mag * jnp.cos(ang), mag * jnp.sin(ang)
    num_re, num_im = ap_re[:, :, 1] - 1.0, ap_im[:, :, 1]
    den = a_re * a_re + a_im * a_im
    coef_re, coef_im = (num_re * a_re + num_im * a_im) / den, (num_im * a_re - num_re * a_im) / den
    bb_re, bb_im = cmul(coef_re[..., None], coef_im[..., None], b_re, b_im)
    depth, g, p = a_re.shape
    gc = b_re.shape[-1]
    sub = SSM_SUB
    ca_re, ca_im = cmul(c_re[:, :, None], c_im[:, :, None], ap_re[:, :, :sub, None, :], ap_im[:, :, :sub, None, :])
    taps = (jnp.einsum('dgtop,dgpi->dgtoi', ca_re, bb_re, precision=lax.Precision.HIGHEST)
            - jnp.einsum('dgtop,dgpi->dgtoi', ca_im, bb_im, precision=lax.Precision.HIGHEST))
    taps = taps.at[:, :, 0].add(d_skip[..., None] * jnp.eye(gc, dtype=F32))
    taps = jnp.concatenate([jnp.zeros((depth, g, sub - 1, gc, gc), F32), taps], axis=2)
    lag_idx = jnp.arange(sub)[None, :] - jnp.arange(sub)[:, None] + (sub - 1)
    toep = taps[:, :, lag_idx]
    toep = toep.transpose(0, 1, 2, 5, 3, 4).reshape(depth, g, sub * gc, sub * gc)
    rev_re, rev_im = ap_re[:, :, sub - 1::-1][:, :, :sub], ap_im[:, :, sub - 1::-1][:, :, :sub]
    wi_re, wi_im = cmul(rev_re[..., None], rev_im[..., None], bb_re[:, :, None], bb_im[:, :, None])
    w_in = jnp.concatenate([wi_re, wi_im], axis=3)
    w_in = w_in.transpose(0, 1, 2, 4, 3).reshape(depth, g, sub * gc, 2 * p)
    wo_re, wo_im = cmul(c_re[:, :, None], c_im[:, :, None], ap_re[:, :, 1:, None, :], ap_im[:, :, 1:, None, :])
    w_out = jnp.concatenate([wo_re, -wo_im], axis=4)
    w_out = w_out.transpose(0, 1, 4, 2, 3).reshape(depth, g, 2 * p, chunk * gc)
    rot_rows = lambda lag: [jnp.concatenate([ap_re[:, :, lag], ap_re[:, :, lag]], -1),
                            jnp.concatenate([-ap_im[:, :, lag], ap_im[:, :, lag]], -1)]
    apow = jnp.stack(rot_rows(sub) + rot_rows(chunk), axis=2)
    return toep.astype(MXU_DTYPE), w_in.astype(MXU_DTYPE), w_out.astype(MXU_DTYPE), apow


def _outproj_kernel(x_ref, attn_ref, y_ref, mod_ref, ag_ref, sg_ref, gw_ref, gb_ref, woa_ref, wos_ref, o_ref):
    a = _rms(attn_ref[...].astype(F32), ag_ref[...]).astype(MXU_DTYPE)
    y = y_ref[...].astype(F32)
    y = 0.5 * y * (1.0 + jnp.tanh(math.sqrt(2.0 / math.pi) * (y + 0.044715 * (y * y * y))))
    y = y * jax.nn.sigmoid(_dot(y.astype(MXU_DTYPE), gw_ref[...]) + gb_ref[...])
    y = _rms(y, sg_ref[...]).astype(MXU_DTYPE)
    mixed = _dot(a, woa_ref[...]) + _dot(y, wos_ref[...])
    o_ref[...] = x_ref[...] + mod_ref[5:6, :] * mixed


def _outproj_call(x, attn, y, mod_l, attn_gain, ssm_gain, glu_w, glu_b, wo_a, wo_s, layer):
    b, s, d = x.shape
    tm = min(ROW_TILE, s)
    aw, sw = attn.shape[2], y.shape[2]
    row = lambda w: pl.BlockSpec((None, tm, w), lambda bi, i: (bi, i, 0))
    vec = lambda w: pl.BlockSpec((None, 1, w), lambda bi, i: (layer, 0, 0))
    wres = lambda r, c: _resident((None, r, c), lambda bi, i: (layer, 0, 0))
    return pl.pallas_call(
        _outproj_kernel,
        out_shape=jax.ShapeDtypeStruct(x.shape, F32),
        grid=(b, s // tm),
        in_specs=[row(d), row(aw), row(sw),
                  pl.BlockSpec((None, N_MOD, d), lambda bi, i: (bi, 0, 0)),
                  vec(aw), vec(sw), wres(sw, sw), vec(sw), wres(aw, d), wres(sw, d)],
        out_specs=row(d),
        compiler_params=_params("parallel", "parallel"),
        name="mixer_outproj",
    )(x, attn, y, mod_l, attn_gain[:, None, :], ssm_gain[:, None, :], glu_w, glu_b[:, None, :], wo_a, wo_s)


def kernel(x, c, positions, mod_w, mod_b, norm_g, ffn1_w1, ffn1_w3, ffn1_w2, ffn2_w1, ffn2_w3, ffn2_w2, w_in, w_out, attn_gain, ssm_gain, ssm_a_re, ssm_a_im, ssm_log_dt, ssm_b_re, ssm_b_im, ssm_c_re, ssm_c_im, ssm_d, glu_w, glu_b, final_g):
    b, s, d = x.shape
    depth = mod_w.shape[0]
    ssm_w = d - ATTN_W
    groups = ssm_w // SSM_GROUP
    chunk = min(SCAN_L, s)
    n_chunks = s // chunk
    cast = lambda a: a.astype(MXU_DTYPE)

    mod = _mod_call(c, mod_w, mod_b)
    tables = _rope_tables(positions)
    ffn1 = (cast(ffn1_w1), cast(ffn1_w3), cast(ffn1_w2))
    ffn2 = (cast(ffn2_w1), cast(ffn2_w3), cast(ffn2_w2))
    o_k = ATTN_W
    o_v = o_k + HEAD_DIM
    o_qi = o_v + HEAD_DIM
    o_ki = o_qi + IDX_HEADS * IDX_DIM
    o_wi = o_ki + IDX_DIM
    o_u = o_wi + IDX_HEADS
    wq = cast(w_in[:, :, :o_k])
    wqi = cast(w_in[:, :, o_qi:o_ki])
    wm = jnp.concatenate([w_in[:, :, o_k:o_v], w_in[:, :, o_ki:o_wi], w_in[:, :, o_v:o_qi], w_in[:, :, o_wi:o_u],
                          jnp.zeros((depth, d, 2 * LANES - (2 * HEAD_DIM + IDX_DIM + IDX_HEADS)), w_in.dtype)],
                         axis=2)
    wm = cast(wm)
    wu = cast(w_in[:, :, o_u:])
    wo_a, wo_s = cast(w_out[:, :ATTN_W]), cast(w_out[:, ATTN_W:])
    glu_wc = cast(glu_w)
    toep, s_win, s_wout, apow = _ssm_operators(ssm_a_re, ssm_a_im, ssm_log_dt, ssm_b_re, ssm_b_im,
                                               ssm_c_re, ssm_c_im, ssm_d, chunk)

    for l in range(depth):
        x = _ffn_call(x, mod[l], norm_g[l, 0], *ffn1, l, 0)
        q, qi, kk, v_aug, wv, u = _inproj_call(x, mod[l], norm_g[l, 1], wq, wqi, wm, wu, tables, l)
        attn = _dsa_call(q, qi, wv, kk, v_aug)
        u_g = u.reshape(b, n_chunks, chunk, groups, SSM_GROUP).transpose(3, 1, 0, 2, 4)
        u_g = u_g.reshape(groups, n_chunks * b, chunk * SSM_GROUP)
        y_g = _ssm_call(u_g, toep, s_win, s_wout, apow, l, b)
        y = y_g.reshape(groups, n_chunks, b, chunk, SSM_GROUP).transpose(2, 1, 3, 0, 4).reshape(b, s, ssm_w)
        x = _outproj_call(x, attn, y, mod[l], attn_gain, ssm_gain, glu_wc, glu_b, wo_a, wo_s, l)
        x = _ffn_call(x, mod[l], norm_g[l, 2], *ffn2, l, 6, final_g=final_g if l == depth - 1 else None)
    return x
```

```python
import functools
import math

import jax
import jax.numpy as jnp
from jax import lax
from jax.experimental import pallas as pl
from jax.experimental.pallas import tpu as pltpu

F32 = jnp.float32
MXU_DTYPE = jnp.bfloat16

CHUNK = 64
N_HEADS = 8
HEAD_DIM = 64
ATTN_W = N_HEADS * HEAD_DIM
IDX_HEADS = 4
IDX_DIM = 64
TOPK_MAX = 256
SSM_GROUP = 16
SSM_STATE = 64
SCAN_L = 128
ROPE_THETA = 500000.0
ROT_DIM = HEAD_DIM // 4
EPS = 1e-6
N_MOD = 9

LANES = 128
VMEM_LIMIT_BYTES = 56 * 1024 * 1024

INT_MIN = -(2 ** 31)
NEG_BIAS = -1e30

ROW_TILE = 512
Q_TILE = 256
K_TILE = 512
LOGIT_SAFE = 40.0
COUNT_CHAINS = 4
TAU_PROBE_BIT = 16
SSM_SUB = 16


def _params(*sem):
    return pltpu.CompilerParams(dimension_semantics=sem, vmem_limit_bytes=VMEM_LIMIT_BYTES)


def _dot(a, b):
    return jnp.dot(a, b, preferred_element_type=F32)


def _rms(x, g):
    return x * lax.rsqrt(jnp.mean(x * x, axis=-1, keepdims=True) + EPS) * g


def _modulated_norm(x, g, mod_ref, base):
    shift = mod_ref[base:base + 1, :]
    scale = mod_ref[base + 1:base + 2, :]
    return _rms(x, g) * (1.0 + scale) + shift


def _mod_kernel(c_ref, w_ref, b_ref, o_ref):
    c = c_ref[...]
    o_ref[...] = _dot(c * jax.nn.sigmoid(c), w_ref[...]) + b_ref[...]


def _mod_call(c, mod_w, mod_b):
    depth, d, nd = mod_w.shape
    b = c.shape[0]
    rows = -(-b // 8) * 8
    c_pad = jnp.zeros((rows, d), F32).at[:b].set(c)
    out = pl.pallas_call(
        _mod_kernel,
        out_shape=jax.ShapeDtypeStruct((depth, rows, nd), F32),
        grid=(depth, nd // d),
        in_specs=[pl.BlockSpec((rows, d), lambda l, j: (0, 0)),
                  pl.BlockSpec((None, d, d), lambda l, j: (l, 0, j)),
                  pl.BlockSpec((None, 1, d), lambda l, j: (l, 0, j))],
        out_specs=pl.BlockSpec((None, rows, d), lambda l, j: (l, 0, j)),
        compiler_params=_params("parallel", "parallel"),
        name="adaln_mod",
    )(c_pad, mod_w, mod_b.reshape(depth, 1, nd))
    return out[:, :b].reshape(depth, b, N_MOD, d)


def _ffn_kernel(x_ref, mod_ref, g_ref, w1_ref, w3_ref, w2_ref, o_ref, *, base, ff_chunk, final_norm):
    x = x_ref[...]
    h = _modulated_norm(x, g_ref[0:1, :], mod_ref, base).astype(MXU_DTYPE)
    acc = jnp.zeros(x.shape, F32)
    for c0 in range(0, w1_ref.shape[1], ff_chunk):
        a = _dot(h, w1_ref[:, c0:c0 + ff_chunk])
        b = _dot(h, w3_ref[:, c0:c0 + ff_chunk])
        t = (a * jax.nn.sigmoid(a) * b).astype(MXU_DTYPE)
        acc = acc + _dot(t, w2_ref[c0:c0 + ff_chunk, :])
    y = x + 0.5 * mod_ref[base + 2:base + 3, :] * acc
    o_ref[...] = _rms(y, g_ref[1:2, :]) if final_norm else y


def _resident(shape, index_map):
    return pl.BlockSpec(shape, index_map, pipeline_mode=pl.Buffered(1))


def _ffn_call(x, mod_l, g, w1, w3, w2, layer, base, final_g=None):
    b, s, d = x.shape
    gains = jnp.stack([g, g if final_g is None else final_g])
    ff = w1.shape[2]
    tm = min(ROW_TILE, s)
    ff_chunk = ff // 2 if (ff // 2) % LANES == 0 else ff
    wspec_in = _resident((None, d, ff), lambda bi, i: (layer, 0, 0))
    wspec_out = _resident((None, ff, d), lambda bi, i: (layer, 0, 0))
    return pl.pallas_call(
        functools.partial(_ffn_kernel, base=base, ff_chunk=ff_chunk, final_norm=final_g is not None),
        out_shape=jax.ShapeDtypeStruct(x.shape, F32),
        grid=(b, s // tm),
        in_specs=[pl.BlockSpec((None, tm, d), lambda bi, i: (bi, i, 0)),
                  pl.BlockSpec((None, N_MOD, d), lambda bi, i: (bi, 0, 0)),
                  pl.BlockSpec((2, d), lambda bi, i: (0, 0)),
                  wspec_in, wspec_in, wspec_out],
        out_specs=pl.BlockSpec((None, tm, d), lambda bi, i: (bi, i, 0)),
        compiler_params=_params("parallel", "parallel"),
        name="ffn",
    )(x, mod_l, gains, w1, w3, w2)


def _rope128(t, cos, s_lo, s_hi):
    half = ROT_DIM // 2
    return t * cos + pltpu.roll(t, LANES - half, 1) * s_lo + pltpu.roll(t, half, 1) * s_hi


def _inproj_kernel(x_ref, mod_ref, g_ref, wq_ref, wqi_ref, wm_ref, wu_ref, cos_ref, slo_ref, shi_ref,
                   q_ref, qi_ref, kk_ref, va_ref, wv_ref, u_ref):
    h = _modulated_norm(x_ref[...], g_ref[...], mod_ref, 3).astype(MXU_DTYPE)
    cos, s_lo, s_hi = cos_ref[...], slo_ref[...], shi_ref[...]
    q = _dot(h, wq_ref[...])
    for j in range(0, q.shape[1], LANES):
        r = _rope128(q[:, j:j + LANES], cos, s_lo, s_hi)
        q_ref[:, j:j + LANES] = (r * HEAD_DIM ** -0.5).astype(q_ref.dtype)
    qi = _dot(h, wqi_ref[...])
    for j in range(0, qi.shape[1], LANES):
        qi_ref[:, j:j + LANES] = _rope128(qi[:, j:j + LANES], cos, s_lo, s_hi).astype(qi_ref.dtype)
    m = _dot(h, wm_ref[...])
    kk_ref[...] = _rope128(m[:, :LANES], cos, s_lo, s_hi).astype(kk_ref.dtype)
    vw = m[:, LANES:]
    wv_ref[...] = vw
    lane = lax.broadcasted_iota(jnp.int32, vw.shape, 1)
    va_ref[...] = jnp.where(lane < HEAD_DIM, vw, jnp.where(lane == HEAD_DIM, 1.0, 0.0)).astype(va_ref.dtype)
    u_ref[...] = _dot(h, wu_ref[...]).astype(u_ref.dtype)


def _inproj_call(x, mod_l, g, wq, wqi, wm, wu, tables, layer):
    b, s, d = x.shape
    tm = min(ROW_TILE, s)
    row = lambda w: pl.BlockSpec((None, tm, w), lambda bi, i: (bi, i, 0))
    wres = lambda w: _resident((None, d, w), lambda bi, i: (layer, 0, 0))
    ssm_w = wu.shape[2]
    return pl.pallas_call(
        _inproj_kernel,
        out_shape=(jax.ShapeDtypeStruct((b, s, ATTN_W), MXU_DTYPE),
                   jax.ShapeDtypeStruct((b, s, IDX_HEADS * IDX_DIM), MXU_DTYPE),
                   jax.ShapeDtypeStruct((b, s, LANES), MXU_DTYPE),
                   jax.ShapeDtypeStruct((b, s, LANES), MXU_DTYPE),
                   jax.ShapeDtypeStruct((b, s, LANES), F32),
                   jax.ShapeDtypeStruct((b, s, ssm_w), MXU_DTYPE)),
        grid=(b, s // tm),
        in_specs=[row(d),
                  pl.BlockSpec((None, N_MOD, d), lambda bi, i: (bi, 0, 0)),
                  pl.BlockSpec((1, d), lambda bi, i: (0, 0)),
                  wres(ATTN_W), wres(IDX_HEADS * IDX_DIM), wres(2 * LANES), wres(ssm_w),
                  row(LANES), row(LANES), row(LANES)],
        out_specs=(row(ATTN_W), row(IDX_HEADS * IDX_DIM), row(LANES), row(LANES), row(LANES), row(ssm_w)),
        compiler_params=_params("parallel", "parallel"),
        name="mixer_inproj",
    )(x, mod_l, g.reshape(1, d), wq, wqi, wm, wu, *tables)


def _rope_tables(positions):
    inv_freq = 1.0 / (ROPE_THETA ** (jnp.arange(0, ROT_DIM, 2, dtype=F32) / ROT_DIM))
    ang = positions.astype(F32)[..., None] * inv_freq
    cos, sin = jnp.cos(ang), jnp.sin(ang)
    half = ROT_DIM // 2
    rest = HEAD_DIM - ROT_DIM
    pad = lambda *parts: jnp.tile(jnp.concatenate(parts, axis=-1), (1, 1, LANES // HEAD_DIM))
    zeros = lambda n: jnp.zeros(cos.shape[:-1] + (n,), F32)
    ones = jnp.ones(cos.shape[:-1] + (rest,), F32)
    return (pad(cos, cos, ones),
            pad(-sin, zeros(half), zeros(rest)),
            pad(zeros(half), sin, zeros(rest)))


def _dsa_kernel(qt_ref, qit_ref, wt_ref, qn_ref, kn_ref, kk_ref, vt_ref, o_ref,
                key_ref, bias_ref, acc_ref, *, topk):
    tq = qt_ref.shape[1]
    tk = kk_ref.shape[1]
    i = pl.program_id(1)
    n_tiles = ((i + 1) * tq + tk - 1) // tk
    col = lax.broadcasted_iota(jnp.int32, (1, tq), 1) + i * tq
    q_lim = (col // CHUNK + 1) * CHUNK
    key_pos = lax.broadcasted_iota(jnp.int32, (tk, tq), 0)
    zeros_half = jnp.zeros((HEAD_DIM, tq), MXU_DTYPE)

    qit = qit_ref[...]
    wqi = jnp.concatenate([jnp.concatenate([zeros_half, qit[h * IDX_DIM:(h + 1) * IDX_DIM, :]], axis=0)
                           for h in range(IDX_HEADS)], axis=1)
    w = wt_ref[...] * (IDX_DIM ** -0.5 * IDX_HEADS ** -0.5)
    w_row = jnp.concatenate([w[h:h + 1, :] for h in range(IDX_HEADS)], axis=1)

    def write_keys(t, masked):
        rel = jnp.maximum(_dot(kk_ref[t], wqi), 0.0) * w_row
        s = rel[:, :tq]
        for h in range(1, IDX_HEADS):
            s = s + rel[:, h * tq:(h + 1) * tq]
        s = s + 0.0
        bits = pltpu.bitcast(s, jnp.int32)
        key = jnp.where(bits < 0, bits ^ jnp.int32(0x7FFFFFFF), bits)
        if masked:
            key = jnp.where(key_pos + t * tk < q_lim, key, jnp.int32(INT_MIN))
        key_ref[t] = key

    def score_body(t, carry):
        write_keys(t, False)
        return carry

    lax.fori_loop(0, n_tiles - 1, score_body, 0)
    write_keys(n_tiles - 1, True)

    def count(thr, strict):
        thr_b = jnp.broadcast_to(thr, (8, tq))

        def body(t, accs):
            accs = list(accs)
            for r in range(0, tk, 8):
                key = key_ref[t, r:r + 8, :]
                hit = jnp.where(key > thr_b if strict else key >= thr_b, 1.0, 0.0)
                accs[(r // 8) % COUNT_CHAINS] = accs[(r // 8) % COUNT_CHAINS] + hit
            return tuple(accs)

        accs = lax.fori_loop(0, n_tiles, body, tuple(jnp.zeros((8, tq), F32) for _ in range(COUNT_CHAINS)))
        total = accs[0]
        for a in accs[1:]:
            total = total + a
        return jnp.sum(total, axis=0, keepdims=True)

    k_f = float(topk)
    n0 = count(jnp.zeros((1, tq), jnp.int32), False)
    nonneg = n0 >= k_f
    tau0 = jnp.where(nonneg, jnp.int32(0), jnp.int32(INT_MIN))
    cnt0 = jnp.where(nonneg, n0, (n_tiles * tk).astype(F32))

    def refine(it, tau, cnt):
        trial = tau | lax.shift_left(jnp.int32(1), 30 - it)
        c = count(trial, False)
        ok = c >= k_f
        return jnp.where(ok, trial, tau), jnp.where(ok, c, cnt)

    tau, cnt = lax.fori_loop(0, TAU_PROBE_BIT, lambda it, tc: refine(it, *tc), (tau0, cnt0))
    locked = count(tau + 1, False) < k_f

    def unsettled(tau_cnt):
        return jnp.max(jnp.where(locked | (tau_cnt[1] == k_f), 0.0, 1.0))

    def tail_cond(carry):
        it, _, _, open_rows = carry
        return jnp.logical_and(it < 31, open_rows > 0.0)

    def tail_body(carry):
        it, tau, cnt, _ = carry
        tau, cnt = refine(it, tau, cnt)
        return it + 1, tau, cnt, unsettled((tau, cnt))

    _, tau, _, _ = lax.while_loop(tail_cond, tail_body,
                                  (jnp.int32(TAU_PROBE_BIT), tau, cnt, unsettled((tau, cnt))))
    need = k_f - count(tau, True)

    lower = (lax.broadcasted_iota(jnp.int32, (tk, tk), 1)
             <= lax.broadcasted_iota(jnp.int32, (tk, tk), 0)).astype(MXU_DTYPE)

    def bias_body(t, ties_before):
        key = key_ref[t]
        tie = jnp.where(key == tau, 1.0, 0.0)
        rank = ties_before + _dot(lower, tie.astype(MXU_DTYPE))
        sel = (key > tau) | ((key == tau) & (rank <= need))
        sel = sel & (key > jnp.int32(INT_MIN))
        bias_ref[t] = jnp.where(sel, 0.0, NEG_BIAS)
        return rank[tk - 1:tk, :]

    lax.fori_loop(0, n_tiles, bias_body, jnp.zeros((1, tq), F32))

    qt = qt_ref[...]
    wq = jnp.concatenate([jnp.concatenate([qt[h * HEAD_DIM:(h + 1) * HEAD_DIM, :], zeros_half], axis=0)
                          for h in range(N_HEADS)], axis=1)

    def logits(t):
        return _dot(kk_ref[t], wq) + jnp.concatenate([bias_ref[t]] * N_HEADS, axis=1)

    safe = jnp.max(qn_ref[...] * kn_ref[...]) <= LOGIT_SAFE * LOGIT_SAFE

    def exact_max():
        def max_body(t, m):
            s = logits(t)
            part = s[:8, :]
            for r in range(8, tk, 8):
                part = jnp.maximum(part, s[r:r + 8, :])
            return jnp.maximum(m, part)
        m = lax.fori_loop(0, n_tiles, max_body, jnp.full((8, N_HEADS * tq), NEG_BIAS, F32))
        return jnp.max(m, axis=0, keepdims=True)

    offset = lax.cond(safe, lambda: jnp.zeros((1, N_HEADS * tq), F32), exact_max)
    acc_ref[...] = jnp.zeros(acc_ref.shape, F32)

    def att_body(t, carry):
        p = jnp.exp(logits(t) - offset)
        acc_ref[...] += _dot(vt_ref[t], p.astype(MXU_DTYPE))
        return carry

    lax.fori_loop(0, n_tiles, att_body, 0)
    acc = acc_ref[...]
    o_ref[...] = (acc[:HEAD_DIM, :] / acc[HEAD_DIM:HEAD_DIM + 1, :]).astype(o_ref.dtype)


def _dsa_call(q, qi, wv, kk, v_aug):
    b, s, _ = q.shape
    tq = min(Q_TILE, s)
    tk = min(K_TILE, s)
    n_kt = s // tk
    n_q = s // tq
    topk = min(TOPK_MAX, s // 4)
    assert tk % tq == 0
    qt = q.transpose(0, 2, 1)
    qit = qi.transpose(0, 2, 1)
    wt = wv[..., HEAD_DIM:HEAD_DIM + 8].transpose(0, 2, 1)
    kk_t = kk.reshape(b, n_kt, tk, LANES)
    vt = v_aug.reshape(b, n_kt, tk, LANES).transpose(0, 1, 3, 2)
    qf = q.astype(F32).reshape(b, n_q, tq, N_HEADS, HEAD_DIM)
    qn2 = jnp.max(jnp.sum(qf * qf, axis=-1), axis=(2, 3)).reshape(b, n_q, 1, 1)
    kf = kk[..., :HEAD_DIM].astype(F32)
    kn2 = jnp.max(jnp.sum(kf * kf, axis=-1), axis=1).reshape(b, 1, 1)
    per_batch = lambda shape: pl.BlockSpec((None,) + shape, lambda bi, i: (bi, 0, 0, 0))
    qcol = lambda height: pl.BlockSpec((None, height, tq), lambda bi, i: (bi, 0, i))
    out_t = pl.pallas_call(
        functools.partial(_dsa_kernel, topk=topk),
        out_shape=jax.ShapeDtypeStruct((b, n_q, HEAD_DIM, N_HEADS * tq), MXU_DTYPE),
        grid=(b, n_q),
        in_specs=[qcol(ATTN_W), qcol(IDX_HEADS * IDX_DIM), qcol(8),
                  pl.BlockSpec((None, None, 1, 1), lambda bi, i: (bi, i, 0, 0)),
                  pl.BlockSpec((None, 1, 1), lambda bi, i: (bi, 0, 0)),
                  per_batch((n_kt, tk, LANES)), per_batch((n_kt, LANES, tk))],
        out_specs=pl.BlockSpec((None, None, HEAD_DIM, N_HEADS * tq), lambda bi, i: (bi, i, 0, 0)),
        scratch_shapes=[pltpu.VMEM((n_kt, tk, tq), jnp.int32),
                        pltpu.VMEM((n_kt, tk, tq), F32),
                        pltpu.VMEM((LANES, N_HEADS * tq), F32)],
        compiler_params=_params("parallel", "parallel"),
        name="dsa_attention",
    )(qt, qit, wt, qn2, kn2, kk_t, vt)
    out = out_t.reshape(b, n_q, HEAD_DIM, N_HEADS, tq).transpose(0, 1, 4, 3, 2)
    return out.reshape(b, s, ATTN_W)


def _ssm_kernel(u_ref, toep_ref, win_ref, wout_ref, apow_ref, y_ref, hloc_ref, hswap_ref, hprev_ref, *, batch):
    rows, width = u_ref.shape
    sub = toep_ref.shape[0]
    n_sub = width // sub
    rot = lambda h, r: apow_ref[r:r + 1, :] * h + apow_ref[r + 1:r + 2, :] * pltpu.roll(h, SSM_STATE, 1)
    h = jnp.zeros((rows, 2 * SSM_STATE), F32)
    local = []
    for j in range(n_sub):
        h = rot(h, 0) + _dot(u_ref[:, j * sub:(j + 1) * sub], win_ref[...])
        local.append(h.astype(MXU_DTYPE))
    hloc_ref[...] = h
    hswap_ref[...] = pltpu.roll(h, SSM_STATE, 1)
    a_same, a_swap = apow_ref[2:3, :], apow_ref[3:4, :]
    a_swap_rolled = pltpu.roll(a_swap, SSM_STATE, 1)
    h = jnp.zeros((batch, 2 * SSM_STATE), F32)
    h_sw = jnp.zeros((batch, 2 * SSM_STATE), F32)
    for c in range(rows // batch):
        hprev_ref[c * batch:(c + 1) * batch, :] = h
        h, h_sw = (a_same * h + a_swap * h_sw + hloc_ref[c * batch:(c + 1) * batch, :],
                   a_same * h_sw + a_swap_rolled * h + hswap_ref[c * batch:(c + 1) * batch, :])
    hprev = hprev_ref[...].astype(MXU_DTYPE)
    for j in range(n_sub):
        acc = _dot(hprev, wout_ref[:, j * sub:(j + 1) * sub]) + _dot(u_ref[:, j * sub:(j + 1) * sub], toep_ref[...])
        if j > 0:
            acc = acc + _dot(local[j - 1], wout_ref[:, :sub])
        y_ref[:, j * sub:(j + 1) * sub] = acc.astype(y_ref.dtype)


def _ssm_call(u_g, toep, win, wout, apow, layer, batch):
    g, rows, width = u_g.shape
    sub = toep.shape[2]
    per_group = lambda *shape: pl.BlockSpec((None, None) + shape, lambda gi: (layer, gi) + (0,) * len(shape))
    return pl.pallas_call(
        functools.partial(_ssm_kernel, batch=batch),
        out_shape=jax.ShapeDtypeStruct((g, rows, width), MXU_DTYPE),
        grid=(g,),
        in_specs=[pl.BlockSpec((None, rows, width), lambda gi: (gi, 0, 0)),
                  per_group(sub, sub), per_group(sub, 2 * SSM_STATE),
                  per_group(2 * SSM_STATE, width), per_group(4, 2 * SSM_STATE)],
        out_specs=pl.BlockSpec((None, rows, width), lambda gi: (gi, 0, 0)),
        scratch_shapes=[pltpu.VMEM((rows, 2 * SSM_STATE), F32)] * 3,
        compiler_params=_params("parallel"),
        name="s5_scan",
    )(u_g, toep, win, wout, apow)


def _ssm_operators(a_re, a_im, log_dt, b_re, b_im, c_re, c_im, d_skip, chunk):
    cmul = lambda xr, xi, yr, yi: (xr * yr - xi * yi, xr * yi + xi * yr)
    dt = jnp.exp(log_dt)[..., None]
    lam_re, lam_im = a_re * dt, a_im * dt
    lags = jnp.arange(chunk + 1, dtype=F32)[None, None, :, None]
    mag = jnp.exp(lam_re[:, :, None, :] * lags)
    ang = lam_im[:, :, None, :] * lags
    ap_re, ap_im = mag * jnp.cos(ang), mag * jnp.sin(ang)
    num_re, num_im = ap_re[:, :, 1] - 1.0, ap_im[:, :, 1]
    den = a_re * a_re + a_im * a_im
    coef_re, coef_im = (num_re * a_re + num_im * a_im) / den, (num_im * a_re - num_re * a_im) / den
    bb_re, bb_im = cmul(coef_re[..., None], coef_im[..., None], b_re, b_im)
    depth, g, p = a_re.shape
    gc = b_re.shape[-1]
    sub = SSM_SUB
    ca_re, ca_im = cmul(c_re[:, :, None], c_im[:, :, None], ap_re[:, :, :sub, None, :], ap_im[:, :, :sub, None, :])
    taps = (jnp.einsum('dgtop,dgpi->dgtoi', ca_re, bb_re, precision=lax.Precision.HIGHEST)
            - jnp.einsum('dgtop,dgpi->dgtoi', ca_im, bb_im, precision=lax.Precision.HIGHEST))
    taps = taps.at[:, :, 0].add(d_skip[..., None] * jnp.eye(gc, dtype=F32))
    taps = jnp.concatenate([jnp.zeros((depth, g, sub - 1, gc, gc), F32), taps], axis=2)
    lag_idx = jnp.arange(sub)[None, :] - jnp.arange(sub)[:, None] + (sub - 1)
    toep = taps[:, :, lag_idx]
    toep = toep.transpose(0, 1, 2, 5, 3, 4).reshape(depth, g, sub * gc, sub * gc)
    rev_re, rev_im = ap_re[:, :, sub - 1::-1][:, :, :sub], ap_im[:, :, sub - 1::-1][:, :, :sub]
    wi_re, wi_im = cmul(rev_re[..., None], rev_im[..., None], bb_re[:, :, None], bb_im[:, :, None])
    w_in = jnp.concatenate([wi_re, wi_im], axis=3)
    w_in = w_in.transpose(0, 1, 2, 4, 3).reshape(depth, g, sub * gc, 2 * p)
    wo_re, wo_im = cmul(c_re[:, :, None], c_im[:, :, None], ap_re[:, :, 1:, None, :], ap_im[:, :, 1:, None, :])
    w_out = jnp.concatenate([wo_re, -wo_im], axis=4)
    w_out = w_out.transpose(0, 1, 4, 2, 3).reshape(depth, g, 2 * p, chunk * gc)
    rot_rows = lambda lag: [jnp.concatenate([ap_re[:, :, lag], ap_re[:, :, lag]], -1),
                            jnp.concatenate([-ap_im[:, :, lag], ap_im[:, :, lag]], -1)]
    apow = jnp.stack(rot_rows(sub) + rot_rows(chunk), axis=2)
    return toep.astype(MXU_DTYPE), w_in.astype(MXU_DTYPE), w_out.astype(MXU_DTYPE), apow


def _outproj_kernel(x_ref, attn_ref, y_ref, mod_ref, ag_ref, sg_ref, gw_ref, gb_ref, woa_ref, wos_ref, o_ref):
    a = _rms(attn_ref[...].astype(F32), ag_ref[...]).astype(MXU_DTYPE)
    y = y_ref[...].astype(F32)
    y = 0.5 * y * (1.0 + jnp.tanh(math.sqrt(2.0 / math.pi) * (y + 0.044715 * (y * y * y))))
    y = y * jax.nn.sigmoid(_dot(y.astype(MXU_DTYPE), gw_ref[...]) + gb_ref[...])
    y = _rms(y, sg_ref[...]).astype(MXU_DTYPE)
    mixed = _dot(a, woa_ref[...]) + _dot(y, wos_ref[...])
    o_ref[...] = x_ref[...] + mod_ref[5:6, :] * mixed


def _outproj_call(x, attn, y, mod_l, attn_gain, ssm_gain, glu_w, glu_b, wo_a, wo_s, layer):
    b, s, d = x.shape
    tm = min(ROW_TILE, s)
    aw, sw = attn.shape[2], y.shape[2]
    row = lambda w: pl.BlockSpec((None, tm, w), lambda bi, i: (bi, i, 0))
    vec = lambda w: pl.BlockSpec((None, 1, w), lambda bi, i: (layer, 0, 0))
    wres = lambda r, c: _resident((None, r, c), lambda bi, i: (layer, 0, 0))
    return pl.pallas_call(
        _outproj_kernel,
        out_shape=jax.ShapeDtypeStruct(x.shape, F32),
        grid=(b, s // tm),
        in_specs=[row(d), row(aw), row(sw),
                  pl.BlockSpec((None, N_MOD, d), lambda bi, i: (bi, 0, 0)),
                  vec(aw), vec(sw), wres(sw, sw), vec(sw), wres(aw, d), wres(sw, d)],
        out_specs=row(d),
        compiler_params=_params("parallel", "parallel"),
        name="mixer_outproj",
    )(x, attn, y, mod_l, attn_gain[:, None, :], ssm_gain[:, None, :], glu_w, glu_b[:, None, :], wo_a, wo_s)


def kernel(x, c, positions, mod_w, mod_b, norm_g, ffn1_w1, ffn1_w3, ffn1_w2, ffn2_w1, ffn2_w3, ffn2_w2, w_in, w_out, attn_gain, ssm_gain, ssm_a_re, ssm_a_im, ssm_log_dt, ssm_b_re, ssm_b_im, ssm_c_re, ssm_c_im, ssm_d, glu_w, glu_b, final_g):
    b, s, d = x.shape
    depth = mod_w.shape[0]
    ssm_w = d - ATTN_W
    groups = ssm_w // SSM_GROUP
    chunk = min(SCAN_L, s)
    n_chunks = s // chunk
    cast = lambda a: a.astype(MXU_DTYPE)

    mod = _mod_call(c, mod_w, mod_b)
    tables = _rope_tables(positions)
    ffn1 = (cast(ffn1_w1), cast(ffn1_w3), cast(ffn1_w2))
    ffn2 = (cast(ffn2_w1), cast(ffn2_w3), cast(ffn2_w2))
    o_k = ATTN_W
    o_v = o_k + HEAD_DIM
    o_qi = o_v + HEAD_DIM
    o_ki = o_qi + IDX_HEADS * IDX_DIM
    o_wi = o_ki + IDX_DIM
    o_u = o_wi + IDX_HEADS
    wq = cast(w_in[:, :, :o_k])
    wqi = cast(w_in[:, :, o_qi:o_ki])
    wm = jnp.concatenate([w_in[:, :, o_k:o_v], w_in[:, :, o_ki:o_wi], w_in[:, :, o_v:o_qi], w_in[:, :, o_wi:o_u],
                          jnp.zeros((depth, d, 2 * LANES - (2 * HEAD_DIM + IDX_DIM + IDX_HEADS)), w_in.dtype)],
                         axis=2)
    wm = cast(wm)
    wu = cast(w_in[:, :, o_u:])
    wo_a, wo_s = cast(w_out[:, :ATTN_W]), cast(w_out[:, ATTN_W:])
    glu_wc = cast(glu_w)
    toep, s_win, s_wout, apow = _ssm_operators(ssm_a_re, ssm_a_im, ssm_log_dt, ssm_b_re, ssm_b_im,
                                               ssm_c_re, ssm_c_im, ssm_d, chunk)

    for l in range(depth):
        x = _ffn_call(x, mod[l], norm_g[l, 0], *ffn1, l, 0)
        q, qi, kk, v_aug, wv, u = _inproj_call(x, mod[l], norm_g[l, 1], wq, wqi, wm, wu, tables, l)
        attn = _dsa_call(q, qi, wv, kk, v_aug)
        u_g = u.reshape(b, n_chunks, chunk, groups, SSM_GROUP).transpose(3, 1, 0, 2, 4)
        u_g = u_g.reshape(groups, n_chunks * b, chunk * SSM_GROUP)
        y_g = _ssm_call(u_g, toep, s_win, s_wout, apow, l, b)
        y = y_g.reshape(groups, n_chunks, b, chunk, SSM_GROUP).transpose(2, 1, 3, 0, 4).reshape(b, s, ssm_w)
        x = _outproj_call(x, attn, y, mod[l], attn_gain, ssm_gain, glu_wc, glu_b, wo_a, wo_s, l)
        x = _ffn_call(x, mod[l], norm_g[l, 2], *ffn2, l, 6, final_g=final_g if l == depth - 1 else None)
    return x
```

```python
import functools
import math

import jax
import jax.numpy as jnp
from jax import lax
from jax.experimental import pallas as pl
from jax.experimental.pallas import tpu as pltpu

F32 = jnp.float32
MXU_DTYPE = jnp.bfloat16

CHUNK = 64
N_HEADS = 8
HEAD_DIM = 64
ATTN_W = N_HEADS * HEAD_DIM
IDX_HEADS = 4
IDX_DIM = 64
TOPK_MAX = 256
SSM_GROUP = 16
SSM_STATE = 64
SCAN_L = 128
ROPE_THETA = 500000.0
ROT_DIM = HEAD_DIM // 4
EPS = 1e-6
N_MOD = 9

LANES = 128
VMEM_LIMIT_BYTES = 56 * 1024 * 1024

INT_MIN = -(2 ** 31)
NEG_BIAS = -1e30

ROW_TILE = 512
Q_TILE = 256
K_TILE = 512
LOGIT_SAFE = 40.0
COUNT_CHAINS = 4
TAU_PROBE_BIT = 16
SSM_SUB = 16


def _params(*sem):
    return pltpu.CompilerParams(dimension_semantics=sem, vmem_limit_bytes=VMEM_LIMIT_BYTES)


def _dot(a, b):
    return jnp.dot(a, b, preferred_element_type=F32)


def _rms(x, g):
    return x * lax.rsqrt(jnp.mean(x * x, axis=-1, keepdims=True) + EPS) * g


def _modulated_norm(x, g, mod_ref, base):
    shift = mod_ref[base:base + 1, :]
    scale = mod_ref[base + 1:base + 2, :]
    return _rms(x, g) * (1.0 + scale) + shift


def _mod_kernel(c_ref, w_ref, b_ref, o_ref):
    c = c_ref[...]
    o_ref[...] = _dot(c * jax.nn.sigmoid(c), w_ref[...]) + b_ref[...]


def _mod_call(c, mod_w, mod_b):
    depth, d, nd = mod_w.shape
    b = c.shape[0]
    rows = -(-b // 8) * 8
    c_pad = jnp.zeros((rows, d), F32).at[:b].set(c)
    out = pl.pallas_call(
        _mod_kernel,
        out_shape=jax.ShapeDtypeStruct((depth, rows, nd), F32),
        grid=(depth, nd // d),
        in_specs=[pl.BlockSpec((rows, d), lambda l, j: (0, 0)),
                  pl.BlockSpec((None, d, d), lambda l, j: (l, 0, j)),
                  pl.BlockSpec((None, 1, d), lambda l, j: (l, 0, j))],
        out_specs=pl.BlockSpec((None, rows, d), lambda l, j: (l, 0, j)),
        compiler_params=_params("parallel", "parallel"),
        name="adaln_mod",
    )(c_pad, mod_w, mod_b.reshape(depth, 1, nd))
    return out[:, :b].reshape(depth, b, N_MOD, d)


def _ffn_kernel(x_ref, mod_ref, g_ref, w1_ref, w3_ref, w2_ref, o_ref, *, base, ff_chunk, final_norm):
    x = x_ref[...]
    h = _modulated_norm(x, g_ref[0:1, :], mod_ref, base).astype(MXU_DTYPE)
    acc = jnp.zeros(x.shape, F32)
    for c0 in range(0, w1_ref.shape[1], ff_chunk):
        a = _dot(h, w1_ref[:, c0:c0 + ff_chunk])
        b = _dot(h, w3_ref[:, c0:c0 + ff_chunk])
        t = (a * jax.nn.sigmoid(a) * b).astype(MXU_DTYPE)
        acc = acc + _dot(t, w2_ref[c0:c0 + ff_chunk, :])
    y = x + 0.5 * mod_ref[base + 2:base + 3, :] * acc
    o_ref[...] = _rms(y, g_ref[1:2, :]) if final_norm else y


def _resident(shape, index_map):
    return pl.BlockSpec(shape, index_map, pipeline_mode=pl.Buffered(1))


def _ffn_call(x, mod_l, g, w1, w3, w2, layer, base, final_g=None):
    b, s, d = x.shape
    gains = jnp.stack([g, g if final_g is None else final_g])
    ff = w1.shape[2]
    tm = min(ROW_TILE, s)
    ff_chunk = ff // 2 if (ff // 2) % LANES == 0 else ff
    wspec_in = _resident((None, d, ff), lambda bi, i: (layer, 0, 0))
    wspec_out = _resident((None, ff, d), lambda bi, i: (layer, 0, 0))
    return pl.pallas_call(
        functools.partial(_ffn_kernel, base=base, ff_chunk=ff_chunk, final_norm=final_g is not None),
        out_shape=jax.ShapeDtypeStruct(x.shape, F32),
        grid=(b, s // tm),
        in_specs=[pl.BlockSpec((None, tm, d), lambda bi, i: (bi, i, 0)),
                  pl.BlockSpec((None, N_MOD, d), lambda bi, i: (bi, 0, 0)),
                  pl.BlockSpec((2, d), lambda bi, i: (0, 0)),
                  wspec_in, wspec_in, wspec_out],
        out_specs=pl.BlockSpec((None, tm, d), lambda bi, i: (bi, i, 0)),
        compiler_params=_params("parallel", "parallel"),
        name="ffn",
    )(x, mod_l, gains, w1, w3, w2)


def _rope128(t, cos, s_lo, s_hi):
    half = ROT_DIM // 2
    return t * cos + pltpu.roll(t, LANES - half, 1) * s_lo + pltpu.roll(t, half, 1) * s_hi


def _dot_nt(a, b):
    return lax.dot_general(a, b, (((1,), (1,)), ((), ())), preferred_element_type=F32)


def _rope_rows(t, cos_t, sin_t):
    half = ROT_DIM // 2
    parts = []
    for base in range(0, t.shape[0], HEAD_DIM):
        t1, t2 = t[base:base + half], t[base + half:base + ROT_DIM]
        parts += [t1 * cos_t - t2 * sin_t, t2 * cos_t + t1 * sin_t, t[base + ROT_DIM:base + HEAD_DIM]]
    return jnp.concatenate(parts, axis=0)


def _inproj_kernel(x_ref, mod_ref, g_ref, wqt_ref, wqit_ref, wkk_ref, wvt_ref, wwt_ref, wu_ref,
                   cos_ref, slo_ref, shi_ref, cost_ref, sint_ref,
                   qt_ref, qit_ref, kk_ref, vt_ref, wt_ref, u_ref, qn_ref, kn_ref):
    h = _modulated_norm(x_ref[...], g_ref[...], mod_ref, 3).astype(MXU_DTYPE)
    cos_t, sin_t = cost_ref[...], sint_ref[...]
    qt = (_rope_rows(_dot_nt(wqt_ref[...], h), cos_t, sin_t) * HEAD_DIM ** -0.5).astype(qt_ref.dtype)
    qt_ref[...] = qt
    qit_ref[...] = _rope_rows(_dot_nt(wqit_ref[...], h), cos_t, sin_t).astype(qit_ref.dtype)
    kk = _rope128(_dot(h, wkk_ref[...]), cos_ref[...], slo_ref[...], shi_ref[...]).astype(kk_ref.dtype)
    kk_ref[...] = kk
    vt = _dot_nt(wvt_ref[...], h)
    row = lax.broadcasted_iota(jnp.int32, vt.shape, 0)
    vt_ref[...] = jnp.where(row == HEAD_DIM, 1.0, vt).astype(vt_ref.dtype)
    wt_ref[...] = _dot_nt(wwt_ref[...], h)
    u_ref[...] = _dot(h, wu_ref[...]).astype(u_ref.dtype)
    q2 = qt.astype(F32) * qt.astype(F32)
    qn_ref[...] = jnp.max(jnp.concatenate(
        [jnp.sum(q2[b0:b0 + HEAD_DIM], axis=0, keepdims=True) for b0 in range(0, q2.shape[0], HEAD_DIM)], axis=0),
        keepdims=True)
    k2 = kk.astype(F32) * kk.astype(F32)
    lane = lax.broadcasted_iota(jnp.int32, k2.shape, 1)
    kn_ref[...] = jnp.max(jnp.sum(jnp.where(lane < HEAD_DIM, k2, 0.0), axis=1, keepdims=True), keepdims=True)


def _inproj_call(x, mod_l, g, wqt, wqit, wkk, wvt, wwt, wu, tables, layer):
    b, s, d = x.shape
    tm = min(ROW_TILE, s)
    n_t = s // tm
    row = lambda w: pl.BlockSpec((None, tm, w), lambda bi, i: (bi, i, 0))
    col = lambda hgt: pl.BlockSpec((None, hgt, tm), lambda bi, i: (bi, 0, i))
    wres = lambda r, c: _resident((None, r, c), lambda bi, i: (layer, 0, 0))
    one = pl.BlockSpec((None, None, 1, 1), lambda bi, i: (bi, i, 0, 0))
    ssm_w = wu.shape[2]
    return pl.pallas_call(
        _inproj_kernel,
        out_shape=(jax.ShapeDtypeStruct((b, ATTN_W, s), MXU_DTYPE),
                   jax.ShapeDtypeStruct((b, IDX_HEADS * IDX_DIM, s), MXU_DTYPE),
                   jax.ShapeDtypeStruct((b, s, LANES), MXU_DTYPE),
                   jax.ShapeDtypeStruct((b, n_t, LANES, tm), MXU_DTYPE),
                   jax.ShapeDtypeStruct((b, 8, s), F32),
                   jax.ShapeDtypeStruct((b, s, ssm_w), MXU_DTYPE),
                   jax.ShapeDtypeStruct((b, n_t, 1, 1), F32),
                   jax.ShapeDtypeStruct((b, n_t, 1, 1), F32)),
        grid=(b, n_t),
        in_specs=[row(d),
                  pl.BlockSpec((None, N_MOD, d), lambda bi, i: (bi, 0, 0)),
                  pl.BlockSpec((1, d), lambda bi, i: (0, 0)),
                  wres(ATTN_W, d), wres(IDX_HEADS * IDX_DIM, d), wres(d, LANES), wres(LANES, d), wres(8, d),
                  wres(d, ssm_w),
                  row(LANES), row(LANES), row(LANES), col(ROT_DIM // 2), col(ROT_DIM // 2)],
        out_specs=(col(ATTN_W), col(IDX_HEADS * IDX_DIM), row(LANES),
                   pl.BlockSpec((None, None, LANES, tm), lambda bi, i: (bi, i, 0, 0)),
                   col(8), row(ssm_w), one, one),
        compiler_params=_params("parallel", "parallel"),
        name="mixer_inproj",
    )(x, mod_l, g.reshape(1, d), wqt, wqit, wkk, wvt, wwt, wu, *tables)


def _rope_tables(positions):
    inv_freq = 1.0 / (ROPE_THETA ** (jnp.arange(0, ROT_DIM, 2, dtype=F32) / ROT_DIM))
    ang = positions.astype(F32)[..., None] * inv_freq
    cos, sin = jnp.cos(ang), jnp.sin(ang)
    half = ROT_DIM // 2
    rest = HEAD_DIM - ROT_DIM
    pad = lambda *parts: jnp.tile(jnp.concatenate(parts, axis=-1), (1, 1, LANES // HEAD_DIM))
    zeros = lambda n: jnp.zeros(cos.shape[:-1] + (n,), F32)
    ones = jnp.ones(cos.shape[:-1] + (rest,), F32)
    return (pad(cos, cos, ones),
            pad(-sin, zeros(half), zeros(rest)),
            pad(zeros(half), sin, zeros(rest)),
            cos.transpose(0, 2, 1), sin.transpose(0, 2, 1))


def _dsa_kernel(qt_ref, qit_ref, wt_ref, qn_ref, kn_ref, kk_ref, vt_ref, o_ref,
                key_ref, bias_ref, acc_ref, *, topk):
    tq = qt_ref.shape[1]
    tk = kk_ref.shape[1]
    i = pl.program_id(1)
    n_tiles = ((i + 1) * tq + tk - 1) // tk
    col = lax.broadcasted_iota(jnp.int32, (1, tq), 1) + i * tq
    q_lim = (col // CHUNK + 1) * CHUNK
    key_pos = lax.broadcasted_iota(jnp.int32, (tk, tq), 0)
    zeros_half = jnp.zeros((HEAD_DIM, tq), MXU_DTYPE)

    qit = qit_ref[...]
    wqi = jnp.concatenate([jnp.concatenate([zeros_half, qit[h * IDX_DIM:(h + 1) * IDX_DIM, :]], axis=0)
                           for h in range(IDX_HEADS)], axis=1)
    w = wt_ref[...] * (IDX_DIM ** -0.5 * IDX_HEADS ** -0.5)
    w_row = jnp.concatenate([w[h:h + 1, :] for h in range(IDX_HEADS)], axis=1)

    def write_keys(t, masked):
        rel = jnp.maximum(_dot(kk_ref[t], wqi), 0.0) * w_row
        s = rel[:, :tq]
        for h in range(1, IDX_HEADS):
            s = s + rel[:, h * tq:(h + 1) * tq]
        s = s + 0.0
        bits = pltpu.bitcast(s, jnp.int32)
        key = jnp.where(bits < 0, bits ^ jnp.int32(0x7FFFFFFF), bits)
        if masked:
            key = jnp.where(key_pos + t * tk < q_lim, key, jnp.int32(INT_MIN))
        key_ref[t] = key

    def score_body(t, carry):
        write_keys(t, False)
        return carry

    lax.fori_loop(0, n_tiles - 1, score_body, 0)
    write_keys(n_tiles - 1, True)

    def count(thr, strict):
        thr_b = jnp.broadcast_to(thr, (8, tq))

        def body(t, accs):
            accs = list(accs)
            for r in range(0, tk, 8):
                key = key_ref[t, r:r + 8, :]
                hit = jnp.where(key > thr_b if strict else key >= thr_b, 1.0, 0.0)
                accs[(r // 8) % COUNT_CHAINS] = accs[(r // 8) % COUNT_CHAINS] + hit
            return tuple(accs)

        accs = lax.fori_loop(0, n_tiles, body, tuple(jnp.zeros((8, tq), F32) for _ in range(COUNT_CHAINS)))
        total = accs[0]
        for a in accs[1:]:
            total = total + a
        return jnp.sum(total, axis=0, keepdims=True)

    k_f = float(topk)
    n0 = count(jnp.zeros((1, tq), jnp.int32), False)
    nonneg = n0 >= k_f
    tau0 = jnp.where(nonneg, jnp.int32(0), jnp.int32(INT_MIN))
    cnt0 = jnp.where(nonneg, n0, (n_tiles * tk).astype(F32))

    def refine(it, tau, cnt):
        trial = tau | lax.shift_left(jnp.int32(1), 30 - it)
        c = count(trial, False)
        ok = c >= k_f
        return jnp.where(ok, trial, tau), jnp.where(ok, c, cnt)

    tau, cnt = lax.fori_loop(0, TAU_PROBE_BIT, lambda it, tc: refine(it, *tc), (tau0, cnt0))
    locked = count(tau + 1, False) < k_f

    def unsettled(tau_cnt):
        return jnp.max(jnp.where(locked | (tau_cnt[1] == k_f), 0.0, 1.0))

    def tail_cond(carry):
        it, _, _, open_rows = carry
        return jnp.logical_and(it < 31, open_rows > 0.0)

    def tail_body(carry):
        it, tau, cnt, _ = carry
        tau, cnt = refine(it, tau, cnt)
        return it + 1, tau, cnt, unsettled((tau, cnt))

    _, tau, _, _ = lax.while_loop(tail_cond, tail_body,
                                  (jnp.int32(TAU_PROBE_BIT), tau, cnt, unsettled((tau, cnt))))
    need = k_f - count(tau, True)

    lower = (lax.broadcasted_iota(jnp.int32, (tk, tk), 1)
             <= lax.broadcasted_iota(jnp.int32, (tk, tk), 0)).astype(MXU_DTYPE)

    def bias_body(t, ties_before):
        key = key_ref[t]
        tie = jnp.where(key == tau, 1.0, 0.0)
        rank = ties_before + _dot(lower, tie.astype(MXU_DTYPE))
        sel = (key > tau) | ((key == tau) & (rank <= need))
        sel = sel & (key > jnp.int32(INT_MIN))
        bias_ref[t] = jnp.where(sel, 0.0, NEG_BIAS)
        return rank[tk - 1:tk, :]

    lax.fori_loop(0, n_tiles, bias_body, jnp.zeros((1, tq), F32))

    qt = qt_ref[...]
    wq = jnp.concatenate([jnp.concatenate([qt[h * HEAD_DIM:(h + 1) * HEAD_DIM, :], zeros_half], axis=0)
                          for h in range(N_HEADS)], axis=1)

    def logits(t):
        return _dot(kk_ref[t], wq) + jnp.concatenate([bias_ref[t]] * N_HEADS, axis=1)

    safe = jnp.max(qn_ref[...] * kn_ref[...]) <= LOGIT_SAFE * LOGIT_SAFE

    def exact_max():
        def max_body(t, m):
            s = logits(t)
            part = s[:8, :]
            for r in range(8, tk, 8):
                part = jnp.maximum(part, s[r:r + 8, :])
            return jnp.maximum(m, part)
        m = lax.fori_loop(0, n_tiles, max_body, jnp.full((8, N_HEADS * tq), NEG_BIAS, F32))
        return jnp.max(m, axis=0, keepdims=True)

    offset = lax.cond(safe, lambda: jnp.zeros((1, N_HEADS * tq), F32), exact_max)
    acc_ref[...] = jnp.zeros(acc_ref.shape, F32)

    def att_body(t, carry):
        p = jnp.exp(logits(t) - offset)
        acc_ref[...] += _dot(vt_ref[t], p.astype(MXU_DTYPE))
        return carry

    lax.fori_loop(0, n_tiles, att_body, 0)
    acc = acc_ref[...]
    o_ref[...] = (acc[:HEAD_DIM, :] / acc[HEAD_DIM:HEAD_DIM + 1, :]).astype(o_ref.dtype)


def _dsa_call(qt, qit, wt, qn2, kn2, kk, vt):
    b, _, s = qt.shape
    tq = min(Q_TILE, s)
    tk = vt.shape[3]
    n_kt = s // tk
    n_q = s // tq
    topk = min(TOPK_MAX, s // 4)
    assert tk % tq == 0 and s % (tq * qn2.shape[1]) == 0
    q_per_norm_tile = s // qn2.shape[1] // tq
    kk_t = kk.reshape(b, n_kt, tk, LANES)
    kn2 = jnp.max(kn2, axis=1)
    per_batch = lambda shape: pl.BlockSpec((None,) + shape, lambda bi, i: (bi, 0, 0, 0))
    qcol = lambda height: pl.BlockSpec((None, height, tq), lambda bi, i: (bi, 0, i))
    out_t = pl.pallas_call(
        functools.partial(_dsa_kernel, topk=topk),
        out_shape=jax.ShapeDtypeStruct((b, n_q, HEAD_DIM, N_HEADS * tq), MXU_DTYPE),
        grid=(b, n_q),
        in_specs=[qcol(ATTN_W), qcol(IDX_HEADS * IDX_DIM), qcol(8),
                  pl.BlockSpec((None, None, 1, 1), lambda bi, i: (bi, i // q_per_norm_tile, 0, 0)),
                  pl.BlockSpec((None, 1, 1), lambda bi, i: (bi, 0, 0)),
                  per_batch((n_kt, tk, LANES)), per_batch((n_kt, LANES, tk))],
        out_specs=pl.BlockSpec((None, None, HEAD_DIM, N_HEADS * tq), lambda bi, i: (bi, i, 0, 0)),
        scratch_shapes=[pltpu.VMEM((n_kt, tk, tq), jnp.int32),
                        pltpu.VMEM((n_kt, tk, tq), F32),
                        pltpu.VMEM((LANES, N_HEADS * tq), F32)],
        compiler_params=_params("parallel", "parallel"),
        name="dsa_attention",
    )(qt, qit, wt, qn2, kn2, kk_t, vt)
    out = out_t.reshape(b, n_q, HEAD_DIM, N_HEADS, tq).transpose(0, 1, 4, 3, 2)
    return out.reshape(b, s, ATTN_W)


def _ssm_kernel(u_ref, toep_ref, win_ref, wout_ref, apow_ref, y_ref, hloc_ref, hswap_ref, hprev_ref, *, batch):
    rows, width = u_ref.shape
    sub = toep_ref.shape[0]
    n_sub = width // sub
    rot = lambda h, r: apow_ref[r:r + 1, :] * h + apow_ref[r + 1:r + 2, :] * pltpu.roll(h, SSM_STATE, 1)
    h = jnp.zeros((rows, 2 * SSM_STATE), F32)
    local = []
    for j in range(n_sub):
        h = rot(h, 0) + _dot(u_ref[:, j * sub:(j + 1) * sub], win_ref[...])
        local.append(h.astype(MXU_DTYPE))
    hloc_ref[...] = h
    hswap_ref[...] = pltpu.roll(h, SSM_STATE, 1)
    a_same, a_swap = apow_ref[2:3, :], apow_ref[3:4, :]
    a_swap_rolled = pltpu.roll(a_swap, SSM_STATE, 1)
    h = jnp.zeros((batch, 2 * SSM_STATE), F32)
    h_sw = jnp.zeros((batch, 2 * SSM_STATE), F32)
    for c in range(rows // batch):
        hprev_ref[c * batch:(c + 1) * batch, :] = h
        h, h_sw = (a_same * h + a_swap * h_sw + hloc_ref[c * batch:(c + 1) * batch, :],
                   a_same * h_sw + a_swap_rolled * h + hswap_ref[c * batch:(c + 1) * batch, :])
    hprev = hprev_ref[...].astype(MXU_DTYPE)
    for j in range(n_sub):
        acc = _dot(hprev, wout_ref[:, j * sub:(j + 1) * sub]) + _dot(u_ref[:, j * sub:(j + 1) * sub], toep_ref[...])
        if j > 0:
            acc = acc + _dot(local[j - 1], wout_ref[:, :sub])
        y_ref[:, j * sub:(j + 1) * sub] = acc.astype(y_ref.dtype)


def _ssm_call(u_g, toep, win, wout, apow, layer, batch):
    g, rows, width = u_g.shape
    sub = toep.shape[2]
    per_group = lambda *shape: pl.BlockSpec((None, None) + shape, lambda gi: (layer, gi) + (0,) * len(shape))
    return pl.pallas_call(
        functools.partial(_ssm_kernel, batch=batch),
        out_shape=jax.ShapeDtypeStruct((g, rows, width), MXU_DTYPE),
        grid=(g,),
        in_specs=[pl.BlockSpec((None, rows, width), lambda gi: (gi, 0, 0)),
                  per_group(sub, sub), per_group(sub, 2 * SSM_STATE),
                  per_group(2 * SSM_STATE, width), per_group(4, 2 * SSM_STATE)],
        out_specs=pl.BlockSpec((None, rows, width), lambda gi: (gi, 0, 0)),
        scratch_shapes=[pltpu.VMEM((rows, 2 * SSM_STATE), F32)] * 3,
        compiler_params=_params("parallel"),
        name="s5_scan",
    )(u_g, toep, win, wout, apow)


def _ssm_operators(a_re, a_im, log_dt, b_re, b_im, c_re, c_im, d_skip, chunk):
    cmul = lambda xr, xi, yr, yi: (xr * yr - xi * yi, xr * yi + xi * yr)
    dt = jnp.exp(log_dt)[..., None]
    lam_re, lam_im = a_re * dt, a_im * dt
    lags = jnp.arange(chunk + 1, dtype=F32)[None, None, :, None]
    mag = jnp.exp(lam_re[:, :, None, :] * lags)
    ang = lam_im[:, :, None, :] * lags
    ap_re, ap_im = mag * jnp.cos(ang), mag * jnp.sin(ang)
    num_re, num_im = ap_re[:, :, 1] - 1.0, ap_im[:, :, 1]
    den = a_re * a_re + a_im * a_im
    coef_re, coef_im = (num_re * a_re + num_im * a_im) / den, (num_im * a_re - num_re * a_im) / den
    bb_re, bb_im = cmul(coef_re[..., None], coef_im[..., None], b_re, b_im)
    depth, g, p = a_re.shape
    gc = b_re.shape[-1]
    sub = SSM_SUB
    ca_re, ca_im = cmul(c_re[:, :, None], c_im[:, :, None], ap_re[:, :, :sub, None, :], ap_im[:, :, :sub, None, :])
    taps = (jnp.einsum('dgtop,dgpi->dgtoi', ca_re, bb_re, precision=lax.Precision.HIGHEST)
            - jnp.einsum('dgtop,dgpi->dgtoi', ca_im, bb_im, precision=lax.Precision.HIGHEST))
    taps = taps.at[:, :, 0].add(d_skip[..., None] * jnp.eye(gc, dtype=F32))
    taps = jnp.concatenate([jnp.zeros((depth, g, sub - 1, gc, gc), F32), taps], axis=2)
    lag_idx = jnp.arange(sub)[None, :] - jnp.arange(sub)[:, None] + (sub - 1)
    toep = taps[:, :, lag_idx]
    toep = toep.transpose(0, 1, 2, 5, 3, 4).reshape(depth, g, sub * gc, sub * gc)
    rev_re, rev_im = ap_re[:, :, sub - 1::-1][:, :, :sub], ap_im[:, :, sub - 1::-1][:, :, :sub]
    wi_re, wi_im = cmul(rev_re[..., None], rev_im[..., None], bb_re[:, :, None], bb_im[:, :, None])
    w_in = jnp.concatenate([wi_re, wi_im], axis=3)
    w_in = w_in.transpose(0, 1, 2, 4, 3).reshape(depth, g, sub * gc, 2 * p)
    wo_re, wo_im = cmul(c_re[:, :, None], c_im[:, :, None], ap_re[:, :, 1:, None, :], ap_im[:, :, 1:, None, :])
    w_out = jnp.concatenate([wo_re, -wo_im], axis=4)
    w_out = w_out.transpose(0, 1, 4, 2, 3).reshape(depth, g, 2 * p, chunk * gc)
    rot_rows = lambda lag: [jnp.concatenate([ap_re[:, :, lag], ap_re[:, :, lag]], -1),
                            jnp.concatenate([-ap_im[:, :, lag], ap_im[:, :, lag]], -1)]
    apow = jnp.stack(rot_rows(sub) + rot_rows(chunk), axis=2)
    return toep.astype(MXU_DTYPE), w_in.astype(MXU_DTYPE), w_out.astype(MXU_DTYPE), apow


def _outproj_kernel(x_ref, attn_ref, y_ref, mod_ref, ag_ref, sg_ref, gw_ref, gb_ref, woa_ref, wos_ref, o_ref):
    a = _rms(attn_ref[...].astype(F32), ag_ref[...]).astype(MXU_DTYPE)
    y = y_ref[...].astype(F32)
    y = 0.5 * y * (1.0 + jnp.tanh(math.sqrt(2.0 / math.pi) * (y + 0.044715 * (y * y * y))))
    y = y * jax.nn.sigmoid(_dot(y.astype(MXU_DTYPE), gw_ref[...]) + gb_ref[...])
    y = _rms(y, sg_ref[...]).astype(MXU_DTYPE)
    mixed = _dot(a, woa_ref[...]) + _dot(y, wos_ref[...])
    o_ref[...] = x_ref[...] + mod_ref[5:6, :] * mixed


def _outproj_call(x, attn, y, mod_l, attn_gain, ssm_gain, glu_w, glu_b, wo_a, wo_s, layer):
    b, s, d = x.shape
    tm = min(ROW_TILE, s)
    aw, sw = attn.shape[2], y.shape[2]
    row = lambda w: pl.BlockSpec((None, tm, w), lambda bi, i: (bi, i, 0))
    vec = lambda w: pl.BlockSpec((None, 1, w), lambda bi, i: (layer, 0, 0))
    wres = lambda r, c: _resident((None, r, c), lambda bi, i: (layer, 0, 0))
    return pl.pallas_call(
        _outproj_kernel,
        out_shape=jax.ShapeDtypeStruct(x.shape, F32),
        grid=(b, s // tm),
        in_specs=[row(d), row(aw), row(sw),
                  pl.BlockSpec((None, N_MOD, d), lambda bi, i: (bi, 0, 0)),
                  vec(aw), vec(sw), wres(sw, sw), vec(sw), wres(aw, d), wres(sw, d)],
        out_specs=row(d),
        compiler_params=_params("parallel", "parallel"),
        name="mixer_outproj",
    )(x, attn, y, mod_l, attn_gain[:, None, :], ssm_gain[:, None, :], glu_w, glu_b[:, None, :], wo_a, wo_s)


def kernel(x, c, positions, mod_w, mod_b, norm_g, ffn1_w1, ffn1_w3, ffn1_w2, ffn2_w1, ffn2_w3, ffn2_w2, w_in, w_out, attn_gain, ssm_gain, ssm_a_re, ssm_a_im, ssm_log_dt, ssm_b_re, ssm_b_im, ssm_c_re, ssm_c_im, ssm_d, glu_w, glu_b, final_g):
    b, s, d = x.shape
    depth = mod_w.shape[0]
    ssm_w = d - ATTN_W
    groups = ssm_w // SSM_GROUP
    chunk = min(SCAN_L, s)
    n_chunks = s // chunk
    cast = lambda a: a.astype(MXU_DTYPE)

    mod = _mod_call(c, mod_w, mod_b)
    tables = _rope_tables(positions)
    ffn1 = (cast(ffn1_w1), cast(ffn1_w3), cast(ffn1_w2))
    ffn2 = (cast(ffn2_w1), cast(ffn2_w3), cast(ffn2_w2))
    o_k = ATTN_W
    o_v = o_k + HEAD_DIM
    o_qi = o_v + HEAD_DIM
    o_ki = o_qi + IDX_HEADS * IDX_DIM
    o_wi = o_ki + IDX_DIM
    o_u = o_wi + IDX_HEADS
    w_in_t = w_in.transpose(0, 2, 1)
    pad_rows = lambda a, rows: jnp.concatenate([a, jnp.zeros((depth, rows - a.shape[1], d), a.dtype)], axis=1)
    wqt = cast(w_in_t[:, :o_k])
    wqit = cast(w_in_t[:, o_qi:o_ki])
    wkk = cast(jnp.concatenate([w_in[:, :, o_k:o_v], w_in[:, :, o_ki:o_wi]], axis=2))
    wvt = cast(pad_rows(w_in_t[:, o_v:o_qi], LANES))
    wwt = cast(pad_rows(w_in_t[:, o_wi:o_u], 8))
    wu = cast(w_in[:, :, o_u:])
    wo_a, wo_s = cast(w_out[:, :ATTN_W]), cast(w_out[:, ATTN_W:])
    glu_wc = cast(glu_w)
    toep, s_win, s_wout, apow = _ssm_operators(ssm_a_re, ssm_a_im, ssm_log_dt, ssm_b_re, ssm_b_im,
                                               ssm_c_re, ssm_c_im, ssm_d, chunk)

    for l in range(depth):
        x = _ffn_call(x, mod[l], norm_g[l, 0], *ffn1, l, 0)
        qt, qit, kk, vt, wt, u, qn2, kn2 = _inproj_call(x, mod[l], norm_g[l, 1], wqt, wqit, wkk, wvt, wwt, wu,
                                                        tables, l)
        attn = _dsa_call(qt, qit, wt, qn2, kn2, kk, vt)
        u_g = u.reshape(b, n_chunks, chunk, groups, SSM_GROUP).transpose(3, 1, 0, 2, 4)
        u_g = u_g.reshape(groups, n_chunks * b, chunk * SSM_GROUP)
        y_g = _ssm_call(u_g, toep, s_win, s_wout, apow, l, b)
        y = y_g.reshape(groups, n_chunks, b, chunk, SSM_GROUP).transpose(2, 1, 3, 0, 4).reshape(b, s, ssm_w)
        x = _outproj_call(x, attn, y, mod[l], attn_gain, ssm_gain, glu_wc, glu_b, wo_a, wo_s, l)
        x = _ffn_call(x, mod[l], norm_g[l, 2], *ffn2, l, 6, final_g=final_g if l == depth - 1 else None)
    return x
```

```python
import functools
import math

import jax
import jax.numpy as jnp
from jax import lax
from jax.experimental import pallas as pl
from jax.experimental.pallas import tpu as pltpu

F32 = jnp.float32
MXU_DTYPE = jnp.bfloat16

CHUNK = 64
N_HEADS = 8
HEAD_DIM = 64
ATTN_W = N_HEADS * HEAD_DIM
IDX_HEADS = 4
IDX_DIM = 64
TOPK_MAX = 256
SSM_GROUP = 16
SSM_STATE = 64
SCAN_L = 128
ROPE_THETA = 500000.0
ROT_DIM = HEAD_DIM // 4
EPS = 1e-6
N_MOD = 9

LANES = 128
MXU_WIDTH = 256
VMEM_LIMIT_BYTES = 56 * 1024 * 1024

INT_MIN = -(2 ** 31)
NEG_BIAS = -1e30

ROW_TILE = 512
Q_TILE = 256
LOGIT_SAFE = 40.0
COUNT_CHAINS = 4
TAU_PROBE_BIT = 23
TAU_TAIL_STEPS = 2
assert (31 - TAU_PROBE_BIT) % TAU_TAIL_STEPS == 0
SSM_SUB = 16


def _params(*sem):
    return pltpu.CompilerParams(dimension_semantics=sem, vmem_limit_bytes=VMEM_LIMIT_BYTES)


def _dot(a, b):
    return jnp.dot(a, b, preferred_element_type=F32)


def _rms(x, g):
    return x * lax.rsqrt(jnp.mean(x * x, axis=-1, keepdims=True) + EPS) * g


def _modulated_norm(x, g, mod_ref, base):
    shift = mod_ref[base:base + 1, :]
    scale = mod_ref[base + 1:base + 2, :]
    return _rms(x, g) * (1.0 + scale) + shift


def _mod_kernel(c_ref, w_ref, b_ref, o_ref):
    c = c_ref[...]
    o_ref[...] = _dot(c * jax.nn.sigmoid(c), w_ref[...]) + b_ref[...]


def _mod_call(c, mod_w, mod_b):
    depth, d, nd = mod_w.shape
    b = c.shape[0]
    rows = -(-b // 8) * 8
    c_pad = jnp.zeros((rows, d), F32).at[:b].set(c)
    out = pl.pallas_call(
        _mod_kernel,
        out_shape=jax.ShapeDtypeStruct((depth, rows, nd), F32),
        grid=(depth, nd // d),
        in_specs=[pl.BlockSpec((rows, d), lambda l, j: (0, 0)),
                  pl.BlockSpec((None, d, d), lambda l, j: (l, 0, j)),
                  pl.BlockSpec((None, 1, d), lambda l, j: (l, 0, j))],
        out_specs=pl.BlockSpec((None, rows, d), lambda l, j: (l, 0, j)),
        compiler_params=_params("parallel", "parallel"),
        name="adaln_mod",
    )(c_pad, mod_w, mod_b.reshape(depth, 1, nd))
    return out[:, :b].reshape(depth, b, N_MOD, d)


def _ffn_kernel(x_ref, mod_ref, g_ref, w1_ref, w3_ref, w2_ref, o_ref, *, base, ff_cuts, final_norm):
    x = x_ref[...]
    h = _modulated_norm(x, g_ref[0:1, :], mod_ref, base).astype(MXU_DTYPE)
    acc = jnp.zeros(x.shape, F32)
    for c0, c1 in zip(ff_cuts[:-1], ff_cuts[1:]):
        a = _dot(h, w1_ref[:, c0:c1])
        b = _dot(h, w3_ref[:, c0:c1])
        t = (a * jax.nn.sigmoid(a) * b).astype(MXU_DTYPE)
        acc = acc + _dot(t, w2_ref[c0:c1, :])
    y = x + 0.5 * mod_ref[base + 2:base + 3, :] * acc
    o_ref[...] = _rms(y, g_ref[1:2, :]) if final_norm else y


def _resident(shape, index_map):
    return pl.BlockSpec(shape, index_map, pipeline_mode=pl.Buffered(1))


def _ffn_call(x, mod_l, g, w1, w3, w2, layer, base, final_g=None):
    b, s, d = x.shape
    gains = jnp.stack([g, g if final_g is None else final_g])
    ff = w1.shape[2]
    tm = min(ROW_TILE, s)
    mid = -(-ff // (2 * MXU_WIDTH)) * MXU_WIDTH
    ff_cuts = (0, mid, ff) if 0 < mid < ff else (0, ff)
    wspec_in = _resident((None, d, ff), lambda bi, i: (layer, 0, 0))
    wspec_out = _resident((None, ff, d), lambda bi, i: (layer, 0, 0))
    return pl.pallas_call(
        functools.partial(_ffn_kernel, base=base, ff_cuts=ff_cuts, final_norm=final_g is not None),
        out_shape=jax.ShapeDtypeStruct(x.shape, F32),
        grid=(b, s // tm),
        in_specs=[pl.BlockSpec((None, tm, d), lambda bi, i: (bi, i, 0)),
                  pl.BlockSpec((None, N_MOD, d), lambda bi, i: (bi, 0, 0)),
                  pl.BlockSpec((2, d), lambda bi, i: (0, 0)),
                  wspec_in, wspec_in, wspec_out],
        out_specs=pl.BlockSpec((None, tm, d), lambda bi, i: (bi, i, 0)),
        compiler_params=_params("parallel", "parallel"),
        name="ffn",
    )(x, mod_l, gains, w1, w3, w2)


def _rope128(t, cos, s_lo, s_hi):
    half = ROT_DIM // 2
    return t * cos + pltpu.roll(t, LANES - half, 1) * s_lo + pltpu.roll(t, half, 1) * s_hi


def _dot_nt(a, b):
    return lax.dot_general(a, b, (((1,), (1,)), ((), ())), preferred_element_type=F32)


def _rope_rows(t, cos_t, sin_t):
    half = ROT_DIM // 2
    parts = []
    for base in range(0, t.shape[0], HEAD_DIM):
        t1, t2 = t[base:base + half], t[base + half:base + ROT_DIM]
        parts += [t1 * cos_t - t2 * sin_t, t2 * cos_t + t1 * sin_t, t[base + ROT_DIM:base + HEAD_DIM]]
    return jnp.concatenate(parts, axis=0)


def _inproj_kernel(x_ref, mod_ref, g_ref, wqt_ref, wqit_ref, wkk_ref, wvt_ref, wwt_ref, wu_ref,
                   cos_ref, slo_ref, shi_ref, cost_ref, sint_ref,
                   qt_ref, qit_ref, kk_ref, vt_ref, wt_ref, u_ref, qn_ref, kn_ref):
    h = _modulated_norm(x_ref[...], g_ref[...], mod_ref, 3).astype(MXU_DTYPE)
    cos_t, sin_t = cost_ref[...], sint_ref[...]
    qt = (_rope_rows(_dot_nt(wqt_ref[...], h), cos_t, sin_t) * HEAD_DIM ** -0.5).astype(qt_ref.dtype)
    qt_ref[...] = qt
    qit_ref[...] = _rope_rows(_dot_nt(wqit_ref[...], h), cos_t, sin_t).astype(qit_ref.dtype)
    kk = _rope128(_dot(h, wkk_ref[...]), cos_ref[...], slo_ref[...], shi_ref[...]).astype(kk_ref.dtype)
    kk_ref[...] = kk
    vt = _dot_nt(wvt_ref[...], h)
    row = lax.broadcasted_iota(jnp.int32, vt.shape, 0)
    vt_ref[...] = jnp.where(row == HEAD_DIM, 1.0, vt).astype(vt_ref.dtype)
    wt_ref[...] = _dot_nt(wwt_ref[...], h)
    u_ref[...] = _dot(h, wu_ref[...]).astype(u_ref.dtype)
    q2 = qt.astype(F32) * qt.astype(F32)
    qn_ref[...] = jnp.max(jnp.concatenate(
        [jnp.sum(q2[b0:b0 + HEAD_DIM], axis=0, keepdims=True) for b0 in range(0, q2.shape[0], HEAD_DIM)], axis=0),
        keepdims=True)
    k2 = kk.astype(F32) * kk.astype(F32)
    lane = lax.broadcasted_iota(jnp.int32, k2.shape, 1)
    kn_ref[...] = jnp.max(jnp.sum(jnp.where(lane < HEAD_DIM, k2, 0.0), axis=1, keepdims=True), keepdims=True)


def _inproj_call(x, mod_l, g, wqt, wqit, wkk, wvt, wwt, wu, tables, layer):
    b, s, d = x.shape
    tm = min(ROW_TILE, s)
    n_t = s // tm
    row = lambda w: pl.BlockSpec((None, tm, w), lambda bi, i: (bi, i, 0))
    col = lambda hgt: pl.BlockSpec((None, hgt, tm), lambda bi, i: (bi, 0, i))
    wres = lambda r, c: _resident((None, r, c), lambda bi, i: (layer, 0, 0))
    one = pl.BlockSpec((None, None, 1, 1), lambda bi, i: (bi, i, 0, 0))
    ssm_w = wu.shape[2]
    return pl.pallas_call(
        _inproj_kernel,
        out_shape=(jax.ShapeDtypeStruct((b, ATTN_W, s), MXU_DTYPE),
                   jax.ShapeDtypeStruct((b, IDX_HEADS * IDX_DIM, s), MXU_DTYPE),
                   jax.ShapeDtypeStruct((b, s, LANES), MXU_DTYPE),
                   jax.ShapeDtypeStruct((b, n_t, LANES, tm), MXU_DTYPE),
                   jax.ShapeDtypeStruct((b, 8, s), F32),
                   jax.ShapeDtypeStruct((b, s, ssm_w), MXU_DTYPE),
                   jax.ShapeDtypeStruct((b, n_t, 1, 1), F32),
                   jax.ShapeDtypeStruct((b, n_t, 1, 1), F32)),
        grid=(b, n_t),
        in_specs=[row(d),
                  pl.BlockSpec((None, N_MOD, d), lambda bi, i: (bi, 0, 0)),
                  pl.BlockSpec((1, d), lambda bi, i: (0, 0)),
                  wres(ATTN_W, d), wres(IDX_HEADS * IDX_DIM, d), wres(d, LANES), wres(LANES, d), wres(8, d),
                  wres(d, ssm_w),
                  row(LANES), row(LANES), row(LANES), col(ROT_DIM // 2), col(ROT_DIM // 2)],
        out_specs=(col(ATTN_W), col(IDX_HEADS * IDX_DIM), row(LANES),
                   pl.BlockSpec((None, None, LANES, tm), lambda bi, i: (bi, i, 0, 0)),
                   col(8), row(ssm_w), one, one),
        compiler_params=_params("parallel", "parallel"),
        name="mixer_inproj",
    )(x, mod_l, g.reshape(1, d), wqt, wqit, wkk, wvt, wwt, wu, *tables)


def _rope_tables(positions):
    inv_freq = 1.0 / (ROPE_THETA ** (jnp.arange(0, ROT_DIM, 2, dtype=F32) / ROT_DIM))
    ang = positions.astype(F32)[..., None] * inv_freq
    cos, sin = jnp.cos(ang), jnp.sin(ang)
    half = ROT_DIM // 2
    rest = HEAD_DIM - ROT_DIM
    pad = lambda *parts: jnp.tile(jnp.concatenate(parts, axis=-1), (1, 1, LANES // HEAD_DIM))
    zeros = lambda n: jnp.zeros(cos.shape[:-1] + (n,), F32)
    ones = jnp.ones(cos.shape[:-1] + (rest,), F32)
    return (pad(cos, cos, ones),
            pad(-sin, zeros(half), zeros(rest)),
            pad(zeros(half), sin, zeros(rest)),
            cos.transpose(0, 2, 1), sin.transpose(0, 2, 1))


def _dsa_kernel(qt_ref, qit_ref, wt_ref, qn_ref, kn_ref, kk_ref, vt_ref, o_ref,
                key_ref, bias_ref, acc_ref, *, topk):
    tq = qt_ref.shape[1]
    tk = kk_ref.shape[1]
    i = pl.program_id(1)
    n_tiles = ((i + 1) * tq + tk - 1) // tk
    col = lax.broadcasted_iota(jnp.int32, (1, tq), 1) + i * tq
    q_lim = (col // CHUNK + 1) * CHUNK
    key_pos = lax.broadcasted_iota(jnp.int32, (tk, tq), 0)
    zeros_half = jnp.zeros((HEAD_DIM, tq), MXU_DTYPE)

    qit = qit_ref[...]
    wqi = jnp.concatenate([jnp.concatenate([zeros_half, qit[h * IDX_DIM:(h + 1) * IDX_DIM, :]], axis=0)
                           for h in range(IDX_HEADS)], axis=1)
    w = wt_ref[...] * (IDX_DIM ** -0.5 * IDX_HEADS ** -0.5)
    w_row = jnp.concatenate([w[h:h + 1, :] for h in range(IDX_HEADS)], axis=1)

    def write_keys(t, masked):
        rel = jnp.maximum(_dot(kk_ref[t], wqi), 0.0) * w_row
        s = rel[:, :tq]
        for h in range(1, IDX_HEADS):
            s = s + rel[:, h * tq:(h + 1) * tq]
        s = s + 0.0
        bits = pltpu.bitcast(s, jnp.int32)
        key = jnp.where(bits < 0, bits ^ jnp.int32(0x7FFFFFFF), bits)
        if masked:
            key = jnp.where(key_pos + t * tk < q_lim, key, jnp.int32(INT_MIN))
        key_ref[t] = key

    def score_body(t, carry):
        write_keys(t, False)
        return carry

    lax.fori_loop(0, n_tiles - 1, score_body, 0)
    write_keys(n_tiles - 1, True)

    def count(thr, strict):
        thr_b = jnp.broadcast_to(thr, (8, tq))

        def body(t, accs):
            accs = list(accs)
            for r in range(0, tk, 8):
                key = key_ref[t, r:r + 8, :]
                hit = jnp.where(key > thr_b if strict else key >= thr_b, 1.0, 0.0)
                accs[(r // 8) % COUNT_CHAINS] = accs[(r // 8) % COUNT_CHAINS] + hit
            return tuple(accs)

        accs = lax.fori_loop(0, n_tiles, body, tuple(jnp.zeros((8, tq), F32) for _ in range(COUNT_CHAINS)))
        total = accs[0]
        for a in accs[1:]:
            total = total + a
        return jnp.sum(total, axis=0, keepdims=True)

    k_f = float(topk)
    n0 = count(jnp.zeros((1, tq), jnp.int32), False)
    nonneg = n0 >= k_f
    tau0 = jnp.where(nonneg, jnp.int32(0), jnp.int32(INT_MIN))
    cnt0 = jnp.where(nonneg, n0, (n_tiles * tk).astype(F32))

    def refine(it, tau, cnt):
        trial = tau | lax.shift_left(jnp.int32(1), 30 - it)
        c = count(trial, False)
        ok = c >= k_f
        return jnp.where(ok, trial, tau), jnp.where(ok, c, cnt)

    tau, cnt = lax.fori_loop(0, TAU_PROBE_BIT, lambda it, tc: refine(it, *tc), (tau0, cnt0))
    locked = count(tau + 1, False) < k_f

    def unsettled(tau_cnt):
        return jnp.max(jnp.where(locked | (tau_cnt[1] == k_f), 0.0, 1.0))

    def tail_cond(carry):
        it, _, _, open_rows = carry
        return jnp.logical_and(it < 31, open_rows > 0.0)

    def tail_body(carry):
        it, tau, cnt, _ = carry
        for step in range(TAU_TAIL_STEPS):
            tau, cnt = refine(it + step, tau, cnt)
        return it + TAU_TAIL_STEPS, tau, cnt, unsettled((tau, cnt))

    _, tau, _, _ = lax.while_loop(tail_cond, tail_body,
                                  (jnp.int32(TAU_PROBE_BIT), tau, cnt, unsettled((tau, cnt))))
    need = k_f - count(tau, True)

    lower = (lax.broadcasted_iota(jnp.int32, (tk, tk), 1)
             <= lax.broadcasted_iota(jnp.int32, (tk, tk), 0)).astype(MXU_DTYPE)

    def bias_body(t, ties_before):
        key = key_ref[t]
        tie = jnp.where(key == tau, 1.0, 0.0)
        rank = ties_before + _dot(lower, tie.astype(MXU_DTYPE))
        sel = (key > tau) | ((key == tau) & (rank <= need))
        sel = sel & (key > jnp.int32(INT_MIN))
        bias_ref[t] = jnp.where(sel, 0.0, NEG_BIAS)
        return rank[tk - 1:tk, :]

    lax.fori_loop(0, n_tiles, bias_body, jnp.zeros((1, tq), F32))

    qt = qt_ref[...]
    wq = jnp.concatenate([jnp.concatenate([qt[h * HEAD_DIM:(h + 1) * HEAD_DIM, :], zeros_half], axis=0)
                          for h in range(N_HEADS)], axis=1)

    def logits(t):
        return _dot(kk_ref[t], wq) + jnp.concatenate([bias_ref[t]] * N_HEADS, axis=1)

    safe = jnp.max(qn_ref[...] * kn_ref[...]) <= LOGIT_SAFE * LOGIT_SAFE

    def exact_max():
        def max_body(t, m):
            s = logits(t)
            part = s[:8, :]
            for r in range(8, tk, 8):
                part = jnp.maximum(part, s[r:r + 8, :])
            return jnp.maximum(m, part)
        m = lax.fori_loop(0, n_tiles, max_body, jnp.full((8, N_HEADS * tq), NEG_BIAS, F32))
        return jnp.max(m, axis=0, keepdims=True)

    offset = lax.cond(safe, lambda: jnp.zeros((1, N_HEADS * tq), F32), exact_max)
    acc_ref[...] = jnp.zeros(acc_ref.shape, F32)

    def att_body(t, carry):
        p = jnp.exp(logits(t) - offset)
        acc_ref[...] += _dot(vt_ref[t], p.astype(MXU_DTYPE))
        return carry

    lax.fori_loop(0, n_tiles, att_body, 0)
    acc = acc_ref[...]
    o_ref[...] = (acc[:HEAD_DIM, :] / acc[HEAD_DIM:HEAD_DIM + 1, :]).astype(o_ref.dtype)


def _dsa_call(qt, qit, wt, qn2, kn2, kk, vt):
    b, _, s = qt.shape
    tq = min(Q_TILE, s)
    tk = vt.shape[3]
    n_kt = s // tk
    n_q = s // tq
    topk = min(TOPK_MAX, s // 4)
    assert tk % tq == 0 and s % (tq * qn2.shape[1]) == 0
    q_per_norm_tile = s // qn2.shape[1] // tq
    kk_t = kk.reshape(b, n_kt, tk, LANES)
    kn2 = jnp.max(kn2, axis=1)
    per_batch = lambda shape: pl.BlockSpec((None,) + shape, lambda bi, i: (bi, 0, 0, 0))
    qcol = lambda height: pl.BlockSpec((None, height, tq), lambda bi, i: (bi, 0, i))
    out_t = pl.pallas_call(
        functools.partial(_dsa_kernel, topk=topk),
        out_shape=jax.ShapeDtypeStruct((b, n_q, HEAD_DIM, N_HEADS * tq), MXU_DTYPE),
        grid=(b, n_q),
        in_specs=[qcol(ATTN_W), qcol(IDX_HEADS * IDX_DIM), qcol(8),
                  pl.BlockSpec((None, None, 1, 1), lambda bi, i: (bi, i // q_per_norm_tile, 0, 0)),
                  pl.BlockSpec((None, 1, 1), lambda bi, i: (bi, 0, 0)),
                  per_batch((n_kt, tk, LANES)), per_batch((n_kt, LANES, tk))],
        out_specs=pl.BlockSpec((None, None, HEAD_DIM, N_HEADS * tq), lambda bi, i: (bi, i, 0, 0)),
        scratch_shapes=[pltpu.VMEM((n_kt, tk, tq), jnp.int32),
                        pltpu.VMEM((n_kt, tk, tq), F32),
                        pltpu.VMEM((LANES, N_HEADS * tq), F32)],
        compiler_params=_params("parallel", "parallel"),
        name="dsa_attention",
    )(qt, qit, wt, qn2, kn2, kk_t, vt)
    out = out_t.reshape(b, n_q, HEAD_DIM, N_HEADS, tq).transpose(0, 1, 4, 3, 2)
    return out.reshape(b, s, ATTN_W)


def _ssm_kernel(u_ref, toep_ref, win_ref, wout_ref, apow_ref, y_ref, hloc_ref, hswap_ref, hprev_ref, *, batch):
    rows, width = u_ref.shape
    sub = toep_ref.shape[0]
    n_sub = width // sub
    rot = lambda h, r: apow_ref[r:r + 1, :] * h + apow_ref[r + 1:r + 2, :] * pltpu.roll(h, SSM_STATE, 1)
    h = jnp.zeros((rows, 2 * SSM_STATE), F32)
    local = []
    for j in range(n_sub):
        h = rot(h, 0) + _dot(u_ref[:, j * sub:(j + 1) * sub], win_ref[...])
        local.append(h.astype(MXU_DTYPE))
    hloc_ref[...] = h
    hswap_ref[...] = pltpu.roll(h, SSM_STATE, 1)
    a_same, a_swap = apow_ref[2:3, :], apow_ref[3:4, :]
    a_swap_rolled = pltpu.roll(a_swap, SSM_STATE, 1)
    h = jnp.zeros((batch, 2 * SSM_STATE), F32)
    h_sw = jnp.zeros((batch, 2 * SSM_STATE), F32)
    for c in range(rows // batch):
        hprev_ref[c * batch:(c + 1) * batch, :] = h
        h, h_sw = (a_same * h + a_swap * h_sw + hloc_ref[c * batch:(c + 1) * batch, :],
                   a_same * h_sw + a_swap_rolled * h + hswap_ref[c * batch:(c + 1) * batch, :])
    hprev = hprev_ref[...].astype(MXU_DTYPE)
    for j in range(n_sub):
        acc = _dot(hprev, wout_ref[:, j * sub:(j + 1) * sub]) + _dot(u_ref[:, j * sub:(j + 1) * sub], toep_ref[...])
        if j > 0:
            acc = acc + _dot(local[j - 1], wout_ref[:, :sub])
        y_ref[:, j * sub:(j + 1) * sub] = acc.astype(y_ref.dtype)


def _ssm_call(u_g, toep, win, wout, apow, layer, batch):
    g, rows, width = u_g.shape
    sub = toep.shape[2]
    per_group = lambda *shape: pl.BlockSpec((None, None) + shape, lambda gi: (layer, gi) + (0,) * len(shape))
    return pl.pallas_call(
        functools.partial(_ssm_kernel, batch=batch),
        out_shape=jax.ShapeDtypeStruct((g, rows, width), MXU_DTYPE),
        grid=(g,),
        in_specs=[pl.BlockSpec((None, rows, width), lambda gi: (gi, 0, 0)),
                  per_group(sub, sub), per_group(sub, 2 * SSM_STATE),
                  per_group(2 * SSM_STATE, width), per_group(4, 2 * SSM_STATE)],
        out_specs=pl.BlockSpec((None, rows, width), lambda gi: (gi, 0, 0)),
        scratch_shapes=[pltpu.VMEM((rows, 2 * SSM_STATE), F32)] * 3,
        compiler_params=_params("parallel"),
        name="s5_scan",
    )(u_g, toep, win, wout, apow)


def _ssm_operators(a_re, a_im, log_dt, b_re, b_im, c_re, c_im, d_skip, chunk):
    cmul = lambda xr, xi, yr, yi: (xr * yr - xi * yi, xr * yi + xi * yr)
    dt = jnp.exp(log_dt)[..., None]
    lam_re, lam_im = a_re * dt, a_im * dt
    lags = jnp.arange(chunk + 1, dtype=F32)[None, None, :, None]
    mag = jnp.exp(lam_re[:, :, None, :] * lags)
    ang = lam_im[:, :, None, :] * lags
    ap_re, ap_im = mag * jnp.cos(ang), mag * jnp.sin(ang)
    num_re, num_im = ap_re[:, :, 1] - 1.0, ap_im[:, :, 1]
    den = a_re * a_re + a_im * a_im
    coef_re, coef_im = (num_re * a_re + num_im * a_im) / den, (num_im * a_re - num_re * a_im) / den
    bb_re, bb_im = cmul(coef_re[..., None], coef_im[..., None], b_re, b_im)
    depth, g, p = a_re.shape
    gc = b_re.shape[-1]
    sub = SSM_SUB
    ca_re, ca_im = cmul(c_re[:, :, None], c_im[:, :, None], ap_re[:, :, :sub, None, :], ap_im[:, :, :sub, None, :])
    taps = (jnp.einsum('dgtop,dgpi->dgtoi', ca_re, bb_re, precision=lax.Precision.HIGHEST)
            - jnp.einsum('dgtop,dgpi->dgtoi', ca_im, bb_im, precision=lax.Precision.HIGHEST))
    taps = taps.at[:, :, 0].add(d_skip[..., None] * jnp.eye(gc, dtype=F32))
    taps = jnp.concatenate([jnp.zeros((depth, g, sub - 1, gc, gc), F32), taps], axis=2)
    lag_idx = jnp.arange(sub)[None, :] - jnp.arange(sub)[:, None] + (sub - 1)
    toep = taps[:, :, lag_idx]
    toep = toep.transpose(0, 1, 2, 5, 3, 4).reshape(depth, g, sub * gc, sub * gc)
    rev_re, rev_im = ap_re[:, :, sub - 1::-1][:, :, :sub], ap_im[:, :, sub - 1::-1][:, :, :sub]
    wi_re, wi_im = cmul(rev_re[..., None], rev_im[..., None], bb_re[:, :, None], bb_im[:, :, None])
    w_in = jnp.concatenate([wi_re, wi_im], axis=3)
    w_in = w_in.transpose(0, 1, 2, 4, 3).reshape(depth, g, sub * gc, 2 * p)
    wo_re, wo_im = cmul(c_re[:, :, None], c_im[:, :, None], ap_re[:, :, 1:, None, :], ap_im[:, :, 1:, None, :])
    w_out = jnp.concatenate([wo_re, -wo_im], axis=4)
    w_out = w_out.transpose(0, 1, 4, 2, 3).reshape(depth, g, 2 * p, chunk * gc)
    rot_rows = lambda lag: [jnp.concatenate([ap_re[:, :, lag], ap_re[:, :, lag]], -1),
                            jnp.concatenate([-ap_im[:, :, lag], ap_im[:, :, lag]], -1)]
    apow = jnp.stack(rot_rows(sub) + rot_rows(chunk), axis=2)
    return toep.astype(MXU_DTYPE), w_in.astype(MXU_DTYPE), w_out.astype(MXU_DTYPE), apow


def _outproj_kernel(x_ref, attn_ref, y_ref, mod_ref, ag_ref, sg_ref, gw_ref, gb_ref, woa_ref, wos_ref, o_ref):
    a = _rms(attn_ref[...].astype(F32), ag_ref[...]).astype(MXU_DTYPE)
    y = y_ref[...].astype(F32)
    y = 0.5 * y * (1.0 + jnp.tanh(math.sqrt(2.0 / math.pi) * (y + 0.044715 * (y * y * y))))
    y = y * jax.nn.sigmoid(_dot(y.astype(MXU_DTYPE), gw_ref[...]) + gb_ref[...])
    y = _rms(y, sg_ref[...]).astype(MXU_DTYPE)
    mixed = _dot(a, woa_ref[...]) + _dot(y, wos_ref[...])
    o_ref[...] = x_ref[...] + mod_ref[5:6, :] * mixed


def _outproj_call(x, attn, y, mod_l, attn_gain, ssm_gain, glu_w, glu_b, wo_a, wo_s, layer):
    b, s, d = x.shape
    tm = min(ROW_TILE, s)
    aw, sw = attn.shape[2], y.shape[2]
    row = lambda w: pl.BlockSpec((None, tm, w), lambda bi, i: (bi, i, 0))
    vec = lambda w: pl.BlockSpec((None, 1, w), lambda bi, i: (layer, 0, 0))
    wres = lambda r, c: _resident((None, r, c), lambda bi, i: (layer, 0, 0))
    return pl.pallas_call(
        _outproj_kernel,
        out_shape=jax.ShapeDtypeStruct(x.shape, F32),
        grid=(b, s // tm),
        in_specs=[row(d), row(aw), row(sw),
                  pl.BlockSpec((None, N_MOD, d), lambda bi, i: (bi, 0, 0)),
                  vec(aw), vec(sw), wres(sw, sw), vec(sw), wres(aw, d), wres(sw, d)],
        out_specs=row(d),
        compiler_params=_params("parallel", "parallel"),
        name="mixer_outproj",
    )(x, attn, y, mod_l, attn_gain[:, None, :], ssm_gain[:, None, :], glu_w, glu_b[:, None, :], wo_a, wo_s)


def kernel(x, c, positions, mod_w, mod_b, norm_g, ffn1_w1, ffn1_w3, ffn1_w2, ffn2_w1, ffn2_w3, ffn2_w2, w_in, w_out, attn_gain, ssm_gain, ssm_a_re, ssm_a_im, ssm_log_dt, ssm_b_re, ssm_b_im, ssm_c_re, ssm_c_im, ssm_d, glu_w, glu_b, final_g):
    b, s, d = x.shape
    depth = mod_w.shape[0]
    ssm_w = d - ATTN_W
    groups = ssm_w // SSM_GROUP
    chunk = min(SCAN_L, s)
    n_chunks = s // chunk
    cast = lambda a: a.astype(MXU_DTYPE)

    mod = _mod_call(c, mod_w, mod_b)
    tables = _rope_tables(positions)
    ffn1 = (cast(ffn1_w1), cast(ffn1_w3), cast(ffn1_w2))
    ffn2 = (cast(ffn2_w1), cast(ffn2_w3), cast(ffn2_w2))
    o_k = ATTN_W
    o_v = o_k + HEAD_DIM
    o_qi = o_v + HEAD_DIM
    o_ki = o_qi + IDX_HEADS * IDX_DIM
    o_wi = o_ki + IDX_DIM
    o_u = o_wi + IDX_HEADS
    w_in_t = w_in.transpose(0, 2, 1)
    pad_rows = lambda a, rows: jnp.concatenate([a, jnp.zeros((depth, rows - a.shape[1], d), a.dtype)], axis=1)
    wqt = cast(w_in_t[:, :o_k])
    wqit = cast(w_in_t[:, o_qi:o_ki])
    wkk = cast(jnp.concatenate([w_in[:, :, o_k:o_v], w_in[:, :, o_ki:o_wi]], axis=2))
    wvt = cast(pad_rows(w_in_t[:, o_v:o_qi], LANES))
    wwt = cast(pad_rows(w_in_t[:, o_wi:o_u], 8))
    wu = cast(w_in[:, :, o_u:])
    wo_a, wo_s = cast(w_out[:, :ATTN_W]), cast(w_out[:, ATTN_W:])
    glu_wc = cast(glu_w)
    toep, s_win, s_wout, apow = _ssm_operators(ssm_a_re, ssm_a_im, ssm_log_dt, ssm_b_re, ssm_b_im,
                                               ssm_c_re, ssm_c_im, ssm_d, chunk)

    for l in range(depth):
        x = _ffn_call(x, mod[l], norm_g[l, 0], *ffn1, l, 0)
        qt, qit, kk, vt, wt, u, qn2, kn2 = _inproj_call(x, mod[l], norm_g[l, 1], wqt, wqit, wkk, wvt, wwt, wu,
                                                        tables, l)
        attn = _dsa_call(qt, qit, wt, qn2, kn2, kk, vt)
        u_g = u.reshape(b, n_chunks, chunk, groups, SSM_GROUP).transpose(3, 1, 0, 2, 4)
        u_g = u_g.reshape(groups, n_chunks * b, chunk * SSM_GROUP)
        y_g = _ssm_call(u_g, toep, s_win, s_wout, apow, l, b)
        y = y_g.reshape(groups, n_chunks, b, chunk, SSM_GROUP).transpose(2, 1, 3, 0, 4).reshape(b, s, ssm_w)
        x = _outproj_call(x, attn, y, mod[l], attn_gain, ssm_gain, glu_wc, glu_b, wo_a, wo_s, l)
        x = _ffn_call(x, mod[l], norm_g[l, 2], *ffn2, l, 6, final_g=final_g if l == depth - 1 else None)
    return x
```

```python
import functools
import math

import jax
import jax.numpy as jnp
from jax import lax
from jax.experimental import pallas as pl
from jax.experimental.pallas import tpu as pltpu

F32 = jnp.float32
MXU_DTYPE = jnp.bfloat16

CHUNK = 64
N_HEADS = 8
HEAD_DIM = 64
ATTN_W = N_HEADS * HEAD_DIM
IDX_HEADS = 4
IDX_DIM = 64
TOPK_MAX = 256
SSM_GROUP = 16
SSM_STATE = 64
SCAN_L = 128
ROPE_THETA = 500000.0
ROT_DIM = HEAD_DIM // 4
EPS = 1e-6
N_MOD = 9

LANES = 128
MXU_WIDTH = 256
BF16_SUBLANES = 16
VT_ROWS = -(-(HEAD_DIM + 1) // BF16_SUBLANES) * BF16_SUBLANES
VMEM_LIMIT_BYTES = 56 * 1024 * 1024

INT_MIN = -(2 ** 31)
NEG_BIAS = -1e30

ROW_TILE = 512
Q_TILE = 256
LOGIT_SAFE = 40.0
COUNT_CHAINS = 4
TAU_PROBE_BIT = 23
TAU_TAIL_STEPS = 2
assert (31 - TAU_PROBE_BIT) % TAU_TAIL_STEPS == 0
SSM_SUB = 16


def _params(*sem):
    return pltpu.CompilerParams(dimension_semantics=sem, vmem_limit_bytes=VMEM_LIMIT_BYTES)


def _dot(a, b):
    return jnp.dot(a, b, preferred_element_type=F32)


def _rms(x, g):
    return x * lax.rsqrt(jnp.mean(x * x, axis=-1, keepdims=True) + EPS) * g


def _modulated_norm(x, g, mod_ref, base):
    shift = mod_ref[base:base + 1, :]
    scale = mod_ref[base + 1:base + 2, :]
    return _rms(x, g) * (1.0 + scale) + shift


def _mod_kernel(c_ref, w_ref, b_ref, o_ref):
    c = c_ref[...]
    o_ref[...] = _dot(c * jax.nn.sigmoid(c), w_ref[...]) + b_ref[...]


def _mod_call(c, mod_w, mod_b):
    depth, d, nd = mod_w.shape
    b = c.shape[0]
    rows = -(-b // 8) * 8
    c_pad = jnp.zeros((rows, d), F32).at[:b].set(c)
    out = pl.pallas_call(
        _mod_kernel,
        out_shape=jax.ShapeDtypeStruct((depth, rows, nd), F32),
        grid=(depth, nd // d),
        in_specs=[pl.BlockSpec((rows, d), lambda l, j: (0, 0)),
                  pl.BlockSpec((None, d, d), lambda l, j: (l, 0, j)),
                  pl.BlockSpec((None, 1, d), lambda l, j: (l, 0, j))],
        out_specs=pl.BlockSpec((None, rows, d), lambda l, j: (l, 0, j)),
        compiler_params=_params("parallel", "parallel"),
        name="adaln_mod",
    )(c_pad, mod_w, mod_b.reshape(depth, 1, nd))
    return out[:, :b].reshape(depth, b, N_MOD, d)


def _ffn_kernel(x_ref, mod_ref, g_ref, w1_ref, w3_ref, w2_ref, o_ref, *, base, ff_cuts, final_norm):
    x = x_ref[...]
    h = _modulated_norm(x, g_ref[0:1, :], mod_ref, base).astype(MXU_DTYPE)
    acc = jnp.zeros(x.shape, F32)
    for c0, c1 in zip(ff_cuts[:-1], ff_cuts[1:]):
        a = _dot(h, w1_ref[:, c0:c1])
        b = _dot(h, w3_ref[:, c0:c1])
        t = (a * jax.nn.sigmoid(a) * b).astype(MXU_DTYPE)
        acc = acc + _dot(t, w2_ref[c0:c1, :])
    y = x + 0.5 * mod_ref[base + 2:base + 3, :] * acc
    o_ref[...] = _rms(y, g_ref[1:2, :]) if final_norm else y


def _resident(shape, index_map):
    return pl.BlockSpec(shape, index_map, pipeline_mode=pl.Buffered(1))


def _ffn_call(x, mod_l, g, w1, w3, w2, layer, base, final_g=None):
    b, s, d = x.shape
    gains = jnp.stack([g, g if final_g is None else final_g])
    ff = w1.shape[2]
    tm = min(ROW_TILE, s)
    mid = -(-ff // (2 * MXU_WIDTH)) * MXU_WIDTH
    ff_cuts = (0, mid, ff) if 0 < mid < ff else (0, ff)
    wspec_in = _resident((None, d, ff), lambda bi, i: (layer, 0, 0))
    wspec_out = _resident((None, ff, d), lambda bi, i: (layer, 0, 0))
    return pl.pallas_call(
        functools.partial(_ffn_kernel, base=base, ff_cuts=ff_cuts, final_norm=final_g is not None),
        out_shape=jax.ShapeDtypeStruct(x.shape, F32),
        grid=(b, s // tm),
        in_specs=[pl.BlockSpec((None, tm, d), lambda bi, i: (bi, i, 0)),
                  pl.BlockSpec((None, N_MOD, d), lambda bi, i: (bi, 0, 0)),
                  pl.BlockSpec((2, d), lambda bi, i: (0, 0)),
                  wspec_in, wspec_in, wspec_out],
        out_specs=pl.BlockSpec((None, tm, d), lambda bi, i: (bi, i, 0)),
        compiler_params=_params("parallel", "parallel"),
        name="ffn",
    )(x, mod_l, gains, w1, w3, w2)


def _rope128(t, cos, s_lo, s_hi):
    half = ROT_DIM // 2
    return t * cos + pltpu.roll(t, LANES - half, 1) * s_lo + pltpu.roll(t, half, 1) * s_hi


def _dot_nt(a, b):
    return lax.dot_general(a, b, (((1,), (1,)), ((), ())), preferred_element_type=F32)


def _rope_rows(t, cos_t, sin_t):
    half = ROT_DIM // 2
    parts = []
    for base in range(0, t.shape[0], HEAD_DIM):
        t1, t2 = t[base:base + half], t[base + half:base + ROT_DIM]
        parts += [t1 * cos_t - t2 * sin_t, t2 * cos_t + t1 * sin_t, t[base + ROT_DIM:base + HEAD_DIM]]
    return jnp.concatenate(parts, axis=0)


def _inproj_kernel(x_ref, mod_ref, g_ref, wqt_ref, wqit_ref, wkk_ref, wvt_ref, wwt_ref, wu_ref,
                   cos_ref, slo_ref, shi_ref, cost_ref, sint_ref,
                   qt_ref, qit_ref, kk_ref, vt_ref, wt_ref, u_ref, qn_ref, kn_ref):
    h = _modulated_norm(x_ref[...], g_ref[...], mod_ref, 3).astype(MXU_DTYPE)
    cos_t, sin_t = cost_ref[...], sint_ref[...]
    qt = (_rope_rows(_dot_nt(wqt_ref[...], h), cos_t, sin_t) * HEAD_DIM ** -0.5).astype(qt_ref.dtype)
    qt_ref[...] = qt
    qit_ref[...] = _rope_rows(_dot_nt(wqit_ref[...], h), cos_t, sin_t).astype(qit_ref.dtype)
    kk = _rope128(_dot(h, wkk_ref[...]), cos_ref[...], slo_ref[...], shi_ref[...]).astype(kk_ref.dtype)
    kk_ref[...] = kk
    vt = _dot_nt(wvt_ref[...], h)
    row = lax.broadcasted_iota(jnp.int32, vt.shape, 0)
    vt_ref[...] = jnp.where(row == HEAD_DIM, 1.0, vt).astype(vt_ref.dtype)
    wt_ref[...] = _dot_nt(wwt_ref[...], h)
    u_ref[...] = _dot(h, wu_ref[...]).astype(u_ref.dtype)
    q2 = qt.astype(F32) * qt.astype(F32)
    qn_ref[...] = jnp.max(jnp.concatenate(
        [jnp.sum(q2[b0:b0 + HEAD_DIM], axis=0, keepdims=True) for b0 in range(0, q2.shape[0], HEAD_DIM)], axis=0),
        keepdims=True)
    k2 = kk.astype(F32) * kk.astype(F32)
    lane = lax.broadcasted_iota(jnp.int32, k2.shape, 1)
    kn_ref[...] = jnp.max(jnp.sum(jnp.where(lane < HEAD_DIM, k2, 0.0), axis=1, keepdims=True), keepdims=True)


def _inproj_call(x, mod_l, g, wqt, wqit, wkk, wvt, wwt, wu, tables, layer):
    b, s, d = x.shape
    tm = min(ROW_TILE, s)
    n_t = s // tm
    row = lambda w: pl.BlockSpec((None, tm, w), lambda bi, i: (bi, i, 0))
    col = lambda hgt: pl.BlockSpec((None, hgt, tm), lambda bi, i: (bi, 0, i))
    wres = lambda r, c: _resident((None, r, c), lambda bi, i: (layer, 0, 0))
    one = pl.BlockSpec((None, None, 1, 1), lambda bi, i: (bi, i, 0, 0))
    ssm_w = wu.shape[2]
    return pl.pallas_call(
        _inproj_kernel,
        out_shape=(jax.ShapeDtypeStruct((b, ATTN_W, s), MXU_DTYPE),
                   jax.ShapeDtypeStruct((b, IDX_HEADS * IDX_DIM, s), MXU_DTYPE),
                   jax.ShapeDtypeStruct((b, s, LANES), MXU_DTYPE),
                   jax.ShapeDtypeStruct((b, n_t, VT_ROWS, tm), MXU_DTYPE),
                   jax.ShapeDtypeStruct((b, 8, s), F32),
                   jax.ShapeDtypeStruct((b, s, ssm_w), MXU_DTYPE),
                   jax.ShapeDtypeStruct((b, n_t, 1, 1), F32),
                   jax.ShapeDtypeStruct((b, n_t, 1, 1), F32)),
        grid=(b, n_t),
        in_specs=[row(d),
                  pl.BlockSpec((None, N_MOD, d), lambda bi, i: (bi, 0, 0)),
                  pl.BlockSpec((1, d), lambda bi, i: (0, 0)),
                  wres(ATTN_W, d), wres(IDX_HEADS * IDX_DIM, d), wres(d, LANES), wres(VT_ROWS, d), wres(8, d),
                  wres(d, ssm_w),
                  row(LANES), row(LANES), row(LANES), col(ROT_DIM // 2), col(ROT_DIM // 2)],
        out_specs=(col(ATTN_W), col(IDX_HEADS * IDX_DIM), row(LANES),
                   pl.BlockSpec((None, None, VT_ROWS, tm), lambda bi, i: (bi, i, 0, 0)),
                   col(8), row(ssm_w), one, one),
        compiler_params=_params("parallel", "parallel"),
        name="mixer_inproj",
    )(x, mod_l, g.reshape(1, d), wqt, wqit, wkk, wvt, wwt, wu, *tables)


def _rope_tables(positions):
    inv_freq = 1.0 / (ROPE_THETA ** (jnp.arange(0, ROT_DIM, 2, dtype=F32) / ROT_DIM))
    ang = positions.astype(F32)[..., None] * inv_freq
    cos, sin = jnp.cos(ang), jnp.sin(ang)
    half = ROT_DIM // 2
    rest = HEAD_DIM - ROT_DIM
    pad = lambda *parts: jnp.tile(jnp.concatenate(parts, axis=-1), (1, 1, LANES // HEAD_DIM))
    zeros = lambda n: jnp.zeros(cos.shape[:-1] + (n,), F32)
    ones = jnp.ones(cos.shape[:-1] + (rest,), F32)
    return (pad(cos, cos, ones),
            pad(-sin, zeros(half), zeros(rest)),
            pad(zeros(half), sin, zeros(rest)),
            cos.transpose(0, 2, 1), sin.transpose(0, 2, 1))


def _dsa_kernel(qt_ref, qit_ref, wt_ref, qn_ref, kn_ref, kk_ref, vt_ref, o_ref,
                key_ref, bias_ref, acc_ref, *, topk):
    tq = qt_ref.shape[1]
    tk = kk_ref.shape[1]
    i = pl.program_id(1)
    n_tiles = ((i + 1) * tq + tk - 1) // tk
    col = lax.broadcasted_iota(jnp.int32, (1, tq), 1) + i * tq
    q_lim = (col // CHUNK + 1) * CHUNK
    key_pos = lax.broadcasted_iota(jnp.int32, (tk, tq), 0)
    zeros_half = jnp.zeros((HEAD_DIM, tq), MXU_DTYPE)

    qit = qit_ref[...]
    wqi = jnp.concatenate([jnp.concatenate([zeros_half, qit[h * IDX_DIM:(h + 1) * IDX_DIM, :]], axis=0)
                           for h in range(IDX_HEADS)], axis=1)
    w = wt_ref[...] * (IDX_DIM ** -0.5 * IDX_HEADS ** -0.5)
    w_row = jnp.concatenate([w[h:h + 1, :] for h in range(IDX_HEADS)], axis=1)

    def write_keys(t, masked):
        rel = jnp.maximum(_dot(kk_ref[t], wqi), 0.0) * w_row
        s = rel[:, :tq]
        for h in range(1, IDX_HEADS):
            s = s + rel[:, h * tq:(h + 1) * tq]
        s = s + 0.0
        bits = pltpu.bitcast(s, jnp.int32)
        key = jnp.where(bits < 0, bits ^ jnp.int32(0x7FFFFFFF), bits)
        if masked:
            key = jnp.where(key_pos + t * tk < q_lim, key, jnp.int32(INT_MIN))
        key_ref[t] = key

    def score_body(t, carry):
        write_keys(t, False)
        return carry

    lax.fori_loop(0, n_tiles - 1, score_body, 0)
    write_keys(n_tiles - 1, True)

    def count(thr, strict):
        thr_b = jnp.broadcast_to(thr, (8, tq))

        def body(t, accs):
            accs = list(accs)
            for r in range(0, tk, 8):
                key = key_ref[t, r:r + 8, :]
                hit = jnp.where(key > thr_b if strict else key >= thr_b, 1.0, 0.0)
                accs[(r // 8) % COUNT_CHAINS] = accs[(r // 8) % COUNT_CHAINS] + hit
            return tuple(accs)

        accs = lax.fori_loop(0, n_tiles, body, tuple(jnp.zeros((8, tq), F32) for _ in range(COUNT_CHAINS)))
        total = accs[0]
        for a in accs[1:]:
            total = total + a
        return jnp.sum(total, axis=0, keepdims=True)

    k_f = float(topk)
    n0 = count(jnp.zeros((1, tq), jnp.int32), False)
    nonneg = n0 >= k_f
    tau0 = jnp.where(nonneg, jnp.int32(0), jnp.int32(INT_MIN))
    cnt0 = jnp.where(nonneg, n0, (n_tiles * tk).astype(F32))

    def refine(it, tau, cnt):
        trial = tau | lax.shift_left(jnp.int32(1), 30 - it)
        c = count(trial, False)
        ok = c >= k_f
        return jnp.where(ok, trial, tau), jnp.where(ok, c, cnt)

    tau, cnt = lax.fori_loop(0, TAU_PROBE_BIT, lambda it, tc: refine(it, *tc), (tau0, cnt0))
    locked = count(tau + 1, False) < k_f

    def unsettled(tau_cnt):
        return jnp.max(jnp.where(locked | (tau_cnt[1] == k_f), 0.0, 1.0))

    def tail_cond(carry):
        it, _, _, open_rows = carry
        return jnp.logical_and(it < 31, open_rows > 0.0)

    def tail_body(carry):
        it, tau, cnt, _ = carry
        for step in range(TAU_TAIL_STEPS):
            tau, cnt = refine(it + step, tau, cnt)
        return it + TAU_TAIL_STEPS, tau, cnt, unsettled((tau, cnt))

    _, tau, _, _ = lax.while_loop(tail_cond, tail_body,
                                  (jnp.int32(TAU_PROBE_BIT), tau, cnt, unsettled((tau, cnt))))
    need = k_f - count(tau, True)

    lower = (lax.broadcasted_iota(jnp.int32, (tk, tk), 1)
             <= lax.broadcasted_iota(jnp.int32, (tk, tk), 0)).astype(MXU_DTYPE)

    def bias_body(t, ties_before):
        key = key_ref[t]
        tie = jnp.where(key == tau, 1.0, 0.0)
        rank = ties_before + _dot(lower, tie.astype(MXU_DTYPE))
        sel = (key > tau) | ((key == tau) & (rank <= need))
        sel = sel & (key > jnp.int32(INT_MIN))
        bias_ref[t] = jnp.where(sel, 0.0, NEG_BIAS)
        return rank[tk - 1:tk, :]

    lax.fori_loop(0, n_tiles, bias_body, jnp.zeros((1, tq), F32))

    qt = qt_ref[...]
    wq = jnp.concatenate([jnp.concatenate([qt[h * HEAD_DIM:(h + 1) * HEAD_DIM, :], zeros_half], axis=0)
                          for h in range(N_HEADS)], axis=1)

    def logits(t):
        return _dot(kk_ref[t], wq) + jnp.concatenate([bias_ref[t]] * N_HEADS, axis=1)

    safe = jnp.max(qn_ref[...] * kn_ref[...]) <= LOGIT_SAFE * LOGIT_SAFE

    def exact_max():
        def max_body(t, m):
            s = logits(t)
            part = s[:8, :]
            for r in range(8, tk, 8):
                part = jnp.maximum(part, s[r:r + 8, :])
            return jnp.maximum(m, part)
        m = lax.fori_loop(0, n_tiles, max_body, jnp.full((8, N_HEADS * tq), NEG_BIAS, F32))
        return jnp.max(m, axis=0, keepdims=True)

    offset = lax.cond(safe, lambda: jnp.zeros((1, N_HEADS * tq), F32), exact_max)
    acc_ref[...] = jnp.zeros(acc_ref.shape, F32)

    def att_body(t, carry):
        p = jnp.exp(logits(t) - offset)
        acc_ref[...] += _dot(vt_ref[t], p.astype(MXU_DTYPE))
        return carry

    lax.fori_loop(0, n_tiles, att_body, 0)
    acc = acc_ref[...]
    o_ref[...] = (acc[:HEAD_DIM, :] / acc[HEAD_DIM:HEAD_DIM + 1, :]).astype(o_ref.dtype)


def _dsa_call(qt, qit, wt, qn2, kn2, kk, vt):
    b, _, s = qt.shape
    tq = min(Q_TILE, s)
    tk = vt.shape[3]
    n_kt = s // tk
    n_q = s // tq
    topk = min(TOPK_MAX, s // 4)
    assert tk % tq == 0 and s % (tq * qn2.shape[1]) == 0
    q_per_norm_tile = s // qn2.shape[1] // tq
    kk_t = kk.reshape(b, n_kt, tk, LANES)
    kn2 = jnp.max(kn2, axis=1)
    per_batch = lambda shape: pl.BlockSpec((None,) + shape, lambda bi, i: (bi, 0, 0, 0))
    qcol = lambda height: pl.BlockSpec((None, height, tq), lambda bi, i: (bi, 0, i))
    out_t = pl.pallas_call(
        functools.partial(_dsa_kernel, topk=topk),
        out_shape=jax.ShapeDtypeStruct((b, n_q, HEAD_DIM, N_HEADS * tq), MXU_DTYPE),
        grid=(b, n_q),
        in_specs=[qcol(ATTN_W), qcol(IDX_HEADS * IDX_DIM), qcol(8),
                  pl.BlockSpec((None, None, 1, 1), lambda bi, i: (bi, i // q_per_norm_tile, 0, 0)),
                  pl.BlockSpec((None, 1, 1), lambda bi, i: (bi, 0, 0)),
                  per_batch((n_kt, tk, LANES)), per_batch((n_kt, VT_ROWS, tk))],
        out_specs=pl.BlockSpec((None, None, HEAD_DIM, N_HEADS * tq), lambda bi, i: (bi, i, 0, 0)),
        scratch_shapes=[pltpu.VMEM((n_kt, tk, tq), jnp.int32),
                        pltpu.VMEM((n_kt, tk, tq), F32),
                        pltpu.VMEM((VT_ROWS, N_HEADS * tq), F32)],
        compiler_params=_params("parallel", "parallel"),
        name="dsa_attention",
    )(qt, qit, wt, qn2, kn2, kk_t, vt)
    out = out_t.reshape(b, n_q, HEAD_DIM, N_HEADS, tq).transpose(0, 1, 4, 3, 2)
    return out.reshape(b, s, ATTN_W)


def _ssm_kernel(u_ref, toep_ref, win_ref, wout_ref, apow_ref, y_ref, hloc_ref, hswap_ref, hprev_ref, *, batch):
    rows, width = u_ref.shape
    sub = toep_ref.shape[0]
    n_sub = width // sub
    rot = lambda h, r: apow_ref[r:r + 1, :] * h + apow_ref[r + 1:r + 2, :] * pltpu.roll(h, SSM_STATE, 1)
    h = jnp.zeros((rows, 2 * SSM_STATE), F32)
    local = []
    for j in range(n_sub):
        h = rot(h, 0) + _dot(u_ref[:, j * sub:(j + 1) * sub], win_ref[...])
        local.append(h.astype(MXU_DTYPE))
    hloc_ref[...] = h
    hswap_ref[...] = pltpu.roll(h, SSM_STATE, 1)
    a_same, a_swap = apow_ref[2:3, :], apow_ref[3:4, :]
    a_swap_rolled = pltpu.roll(a_swap, SSM_STATE, 1)
    h = jnp.zeros((batch, 2 * SSM_STATE), F32)
    h_sw = jnp.zeros((batch, 2 * SSM_STATE), F32)
    for c in range(rows // batch):
        hprev_ref[c * batch:(c + 1) * batch, :] = h
        h, h_sw = (a_same * h + a_swap * h_sw + hloc_ref[c * batch:(c + 1) * batch, :],
                   a_same * h_sw + a_swap_rolled * h + hswap_ref[c * batch:(c + 1) * batch, :])
    hprev = hprev_ref[...].astype(MXU_DTYPE)
    for j in range(n_sub):
        acc = _dot(hprev, wout_ref[:, j * sub:(j + 1) * sub]) + _dot(u_ref[:, j * sub:(j + 1) * sub], toep_ref[...])
        if j > 0:
            acc = acc + _dot(local[j - 1], wout_ref[:, :sub])
        y_ref[:, j * sub:(j + 1) * sub] = acc.astype(y_ref.dtype)


def _ssm_call(u_g, toep, win, wout, apow, layer, batch):
    g, rows, width = u_g.shape
    sub = toep.shape[2]
    per_group = lambda *shape: pl.BlockSpec((None, None) + shape, lambda gi: (layer, gi) + (0,) * len(shape))
    return pl.pallas_call(
        functools.partial(_ssm_kernel, batch=batch),
        out_shape=jax.ShapeDtypeStruct((g, rows, width), MXU_DTYPE),
        grid=(g,),
        in_specs=[pl.BlockSpec((None, rows, width), lambda gi: (gi, 0, 0)),
                  per_group(sub, sub), per_group(sub, 2 * SSM_STATE),
                  per_group(2 * SSM_STATE, width), per_group(4, 2 * SSM_STATE)],
        out_specs=pl.BlockSpec((None, rows, width), lambda gi: (gi, 0, 0)),
        scratch_shapes=[pltpu.VMEM((rows, 2 * SSM_STATE), F32)] * 3,
        compiler_params=_params("parallel"),
        name="s5_scan",
    )(u_g, toep, win, wout, apow)


def _ssm_operators(a_re, a_im, log_dt, b_re, b_im, c_re, c_im, d_skip, chunk):
    cmul = lambda xr, xi, yr, yi: (xr * yr - xi * yi, xr * yi + xi * yr)
    dt = jnp.exp(log_dt)[..., None]
    lam_re, lam_im = a_re * dt, a_im * dt
    lags = jnp.arange(chunk + 1, dtype=F32)[None, None, :, None]
    mag = jnp.exp(lam_re[:, :, None, :] * lags)
    ang = lam_im[:, :, None, :] * lags
    ap_re, ap_im = mag * jnp.cos(ang), mag * jnp.sin(ang)
    num_re, num_im = ap_re[:, :, 1] - 1.0, ap_im[:, :, 1]
    den = a_re * a_re + a_im * a_im
    coef_re, coef_im = (num_re * a_re + num_im * a_im) / den, (num_im * a_re - num_re * a_im) / den
    bb_re, bb_im = cmul(coef_re[..., None], coef_im[..., None], b_re, b_im)
    depth, g, p = a_re.shape
    gc = b_re.shape[-1]
    sub = SSM_SUB
    ca_re, ca_im = cmul(c_re[:, :, None], c_im[:, :, None], ap_re[:, :, :sub, None, :], ap_im[:, :, :sub, None, :])
    taps = (jnp.einsum('dgtop,dgpi->dgtoi', ca_re, bb_re, precision=lax.Precision.HIGHEST)
            - jnp.einsum('dgtop,dgpi->dgtoi', ca_im, bb_im, precision=lax.Precision.HIGHEST))
    taps = taps.at[:, :, 0].add(d_skip[..., None] * jnp.eye(gc, dtype=F32))
    taps = jnp.concatenate([jnp.zeros((depth, g, sub - 1, gc, gc), F32), taps], axis=2)
    lag_idx = jnp.arange(sub)[None, :] - jnp.arange(sub)[:, None] + (sub - 1)
    toep = taps[:, :, lag_idx]
    toep = toep.transpose(0, 1, 2, 5, 3, 4).reshape(depth, g, sub * gc, sub * gc)
    rev_re, rev_im = ap_re[:, :, sub - 1::-1][:, :, :sub], ap_im[:, :, sub - 1::-1][:, :, :sub]
    wi_re, wi_im = cmul(rev_re[..., None], rev_im[..., None], bb_re[:, :, None], bb_im[:, :, None])
    w_in = jnp.concatenate([wi_re, wi_im], axis=3)
    w_in = w_in.transpose(0, 1, 2, 4, 3).reshape(depth, g, sub * gc, 2 * p)
    wo_re, wo_im = cmul(c_re[:, :, None], c_im[:, :, None], ap_re[:, :, 1:, None, :], ap_im[:, :, 1:, None, :])
    w_out = jnp.concatenate([wo_re, -wo_im], axis=4)
    w_out = w_out.transpose(0, 1, 4, 2, 3).reshape(depth, g, 2 * p, chunk * gc)
    rot_rows = lambda lag: [jnp.concatenate([ap_re[:, :, lag], ap_re[:, :, lag]], -1),
                            jnp.concatenate([-ap_im[:, :, lag], ap_im[:, :, lag]], -1)]
    apow = jnp.stack(rot_rows(sub) + rot_rows(chunk), axis=2)
    return toep.astype(MXU_DTYPE), w_in.astype(MXU_DTYPE), w_out.astype(MXU_DTYPE), apow


def _outproj_kernel(x_ref, attn_ref, y_ref, mod_ref, ag_ref, sg_ref, gw_ref, gb_ref, woa_ref, wos_ref, o_ref):
    a = _rms(attn_ref[...].astype(F32), ag_ref[...]).astype(MXU_DTYPE)
    y = y_ref[...].astype(F32)
    y = 0.5 * y * (1.0 + jnp.tanh(math.sqrt(2.0 / math.pi) * (y + 0.044715 * (y * y * y))))
    y = y * jax.nn.sigmoid(_dot(y.astype(MXU_DTYPE), gw_ref[...]) + gb_ref[...])
    y = _rms(y, sg_ref[...]).astype(MXU_DTYPE)
    mixed = _dot(a, woa_ref[...]) + _dot(y, wos_ref[...])
    o_ref[...] = x_ref[...] + mod_ref[5:6, :] * mixed


def _outproj_call(x, attn, y, mod_l, attn_gain, ssm_gain, glu_w, glu_b, wo_a, wo_s, layer):
    b, s, d = x.shape
    tm = min(ROW_TILE, s)
    aw, sw = attn.shape[2], y.shape[2]
    row = lambda w: pl.BlockSpec((None, tm, w), lambda bi, i: (bi, i, 0))
    vec = lambda w: pl.BlockSpec((None, 1, w), lambda bi, i: (layer, 0, 0))
    wres = lambda r, c: _resident((None, r, c), lambda bi, i: (layer, 0, 0))
    return pl.pallas_call(
        _outproj_kernel,
        out_shape=jax.ShapeDtypeStruct(x.shape, F32),
        grid=(b, s // tm),
        in_specs=[row(d), row(aw), row(sw),
                  pl.BlockSpec((None, N_MOD, d), lambda bi, i: (bi, 0, 0)),
                  vec(aw), vec(sw), wres(sw, sw), vec(sw), wres(aw, d), wres(sw, d)],
        out_specs=row(d),
        compiler_params=_params("parallel", "parallel"),
        name="mixer_outproj",
    )(x, attn, y, mod_l, attn_gain[:, None, :], ssm_gain[:, None, :], glu_w, glu_b[:, None, :], wo_a, wo_s)


def kernel(x, c, positions, mod_w, mod_b, norm_g, ffn1_w1, ffn1_w3, ffn1_w2, ffn2_w1, ffn2_w3, ffn2_w2, w_in, w_out, attn_gain, ssm_gain, ssm_a_re, ssm_a_im, ssm_log_dt, ssm_b_re, ssm_b_im, ssm_c_re, ssm_c_im, ssm_d, glu_w, glu_b, final_g):
    b, s, d = x.shape
    depth = mod_w.shape[0]
    ssm_w = d - ATTN_W
    groups = ssm_w // SSM_GROUP
    chunk = min(SCAN_L, s)
    n_chunks = s // chunk
    cast = lambda a: a.astype(MXU_DTYPE)

    mod = _mod_call(c, mod_w, mod_b)
    tables = _rope_tables(positions)
    ffn1 = (cast(ffn1_w1), cast(ffn1_w3), cast(ffn1_w2))
    ffn2 = (cast(ffn2_w1), cast(ffn2_w3), cast(ffn2_w2))
    o_k = ATTN_W
    o_v = o_k + HEAD_DIM
    o_qi = o_v + HEAD_DIM
    o_ki = o_qi + IDX_HEADS * IDX_DIM
    o_wi = o_ki + IDX_DIM
    o_u = o_wi + IDX_HEADS
    w_in_t = w_in.transpose(0, 2, 1)
    pad_rows = lambda a, rows: jnp.concatenate([a, jnp.zeros((depth, rows - a.shape[1], d), a.dtype)], axis=1)
    wqt = cast(w_in_t[:, :o_k])
    wqit = cast(w_in_t[:, o_qi:o_ki])
    wkk = cast(jnp.concatenate([w_in[:, :, o_k:o_v], w_in[:, :, o_ki:o_wi]], axis=2))
    wvt = cast(pad_rows(w_in_t[:, o_v:o_qi], VT_ROWS))
    wwt = cast(pad_rows(w_in_t[:, o_wi:o_u], 8))
    wu = cast(w_in[:, :, o_u:])
    wo_a, wo_s = cast(w_out[:, :ATTN_W]), cast(w_out[:, ATTN_W:])
    glu_wc = cast(glu_w)
    toep, s_win, s_wout, apow = _ssm_operators(ssm_a_re, ssm_a_im, ssm_log_dt, ssm_b_re, ssm_b_im,
                                               ssm_c_re, ssm_c_im, ssm_d, chunk)

    for l in range(depth):
        x = _ffn_call(x, mod[l], norm_g[l, 0], *ffn1, l, 0)
        qt, qit, kk, vt, wt, u, qn2, kn2 = _inproj_call(x, mod[l], norm_g[l, 1], wqt, wqit, wkk, wvt, wwt, wu,
                                                        tables, l)
        attn = _dsa_call(qt, qit, wt, qn2, kn2, kk, vt)
        u_g = u.reshape(b, n_chunks, chunk, groups, SSM_GROUP).transpose(3, 1, 0, 2, 4)
        u_g = u_g.reshape(groups, n_chunks * b, chunk * SSM_GROUP)
        y_g = _ssm_call(u_g, toep, s_win, s_wout, apow, l, b)
        y = y_g.reshape(groups, n_chunks, b, chunk, SSM_GROUP).transpose(2, 1, 3, 0, 4).reshape(b, s, ssm_w)
        x = _outproj_call(x, attn, y, mod[l], attn_gain, ssm_gain, glu_wc, glu_b, wo_a, wo_s, l)
        x = _ffn_call(x, mod[l], norm_g[l, 2], *ffn2, l, 6, final_g=final_g if l == depth - 1 else None)
    return x
```

```python
import functools
import math

import jax
import jax.numpy as jnp
from jax import lax
from jax.experimental import pallas as pl
from jax.experimental.pallas import tpu as pltpu

F32 = jnp.float32
MXU_DTYPE = jnp.bfloat16

CHUNK = 64
N_HEADS = 8
HEAD_DIM = 64
ATTN_W = N_HEADS * HEAD_DIM
IDX_HEADS = 4
IDX_DIM = 64
TOPK_MAX = 256
SSM_GROUP = 16
SSM_STATE = 64
SCAN_L = 128
ROPE_THETA = 500000.0
ROT_DIM = HEAD_DIM // 4
EPS = 1e-6
N_MOD = 9

LANES = 128
MXU_WIDTH = 256
BF16_SUBLANES = 16
VT_ROWS = -(-(HEAD_DIM + 1) // BF16_SUBLANES) * BF16_SUBLANES
VMEM_LIMIT_BYTES = 56 * 1024 * 1024

INT_MIN = -(2 ** 31)
NEG_BIAS = -1e30

ROW_TILE = 512
Q_TILE = 256
LOGIT_SAFE = 40.0
COUNT_CHAINS = 4
TAU_PROBE_BIT = 23
TAU_TAIL_STEPS = 2
assert (31 - TAU_PROBE_BIT) % TAU_TAIL_STEPS == 0
SSM_SUB = 16


def _params(*sem):
    return pltpu.CompilerParams(dimension_semantics=sem, vmem_limit_bytes=VMEM_LIMIT_BYTES)


def _dot(a, b):
    return jnp.dot(a, b, preferred_element_type=F32)


def _rms(x, g):
    return x * lax.rsqrt(jnp.mean(x * x, axis=-1, keepdims=True) + EPS) * g


def _modulated_norm(x, g, mod_ref, base):
    shift = mod_ref[base:base + 1, :]
    scale = mod_ref[base + 1:base + 2, :]
    return _rms(x, g) * (1.0 + scale) + shift


def _mod_kernel(c_ref, w_ref, b_ref, o_ref):
    c = c_ref[...]
    o_ref[...] = _dot(c * jax.nn.sigmoid(c), w_ref[...]) + b_ref[...]


def _mod_call(c, mod_w, mod_b):
    depth, d, nd = mod_w.shape
    b = c.shape[0]
    rows = -(-b // 8) * 8
    c_pad = jnp.zeros((rows, d), F32).at[:b].set(c)
    out = pl.pallas_call(
        _mod_kernel,
        out_shape=jax.ShapeDtypeStruct((depth, rows, nd), F32),
        grid=(depth, nd // d),
        in_specs=[pl.BlockSpec((rows, d), lambda l, j: (0, 0)),
                  pl.BlockSpec((None, d, d), lambda l, j: (l, 0, j)),
                  pl.BlockSpec((None, 1, d), lambda l, j: (l, 0, j))],
        out_specs=pl.BlockSpec((None, rows, d), lambda l, j: (l, 0, j)),
        compiler_params=_params("parallel", "parallel"),
        name="adaln_mod",
    )(c_pad, mod_w, mod_b.reshape(depth, 1, nd))
    return out[:, :b].reshape(depth, b, N_MOD, d)


def _ffn_kernel(x_ref, mod_ref, g_ref, w1_ref, w3_ref, w2_ref, o_ref, *, base, ff_cuts, final_norm):
    x = x_ref[...]
    h = _modulated_norm(x, g_ref[0:1, :], mod_ref, base).astype(MXU_DTYPE)
    acc = jnp.zeros(x.shape, F32)
    for c0, c1 in zip(ff_cuts[:-1], ff_cuts[1:]):
        a = _dot(h, w1_ref[:, c0:c1])
        b = _dot(h, w3_ref[:, c0:c1])
        t = (a * jax.nn.sigmoid(a) * b).astype(MXU_DTYPE)
        acc = acc + _dot(t, w2_ref[c0:c1, :])
    y = x + 0.5 * mod_ref[base + 2:base + 3, :] * acc
    o_ref[...] = _rms(y, g_ref[1:2, :]) if final_norm else y


def _resident(shape, index_map):
    return pl.BlockSpec(shape, index_map, pipeline_mode=pl.Buffered(1))


def _ffn_call(x, mod_l, g, w1, w3, w2, layer, base, final_g=None):
    b, s, d = x.shape
    gains = jnp.stack([g, g if final_g is None else final_g])
    ff = w1.shape[2]
    tm = min(ROW_TILE, s)
    mid = -(-ff // (2 * MXU_WIDTH)) * MXU_WIDTH
    ff_cuts = (0, mid, ff) if 0 < mid < ff else (0, ff)
    wspec_in = _resident((None, d, ff), lambda bi, i: (layer, 0, 0))
    wspec_out = _resident((None, ff, d), lambda bi, i: (layer, 0, 0))
    return pl.pallas_call(
        functools.partial(_ffn_kernel, base=base, ff_cuts=ff_cuts, final_norm=final_g is not None),
        out_shape=jax.ShapeDtypeStruct(x.shape, F32),
        grid=(b, s // tm),
        in_specs=[pl.BlockSpec((None, tm, d), lambda bi, i: (bi, i, 0)),
                  pl.BlockSpec((None, N_MOD, d), lambda bi, i: (bi, 0, 0)),
                  pl.BlockSpec((2, d), lambda bi, i: (0, 0)),
                  wspec_in, wspec_in, wspec_out],
        out_specs=pl.BlockSpec((None, tm, d), lambda bi, i: (bi, i, 0)),
        compiler_params=_params("parallel", "parallel"),
        name="ffn",
    )(x, mod_l, gains, w1, w3, w2)


def _rope128(t, cos, s_lo, s_hi):
    half = ROT_DIM // 2
    return t * cos + pltpu.roll(t, LANES - half, 1) * s_lo + pltpu.roll(t, half, 1) * s_hi


def _dot_nt(a, b):
    return lax.dot_general(a, b, (((1,), (1,)), ((), ())), preferred_element_type=F32)


def _rope_rows(t, cos_t, sin_t):
    half = ROT_DIM // 2
    parts = []
    for base in range(0, t.shape[0], HEAD_DIM):
        t1, t2 = t[base:base + half], t[base + half:base + ROT_DIM]
        parts += [t1 * cos_t - t2 * sin_t, t2 * cos_t + t1 * sin_t, t[base + ROT_DIM:base + HEAD_DIM]]
    return jnp.concatenate(parts, axis=0)


def _inproj_kernel(x_ref, mod_ref, g_ref, wqt_ref, wqit_ref, wkk_ref, wvt_ref, wwt_ref, wu_ref,
                   cos_ref, slo_ref, shi_ref, cost_ref, sint_ref,
                   qt_ref, qit_ref, kk_ref, vt_ref, wt_ref, u_ref, qn_ref, kn_ref):
    h = _modulated_norm(x_ref[...], g_ref[...], mod_ref, 3).astype(MXU_DTYPE)
    cos_t, sin_t = cost_ref[...], sint_ref[...]
    qt = (_rope_rows(_dot_nt(wqt_ref[...], h), cos_t, sin_t) * HEAD_DIM ** -0.5).astype(qt_ref.dtype)
    qt_ref[...] = qt
    qit_ref[...] = _rope_rows(_dot_nt(wqit_ref[...], h), cos_t, sin_t).astype(qit_ref.dtype)
    kk = _rope128(_dot(h, wkk_ref[...]), cos_ref[...], slo_ref[...], shi_ref[...]).astype(kk_ref.dtype)
    kk_ref[...] = kk
    vt = _dot_nt(wvt_ref[...], h)
    row = lax.broadcasted_iota(jnp.int32, vt.shape, 0)
    vt_ref[...] = jnp.where(row == HEAD_DIM, 1.0, vt).astype(vt_ref.dtype)
    wt_ref[...] = _dot_nt(wwt_ref[...], h)
    u_ref[...] = _dot(h, wu_ref[...]).astype(u_ref.dtype)
    q2 = qt.astype(F32) * qt.astype(F32)
    qn_ref[...] = jnp.max(jnp.concatenate(
        [jnp.sum(q2[b0:b0 + HEAD_DIM], axis=0, keepdims=True) for b0 in range(0, q2.shape[0], HEAD_DIM)], axis=0),
        keepdims=True)
    k2 = kk.astype(F32) * kk.astype(F32)
    lane = lax.broadcasted_iota(jnp.int32, k2.shape, 1)
    kn_ref[...] = jnp.max(jnp.sum(jnp.where(lane < HEAD_DIM, k2, 0.0), axis=1, keepdims=True), keepdims=True)


def _inproj_call(x, mod_l, g, wqt, wqit, wkk, wvt, wwt, wu, tables, layer):
    b, s, d = x.shape
    tm = min(ROW_TILE, s)
    n_t = s // tm
    row = lambda w: pl.BlockSpec((None, tm, w), lambda bi, i: (bi, i, 0))
    col = lambda hgt: pl.BlockSpec((None, hgt, tm), lambda bi, i: (bi, 0, i))
    wres = lambda r, c: _resident((None, r, c), lambda bi, i: (layer, 0, 0))
    one = pl.BlockSpec((None, None, 1, 1), lambda bi, i: (bi, i, 0, 0))
    ssm_w = wu.shape[2]
    return pl.pallas_call(
        _inproj_kernel,
        out_shape=(jax.ShapeDtypeStruct((b, ATTN_W, s), MXU_DTYPE),
                   jax.ShapeDtypeStruct((b, IDX_HEADS * IDX_DIM, s), MXU_DTYPE),
                   jax.ShapeDtypeStruct((b, s, LANES), MXU_DTYPE),
                   jax.ShapeDtypeStruct((b, n_t, VT_ROWS, tm), MXU_DTYPE),
                   jax.ShapeDtypeStruct((b, 8, s), F32),
                   jax.ShapeDtypeStruct((b, s, ssm_w), MXU_DTYPE),
                   jax.ShapeDtypeStruct((b, n_t, 1, 1), F32),
                   jax.ShapeDtypeStruct((b, n_t, 1, 1), F32)),
        grid=(b, n_t),
        in_specs=[row(d),
                  pl.BlockSpec((None, N_MOD, d), lambda bi, i: (bi, 0, 0)),
                  pl.BlockSpec((1, d), lambda bi, i: (0, 0)),
                  wres(ATTN_W, d), wres(IDX_HEADS * IDX_DIM, d), wres(d, LANES), wres(VT_ROWS, d), wres(8, d),
                  wres(d, ssm_w),
                  row(LANES), row(LANES), row(LANES), col(ROT_DIM // 2), col(ROT_DIM // 2)],
        out_specs=(col(ATTN_W), col(IDX_HEADS * IDX_DIM), row(LANES),
                   pl.BlockSpec((None, None, VT_ROWS, tm), lambda bi, i: (bi, i, 0, 0)),
                   col(8), row(ssm_w), one, one),
        compiler_params=_params("parallel", "parallel"),
        name="mixer_inproj",
    )(x, mod_l, g.reshape(1, d), wqt, wqit, wkk, wvt, wwt, wu, *tables)


def _rope_tables(positions):
    inv_freq = 1.0 / (ROPE_THETA ** (jnp.arange(0, ROT_DIM, 2, dtype=F32) / ROT_DIM))
    ang = positions.astype(F32)[..., None] * inv_freq
    cos, sin = jnp.cos(ang), jnp.sin(ang)
    half = ROT_DIM // 2
    rest = HEAD_DIM - ROT_DIM
    pad = lambda *parts: jnp.tile(jnp.concatenate(parts, axis=-1), (1, 1, LANES // HEAD_DIM))
    zeros = lambda n: jnp.zeros(cos.shape[:-1] + (n,), F32)
    ones = jnp.ones(cos.shape[:-1] + (rest,), F32)
    return (pad(cos, cos, ones),
            pad(-sin, zeros(half), zeros(rest)),
            pad(zeros(half), sin, zeros(rest)),
            cos.transpose(0, 2, 1), sin.transpose(0, 2, 1))


def _dsa_kernel(qt_ref, qit_ref, wt_ref, qn_ref, kn_ref, kk_ref, vt_ref, o_ref,
                key_ref, bias_ref, acc_ref, *, topk):
    tq = qt_ref.shape[1]
    tk = kk_ref.shape[1]
    i = pl.program_id(1)
    n_tiles = ((i + 1) * tq + tk - 1) // tk
    col = lax.broadcasted_iota(jnp.int32, (1, tq), 1) + i * tq
    q_lim = (col // CHUNK + 1) * CHUNK
    key_pos = lax.broadcasted_iota(jnp.int32, (tk, tq), 0)
    zeros_half = jnp.zeros((HEAD_DIM, tq), MXU_DTYPE)

    qit = qit_ref[...]
    wqi = jnp.concatenate([jnp.concatenate([zeros_half, qit[h * IDX_DIM:(h + 1) * IDX_DIM, :]], axis=0)
                           for h in range(IDX_HEADS)], axis=1)
    w = wt_ref[...] * (IDX_DIM ** -0.5 * IDX_HEADS ** -0.5)
    w_row = jnp.concatenate([w[h:h + 1, :] for h in range(IDX_HEADS)], axis=1)

    def write_keys(t, masked):
        rel = jnp.maximum(_dot(kk_ref[t], wqi), 0.0) * w_row
        s = rel[:, :tq]
        for h in range(1, IDX_HEADS):
            s = s + rel[:, h * tq:(h + 1) * tq]
        s = s + 0.0
        bits = pltpu.bitcast(s, jnp.int32)
        key = jnp.where(bits < 0, bits ^ jnp.int32(0x7FFFFFFF), bits)
        if masked:
            key = jnp.where(key_pos + t * tk < q_lim, key, jnp.int32(INT_MIN))
        key_ref[t] = key

    def score_body(t, carry):
        write_keys(t, False)
        return carry

    lax.fori_loop(0, n_tiles - 1, score_body, 0)
    write_keys(n_tiles - 1, True)

    def count(thr, strict):
        thr_b = jnp.broadcast_to(thr, (8, tq))

        def body(t, accs):
            accs = list(accs)
            for r in range(0, tk, 8):
                key = key_ref[t, r:r + 8, :]
                hit = jnp.where(key > thr_b if strict else key >= thr_b, 1.0, 0.0)
                accs[(r // 8) % COUNT_CHAINS] = accs[(r // 8) % COUNT_CHAINS] + hit
            return tuple(accs)

        accs = lax.fori_loop(0, n_tiles, body, tuple(jnp.zeros((8, tq), F32) for _ in range(COUNT_CHAINS)))
        total = accs[0]
        for a in accs[1:]:
            total = total + a
        return jnp.sum(total, axis=0, keepdims=True)

    k_f = float(topk)
    n0 = count(jnp.zeros((1, tq), jnp.int32), False)
    nonneg = n0 >= k_f
    tau0 = jnp.where(nonneg, jnp.int32(0), jnp.int32(INT_MIN))
    cnt0 = jnp.where(nonneg, n0, (n_tiles * tk).astype(F32))

    def refine(it, tau, cnt):
        trial = tau | lax.shift_left(jnp.int32(1), 30 - it)
        c = count(trial, False)
        ok = c >= k_f
        return jnp.where(ok, trial, tau), jnp.where(ok, c, cnt)

    tau, cnt = lax.fori_loop(0, TAU_PROBE_BIT, lambda it, tc: refine(it, *tc), (tau0, cnt0))
    locked = count(tau + 1, False) < k_f

    def unsettled(tau_cnt):
        return jnp.max(jnp.where(locked | (tau_cnt[1] == k_f), 0.0, 1.0))

    def tail_cond(carry):
        it, _, _, open_rows = carry
        return jnp.logical_and(it < 31, open_rows > 0.0)

    def tail_body(carry):
        it, tau, cnt, _ = carry
        for step in range(TAU_TAIL_STEPS):
            tau, cnt = refine(it + step, tau, cnt)
        return it + TAU_TAIL_STEPS, tau, cnt, unsettled((tau, cnt))

    _, tau, cnt, _ = lax.while_loop(tail_cond, tail_body,
                                    (jnp.int32(TAU_PROBE_BIT), tau, cnt, unsettled((tau, cnt))))
    above = count(tau, True)
    need = k_f - above
    surplus_ties = jnp.max(jnp.where(cnt - above > need, 1.0, 0.0)) > 0.0

    @pl.when(jnp.logical_not(surplus_ties))
    def _():
        def bias_body(t, carry):
            key = key_ref[t]
            bias_ref[t] = jnp.where((key >= tau) & (key > jnp.int32(INT_MIN)), 0.0, NEG_BIAS)
            return carry

        lax.fori_loop(0, n_tiles, bias_body, 0)

    @pl.when(surplus_ties)
    def _():
        lower = (lax.broadcasted_iota(jnp.int32, (tk, tk), 1)
                 <= lax.broadcasted_iota(jnp.int32, (tk, tk), 0)).astype(MXU_DTYPE)

        def bias_body(t, ties_before):
            key = key_ref[t]
            tie = jnp.where(key == tau, 1.0, 0.0)
            rank = ties_before + _dot(lower, tie.astype(MXU_DTYPE))
            sel = (key > tau) | ((key == tau) & (rank <= need))
            sel = sel & (key > jnp.int32(INT_MIN))
            bias_ref[t] = jnp.where(sel, 0.0, NEG_BIAS)
            return rank[tk - 1:tk, :]

        lax.fori_loop(0, n_tiles, bias_body, jnp.zeros((1, tq), F32))

    qt = qt_ref[...]
    wq = jnp.concatenate([jnp.concatenate([qt[h * HEAD_DIM:(h + 1) * HEAD_DIM, :], zeros_half], axis=0)
                          for h in range(N_HEADS)], axis=1)

    def logits(t):
        return _dot(kk_ref[t], wq) + jnp.concatenate([bias_ref[t]] * N_HEADS, axis=1)

    safe = jnp.max(qn_ref[...] * kn_ref[...]) <= LOGIT_SAFE * LOGIT_SAFE

    def exact_max():
        def max_body(t, m):
            s = logits(t)
            part = s[:8, :]
            for r in range(8, tk, 8):
                part = jnp.maximum(part, s[r:r + 8, :])
            return jnp.maximum(m, part)
        m = lax.fori_loop(0, n_tiles, max_body, jnp.full((8, N_HEADS * tq), NEG_BIAS, F32))
        return jnp.max(m, axis=0, keepdims=True)

    offset = lax.cond(safe, lambda: jnp.zeros((1, N_HEADS * tq), F32), exact_max)
    acc_ref[...] = jnp.zeros(acc_ref.shape, F32)

    def att_body(t, carry):
        p = jnp.exp(logits(t) - offset)
        acc_ref[...] += _dot(vt_ref[t], p.astype(MXU_DTYPE))
        return carry

    lax.fori_loop(0, n_tiles, att_body, 0)
    acc = acc_ref[...]
    o_ref[...] = (acc[:HEAD_DIM, :] / acc[HEAD_DIM:HEAD_DIM + 1, :]).astype(o_ref.dtype)


def _dsa_call(qt, qit, wt, qn2, kn2, kk, vt):
    b, _, s = qt.shape
    tq = min(Q_TILE, s)
    tk = vt.shape[3]
    n_kt = s // tk
    n_q = s // tq
    topk = min(TOPK_MAX, s // 4)
    assert tk % tq == 0 and s % (tq * qn2.shape[1]) == 0
    q_per_norm_tile = s // qn2.shape[1] // tq
    kk_t = kk.reshape(b, n_kt, tk, LANES)
    kn2 = jnp.max(kn2, axis=1)
    per_batch = lambda shape: pl.BlockSpec((None,) + shape, lambda bi, i: (bi, 0, 0, 0))
    qcol = lambda height: pl.BlockSpec((None, height, tq), lambda bi, i: (bi, 0, i))
    out_t = pl.pallas_call(
        functools.partial(_dsa_kernel, topk=topk),
        out_shape=jax.ShapeDtypeStruct((b, n_q, HEAD_DIM, N_HEADS * tq), MXU_DTYPE),
        grid=(b, n_q),
        in_specs=[qcol(ATTN_W), qcol(IDX_HEADS * IDX_DIM), qcol(8),
                  pl.BlockSpec((None, None, 1, 1), lambda bi, i: (bi, i // q_per_norm_tile, 0, 0)),
                  pl.BlockSpec((None, 1, 1), lambda bi, i: (bi, 0, 0)),
                  per_batch((n_kt, tk, LANES)), per_batch((n_kt, VT_ROWS, tk))],
        out_specs=pl.BlockSpec((None, None, HEAD_DIM, N_HEADS * tq), lambda bi, i: (bi, i, 0, 0)),
        scratch_shapes=[pltpu.VMEM((n_kt, tk, tq), jnp.int32),
                        pltpu.VMEM((n_kt, tk, tq), F32),
                        pltpu.VMEM((VT_ROWS, N_HEADS * tq), F32)],
        compiler_params=_params("parallel", "parallel"),
        name="dsa_attention",
    )(qt, qit, wt, qn2, kn2, kk_t, vt)
    out = out_t.reshape(b, n_q, HEAD_DIM, N_HEADS, tq).transpose(0, 1, 4, 3, 2)
    return out.reshape(b, s, ATTN_W)


def _ssm_kernel(u_ref, toep_ref, win_ref, wout_ref, apow_ref, y_ref, hloc_ref, hswap_ref, hprev_ref, *, batch):
    rows, width = u_ref.shape
    sub = toep_ref.shape[0]
    n_sub = width // sub
    rot = lambda h, r: apow_ref[r:r + 1, :] * h + apow_ref[r + 1:r + 2, :] * pltpu.roll(h, SSM_STATE, 1)
    h = jnp.zeros((rows, 2 * SSM_STATE), F32)
    local = []
    for j in range(n_sub):
        h = rot(h, 0) + _dot(u_ref[:, j * sub:(j + 1) * sub], win_ref[...])
        local.append(h.astype(MXU_DTYPE))
    hloc_ref[...] = h
    hswap_ref[...] = pltpu.roll(h, SSM_STATE, 1)
    a_same, a_swap = apow_ref[2:3, :], apow_ref[3:4, :]
    a_swap_rolled = pltpu.roll(a_swap, SSM_STATE, 1)
    h = jnp.zeros((batch, 2 * SSM_STATE), F32)
    h_sw = jnp.zeros((batch, 2 * SSM_STATE), F32)
    for c in range(rows // batch):
        hprev_ref[c * batch:(c + 1) * batch, :] = h
        h, h_sw = (a_same * h + a_swap * h_sw + hloc_ref[c * batch:(c + 1) * batch, :],
                   a_same * h_sw + a_swap_rolled * h + hswap_ref[c * batch:(c + 1) * batch, :])
    hprev = hprev_ref[...].astype(MXU_DTYPE)
    for j in range(n_sub):
        acc = _dot(hprev, wout_ref[:, j * sub:(j + 1) * sub]) + _dot(u_ref[:, j * sub:(j + 1) * sub], toep_ref[...])
        if j > 0:
            acc = acc + _dot(local[j - 1], wout_ref[:, :sub])
        y_ref[:, j * sub:(j + 1) * sub] = acc.astype(y_ref.dtype)


def _ssm_call(u_g, toep, win, wout, apow, layer, batch):
    g, rows, width = u_g.shape
    sub = toep.shape[2]
    per_group = lambda *shape: pl.BlockSpec((None, None) + shape, lambda gi: (layer, gi) + (0,) * len(shape))
    return pl.pallas_call(
        functools.partial(_ssm_kernel, batch=batch),
        out_shape=jax.ShapeDtypeStruct((g, rows, width), MXU_DTYPE),
        grid=(g,),
        in_specs=[pl.BlockSpec((None, rows, width), lambda gi: (gi, 0, 0)),
                  per_group(sub, sub), per_group(sub, 2 * SSM_STATE),
                  per_group(2 * SSM_STATE, width), per_group(4, 2 * SSM_STATE)],
        out_specs=pl.BlockSpec((None, rows, width), lambda gi: (gi, 0, 0)),
        scratch_shapes=[pltpu.VMEM((rows, 2 * SSM_STATE), F32)] * 3,
        compiler_params=_params("parallel"),
        name="s5_scan",
    )(u_g, toep, win, wout, apow)


def _ssm_operators(a_re, a_im, log_dt, b_re, b_im, c_re, c_im, d_skip, chunk):
    cmul = lambda xr, xi, yr, yi: (xr * yr - xi * yi, xr * yi + xi * yr)
    dt = jnp.exp(log_dt)[..., None]
    lam_re, lam_im = a_re * dt, a_im * dt
    lags = jnp.arange(chunk + 1, dtype=F32)[None, None, :, None]
    mag = jnp.exp(lam_re[:, :, None, :] * lags)
    ang = lam_im[:, :, None, :] * lags
    ap_re, ap_im = mag * jnp.cos(ang), mag * jnp.sin(ang)
    num_re, num_im = ap_re[:, :, 1] - 1.0, ap_im[:, :, 1]
    den = a_re * a_re + a_im * a_im
    coef_re, coef_im = (num_re * a_re + num_im * a_im) / den, (num_im * a_re - num_re * a_im) / den
    bb_re, bb_im = cmul(coef_re[..., None], coef_im[..., None], b_re, b_im)
    depth, g, p = a_re.shape
    gc = b_re.shape[-1]
    sub = SSM_SUB
    ca_re, ca_im = cmul(c_re[:, :, None], c_im[:, :, None], ap_re[:, :, :sub, None, :], ap_im[:, :, :sub, None, :])
    taps = (jnp.einsum('dgtop,dgpi->dgtoi', ca_re, bb_re, precision=lax.Precision.HIGHEST)
            - jnp.einsum('dgtop,dgpi->dgtoi', ca_im, bb_im, precision=lax.Precision.HIGHEST))
    taps = taps.at[:, :, 0].add(d_skip[..., None] * jnp.eye(gc, dtype=F32))
    taps = jnp.concatenate([jnp.zeros((depth, g, sub - 1, gc, gc), F32), taps], axis=2)
    lag_idx = jnp.arange(sub)[None, :] - jnp.arange(sub)[:, None] + (sub - 1)
    toep = taps[:, :, lag_idx]
    toep = toep.transpose(0, 1, 2, 5, 3, 4).reshape(depth, g, sub * gc, sub * gc)
    rev_re, rev_im = ap_re[:, :, sub - 1::-1][:, :, :sub], ap_im[:, :, sub - 1::-1][:, :, :sub]
    wi_re, wi_im = cmul(rev_re[..., None], rev_im[..., None], bb_re[:, :, None], bb_im[:, :, None])
    w_in = jnp.concatenate([wi_re, wi_im], axis=3)
    w_in = w_in.transpose(0, 1, 2, 4, 3).reshape(depth, g, sub * gc, 2 * p)
    wo_re, wo_im = cmul(c_re[:, :, None], c_im[:, :, None], ap_re[:, :, 1:, None, :], ap_im[:, :, 1:, None, :])
    w_out = jnp.concatenate([wo_re, -wo_im], axis=4)
    w_out = w_out.transpose(0, 1, 4, 2, 3).reshape(depth, g, 2 * p, chunk * gc)
    rot_rows = lambda lag: [jnp.concatenate([ap_re[:, :, lag], ap_re[:, :, lag]], -1),
                            jnp.concatenate([-ap_im[:, :, lag], ap_im[:, :, lag]], -1)]
    apow = jnp.stack(rot_rows(sub) + rot_rows(chunk), axis=2)
    return toep.astype(MXU_DTYPE), w_in.astype(MXU_DTYPE), w_out.astype(MXU_DTYPE), apow


def _outproj_kernel(x_ref, attn_ref, y_ref, mod_ref, ag_ref, sg_ref, gw_ref, gb_ref, woa_ref, wos_ref, o_ref):
    a = _rms(attn_ref[...].astype(F32), ag_ref[...]).astype(MXU_DTYPE)
    y = y_ref[...].astype(F32)
    y = 0.5 * y * (1.0 + jnp.tanh(math.sqrt(2.0 / math.pi) * (y + 0.044715 * (y * y * y))))
    y = y * jax.nn.sigmoid(_dot(y.astype(MXU_DTYPE), gw_ref[...]) + gb_ref[...])
    y = _rms(y, sg_ref[...]).astype(MXU_DTYPE)
    mixed = _dot(a, woa_ref[...]) + _dot(y, wos_ref[...])
    o_ref[...] = x_ref[...] + mod_ref[5:6, :] * mixed


def _outproj_call(x, attn, y, mod_l, attn_gain, ssm_gain, glu_w, glu_b, wo_a, wo_s, layer):
    b, s, d = x.shape
    tm = min(ROW_TILE, s)
    aw, sw = attn.shape[2], y.shape[2]
    row = lambda w: pl.BlockSpec((None, tm, w), lambda bi, i: (bi, i, 0))
    vec = lambda w: pl.BlockSpec((None, 1, w), lambda bi, i: (layer, 0, 0))
    wres = lambda r, c: _resident((None, r, c), lambda bi, i: (layer, 0, 0))
    return pl.pallas_call(
        _outproj_kernel,
        out_shape=jax.ShapeDtypeStruct(x.shape, F32),
        grid=(b, s // tm),
        in_specs=[row(d), row(aw), row(sw),
                  pl.BlockSpec((None, N_MOD, d), lambda bi, i: (bi, 0, 0)),
                  vec(aw), vec(sw), wres(sw, sw), vec(sw), wres(aw, d), wres(sw, d)],
        out_specs=row(d),
        compiler_params=_params("parallel", "parallel"),
        name="mixer_outproj",
    )(x, attn, y, mod_l, attn_gain[:, None, :], ssm_gain[:, None, :], glu_w, glu_b[:, None, :], wo_a, wo_s)


def kernel(x, c, positions, mod_w, mod_b, norm_g, ffn1_w1, ffn1_w3, ffn1_w2, ffn2_w1, ffn2_w3, ffn2_w2, w_in, w_out, attn_gain, ssm_gain, ssm_a_re, ssm_a_im, ssm_log_dt, ssm_b_re, ssm_b_im, ssm_c_re, ssm_c_im, ssm_d, glu_w, glu_b, final_g):
    b, s, d = x.shape
    depth = mod_w.shape[0]
    ssm_w = d - ATTN_W
    groups = ssm_w // SSM_GROUP
    chunk = min(SCAN_L, s)
    n_chunks = s // chunk
    cast = lambda a: a.astype(MXU_DTYPE)

    mod = _mod_call(c, mod_w, mod_b)
    tables = _rope_tables(positions)
    ffn1 = (cast(ffn1_w1), cast(ffn1_w3), cast(ffn1_w2))
    ffn2 = (cast(ffn2_w1), cast(ffn2_w3), cast(ffn2_w2))
    o_k = ATTN_W
    o_v = o_k + HEAD_DIM
    o_qi = o_v + HEAD_DIM
    o_ki = o_qi + IDX_HEADS * IDX_DIM
    o_wi = o_ki + IDX_DIM
    o_u = o_wi + IDX_HEADS
    w_in_t = w_in.transpose(0, 2, 1)
    pad_rows = lambda a, rows: jnp.concatenate([a, jnp.zeros((depth, rows - a.shape[1], d), a.dtype)], axis=1)
    wqt = cast(w_in_t[:, :o_k])
    wqit = cast(w_in_t[:, o_qi:o_ki])
    wkk = cast(jnp.concatenate([w_in[:, :, o_k:o_v], w_in[:, :, o_ki:o_wi]], axis=2))
    wvt = cast(pad_rows(w_in_t[:, o_v:o_qi], VT_ROWS))
    wwt = cast(pad_rows(w_in_t[:, o_wi:o_u], 8))
    wu = cast(w_in[:, :, o_u:])
    wo_a, wo_s = cast(w_out[:, :ATTN_W]), cast(w_out[:, ATTN_W:])
    glu_wc = cast(glu_w)
    toep, s_win, s_wout, apow = _ssm_operators(ssm_a_re, ssm_a_im, ssm_log_dt, ssm_b_re, ssm_b_im,
                                               ssm_c_re, ssm_c_im, ssm_d, chunk)

    for l in range(depth):
        x = _ffn_call(x, mod[l], norm_g[l, 0], *ffn1, l, 0)
        qt, qit, kk, vt, wt, u, qn2, kn2 = _inproj_call(x, mod[l], norm_g[l, 1], wqt, wqit, wkk, wvt, wwt, wu,
                                                        tables, l)
        attn = _dsa_call(qt, qit, wt, qn2, kn2, kk, vt)
        u_g = u.reshape(b, n_chunks, chunk, groups, SSM_GROUP).transpose(3, 1, 0, 2, 4)
        u_g = u_g.reshape(groups, n_chunks * b, chunk * SSM_GROUP)
        y_g = _ssm_call(u_g, toep, s_win, s_wout, apow, l, b)
        y = y_g.reshape(groups, n_chunks, b, chunk, SSM_GROUP).transpose(2, 1, 3, 0, 4).reshape(b, s, ssm_w)
        x = _outproj_call(x, attn, y, mod[l], attn_gain, ssm_gain, glu_wc, glu_b, wo_a, wo_s, l)
        x = _ffn_call(x, mod[l], norm_g[l, 2], *ffn2, l, 6, final_g=final_g if l == depth - 1 else None)
    return x
```

```python
import functools
import math

import jax
import jax.numpy as jnp
from jax import lax
from jax.experimental import pallas as pl
from jax.experimental.pallas import tpu as pltpu

F32 = jnp.float32
MXU_DTYPE = jnp.bfloat16

CHUNK = 64
N_HEADS = 8
HEAD_DIM = 64
ATTN_W = N_HEADS * HEAD_DIM
IDX_HEADS = 4
IDX_DIM = 64
TOPK_MAX = 256
SSM_GROUP = 16
SSM_STATE = 64
SCAN_L = 128
ROPE_THETA = 500000.0
ROT_DIM = HEAD_DIM // 4
EPS = 1e-6
N_MOD = 9

LANES = 128
MXU_WIDTH = 256
BF16_SUBLANES = 16
VT_ROWS = -(-(HEAD_DIM + 1) // BF16_SUBLANES) * BF16_SUBLANES
VMEM_LIMIT_BYTES = 56 * 1024 * 1024

INT_MIN = -(2 ** 31)
NEG_BIAS = -1e30

ROW_TILE = 512
Q_TILE = 512
LOGIT_SAFE = 40.0
COUNT_CHAINS = 4
TAU_PROBE_BIT = 23
TAU_TAIL_STEPS = 2
assert (31 - TAU_PROBE_BIT) % TAU_TAIL_STEPS == 0
SSM_SUB = 16


def _params(*sem):
    return pltpu.CompilerParams(dimension_semantics=sem, vmem_limit_bytes=VMEM_LIMIT_BYTES)


def _dot(a, b):
    return jnp.dot(a, b, preferred_element_type=F32)


def _rms(x, g):
    return x * lax.rsqrt(jnp.mean(x * x, axis=-1, keepdims=True) + EPS) * g


def _modulated_norm(x, g, mod_ref, base):
    shift = mod_ref[base:base + 1, :]
    scale = mod_ref[base + 1:base + 2, :]
    return _rms(x, g) * (1.0 + scale) + shift


def _mod_kernel(c_ref, w_ref, b_ref, o_ref):
    c = c_ref[...]
    o_ref[...] = _dot(c * jax.nn.sigmoid(c), w_ref[...]) + b_ref[...]


def _mod_call(c, mod_w, mod_b):
    depth, d, nd = mod_w.shape
    b = c.shape[0]
    rows = -(-b // 8) * 8
    c_pad = jnp.zeros((rows, d), F32).at[:b].set(c)
    out = pl.pallas_call(
        _mod_kernel,
        out_shape=jax.ShapeDtypeStruct((depth, rows, nd), F32),
        grid=(depth, nd // d),
        in_specs=[pl.BlockSpec((rows, d), lambda l, j: (0, 0)),
                  pl.BlockSpec((None, d, d), lambda l, j: (l, 0, j)),
                  pl.BlockSpec((None, 1, d), lambda l, j: (l, 0, j))],
        out_specs=pl.BlockSpec((None, rows, d), lambda l, j: (l, 0, j)),
        compiler_params=_params("parallel", "parallel"),
        name="adaln_mod",
    )(c_pad, mod_w, mod_b.reshape(depth, 1, nd))
    return out[:, :b].reshape(depth, b, N_MOD, d)


def _ffn_kernel(x_ref, mod_ref, g_ref, w1_ref, w3_ref, w2_ref, o_ref, *, base, ff_cuts, final_norm):
    x = x_ref[...]
    h = _modulated_norm(x, g_ref[0:1, :], mod_ref, base).astype(MXU_DTYPE)
    acc = jnp.zeros(x.shape, F32)
    for c0, c1 in zip(ff_cuts[:-1], ff_cuts[1:]):
        a = _dot(h, w1_ref[:, c0:c1])
        b = _dot(h, w3_ref[:, c0:c1])
        t = (a * jax.nn.sigmoid(a) * b).astype(MXU_DTYPE)
        acc = acc + _dot(t, w2_ref[c0:c1, :])
    y = x + 0.5 * mod_ref[base + 2:base + 3, :] * acc
    o_ref[...] = _rms(y, g_ref[1:2, :]) if final_norm else y


def _resident(shape, index_map):
    return pl.BlockSpec(shape, index_map, pipeline_mode=pl.Buffered(1))


def _ffn_call(x, mod_l, g, w1, w3, w2, layer, base, final_g=None):
    b, s, d = x.shape
    gains = jnp.stack([g, g if final_g is None else final_g])
    ff = w1.shape[2]
    tm = min(ROW_TILE, s)
    mid = -(-ff // (2 * MXU_WIDTH)) * MXU_WIDTH
    ff_cuts = (0, mid, ff) if 0 < mid < ff else (0, ff)
    wspec_in = _resident((None, d, ff), lambda bi, i: (layer, 0, 0))
    wspec_out = _resident((None, ff, d), lambda bi, i: (layer, 0, 0))
    return pl.pallas_call(
        functools.partial(_ffn_kernel, base=base, ff_cuts=ff_cuts, final_norm=final_g is not None),
        out_shape=jax.ShapeDtypeStruct(x.shape, F32),
        grid=(b, s // tm),
        in_specs=[pl.BlockSpec((None, tm, d), lambda bi, i: (bi, i, 0)),
                  pl.BlockSpec((None, N_MOD, d), lambda bi, i: (bi, 0, 0)),
                  pl.BlockSpec((2, d), lambda bi, i: (0, 0)),
                  wspec_in, wspec_in, wspec_out],
        out_specs=pl.BlockSpec((None, tm, d), lambda bi, i: (bi, i, 0)),
        compiler_params=_params("parallel", "parallel"),
        name="ffn",
    )(x, mod_l, gains, w1, w3, w2)


def _rope128(t, cos, s_lo, s_hi):
    half = ROT_DIM // 2
    return t * cos + pltpu.roll(t, LANES - half, 1) * s_lo + pltpu.roll(t, half, 1) * s_hi


def _dot_nt(a, b):
    return lax.dot_general(a, b, (((1,), (1,)), ((), ())), preferred_element_type=F32)


def _rope_rows(t, cos_t, sin_t):
    half = ROT_DIM // 2
    parts = []
    for base in range(0, t.shape[0], HEAD_DIM):
        t1, t2 = t[base:base + half], t[base + half:base + ROT_DIM]
        parts += [t1 * cos_t - t2 * sin_t, t2 * cos_t + t1 * sin_t, t[base + ROT_DIM:base + HEAD_DIM]]
    return jnp.concatenate(parts, axis=0)


def _inproj_kernel(x_ref, mod_ref, g_ref, wqt_ref, wqit_ref, wkk_ref, wvt_ref, wwt_ref, wu_ref,
                   cos_ref, slo_ref, shi_ref, cost_ref, sint_ref,
                   qt_ref, qit_ref, kk_ref, vt_ref, wt_ref, u_ref, qn_ref, kn_ref):
    h = _modulated_norm(x_ref[...], g_ref[...], mod_ref, 3).astype(MXU_DTYPE)
    cos_t, sin_t = cost_ref[...], sint_ref[...]
    qt = (_rope_rows(_dot_nt(wqt_ref[...], h), cos_t, sin_t) * HEAD_DIM ** -0.5).astype(qt_ref.dtype)
    qt_ref[...] = qt
    qit_ref[...] = _rope_rows(_dot_nt(wqit_ref[...], h), cos_t, sin_t).astype(qit_ref.dtype)
    kk = _rope128(_dot(h, wkk_ref[...]), cos_ref[...], slo_ref[...], shi_ref[...]).astype(kk_ref.dtype)
    kk_ref[...] = kk
    vt = _dot_nt(wvt_ref[...], h)
    row = lax.broadcasted_iota(jnp.int32, vt.shape, 0)
    vt_ref[...] = jnp.where(row == HEAD_DIM, 1.0, vt).astype(vt_ref.dtype)
    wt_ref[...] = _dot_nt(wwt_ref[...], h)
    u_ref[...] = _dot(h, wu_ref[...]).astype(u_ref.dtype)
    q2 = qt.astype(F32) * qt.astype(F32)
    qn_ref[...] = jnp.max(jnp.concatenate(
        [jnp.sum(q2[b0:b0 + HEAD_DIM], axis=0, keepdims=True) for b0 in range(0, q2.shape[0], HEAD_DIM)], axis=0),
        keepdims=True)
    k2 = kk.astype(F32) * kk.astype(F32)
    lane = lax.broadcasted_iota(jnp.int32, k2.shape, 1)
    kn_ref[...] = jnp.max(jnp.sum(jnp.where(lane < HEAD_DIM, k2, 0.0), axis=1, keepdims=True), keepdims=True)


def _inproj_call(x, mod_l, g, wqt, wqit, wkk, wvt, wwt, wu, tables, layer):
    b, s, d = x.shape
    tm = min(ROW_TILE, s)
    n_t = s // tm
    row = lambda w: pl.BlockSpec((None, tm, w), lambda bi, i: (bi, i, 0))
    col = lambda hgt: pl.BlockSpec((None, hgt, tm), lambda bi, i: (bi, 0, i))
    wres = lambda r, c: _resident((None, r, c), lambda bi, i: (layer, 0, 0))
    one = pl.BlockSpec((None, None, 1, 1), lambda bi, i: (bi, i, 0, 0))
    ssm_w = wu.shape[2]
    return pl.pallas_call(
        _inproj_kernel,
        out_shape=(jax.ShapeDtypeStruct((b, ATTN_W, s), MXU_DTYPE),
                   jax.ShapeDtypeStruct((b, IDX_HEADS * IDX_DIM, s), MXU_DTYPE),
                   jax.ShapeDtypeStruct((b, s, LANES), MXU_DTYPE),
                   jax.ShapeDtypeStruct((b, n_t, VT_ROWS, tm), MXU_DTYPE),
                   jax.ShapeDtypeStruct((b, 8, s), F32),
                   jax.ShapeDtypeStruct((b, s, ssm_w), MXU_DTYPE),
                   jax.ShapeDtypeStruct((b, n_t, 1, 1), F32),
                   jax.ShapeDtypeStruct((b, n_t, 1, 1), F32)),
        grid=(b, n_t),
        in_specs=[row(d),
                  pl.BlockSpec((None, N_MOD, d), lambda bi, i: (bi, 0, 0)),
                  pl.BlockSpec((1, d), lambda bi, i: (0, 0)),
                  wres(ATTN_W, d), wres(IDX_HEADS * IDX_DIM, d), wres(d, LANES), wres(VT_ROWS, d), wres(8, d),
                  wres(d, ssm_w),
                  row(LANES), row(LANES), row(LANES), col(ROT_DIM // 2), col(ROT_DIM // 2)],
        out_specs=(col(ATTN_W), col(IDX_HEADS * IDX_DIM), row(LANES),
                   pl.BlockSpec((None, None, VT_ROWS, tm), lambda bi, i: (bi, i, 0, 0)),
                   col(8), row(ssm_w), one, one),
        compiler_params=_params("parallel", "parallel"),
        name="mixer_inproj",
    )(x, mod_l, g.reshape(1, d), wqt, wqit, wkk, wvt, wwt, wu, *tables)


def _rope_tables(positions):
    inv_freq = 1.0 / (ROPE_THETA ** (jnp.arange(0, ROT_DIM, 2, dtype=F32) / ROT_DIM))
    ang = positions.astype(F32)[..., None] * inv_freq
    cos, sin = jnp.cos(ang), jnp.sin(ang)
    half = ROT_DIM // 2
    rest = HEAD_DIM - ROT_DIM
    pad = lambda *parts: jnp.tile(jnp.concatenate(parts, axis=-1), (1, 1, LANES // HEAD_DIM))
    zeros = lambda n: jnp.zeros(cos.shape[:-1] + (n,), F32)
    ones = jnp.ones(cos.shape[:-1] + (rest,), F32)
    return (pad(cos, cos, ones),
            pad(-sin, zeros(half), zeros(rest)),
            pad(zeros(half), sin, zeros(rest)),
            cos.transpose(0, 2, 1), sin.transpose(0, 2, 1))


def _dsa_kernel(qt_ref, qit_ref, wt_ref, qn_ref, kn_ref, kk_ref, vt_ref, o_ref,
                key_ref, bias_ref, acc_ref, *, topk):
    tq = qt_ref.shape[1]
    tk = kk_ref.shape[1]
    i = pl.program_id(1)
    n_tiles = ((i + 1) * tq + tk - 1) // tk
    col = lax.broadcasted_iota(jnp.int32, (1, tq), 1) + i * tq
    q_lim = (col // CHUNK + 1) * CHUNK
    key_pos = lax.broadcasted_iota(jnp.int32, (tk, tq), 0)
    zeros_half = jnp.zeros((HEAD_DIM, tq), MXU_DTYPE)

    qit = qit_ref[...]
    wqi = jnp.concatenate([jnp.concatenate([zeros_half, qit[h * IDX_DIM:(h + 1) * IDX_DIM, :]], axis=0)
                           for h in range(IDX_HEADS)], axis=1)
    w = wt_ref[...] * (IDX_DIM ** -0.5 * IDX_HEADS ** -0.5)
    w_row = jnp.concatenate([w[h:h + 1, :] for h in range(IDX_HEADS)], axis=1)

    def write_keys(t, masked):
        rel = jnp.maximum(_dot(kk_ref[t], wqi), 0.0) * w_row
        s = rel[:, :tq]
        for h in range(1, IDX_HEADS):
            s = s + rel[:, h * tq:(h + 1) * tq]
        s = s + 0.0
        bits = pltpu.bitcast(s, jnp.int32)
        key = jnp.where(bits < 0, bits ^ jnp.int32(0x7FFFFFFF), bits)
        if masked:
            key = jnp.where(key_pos + t * tk < q_lim, key, jnp.int32(INT_MIN))
        key_ref[t] = key

    def score_body(t, carry):
        write_keys(t, False)
        return carry

    lax.fori_loop(0, n_tiles - 1, score_body, 0)
    write_keys(n_tiles - 1, True)

    def count(thr, strict):
        thr_b = jnp.broadcast_to(thr, (8, tq))

        def body(t, accs):
            accs = list(accs)
            for r in range(0, tk, 8):
                key = key_ref[t, r:r + 8, :]
                hit = jnp.where(key > thr_b if strict else key >= thr_b, 1.0, 0.0)
                accs[(r // 8) % COUNT_CHAINS] = accs[(r // 8) % COUNT_CHAINS] + hit
            return tuple(accs)

        accs = lax.fori_loop(0, n_tiles, body, tuple(jnp.zeros((8, tq), F32) for _ in range(COUNT_CHAINS)))
        total = accs[0]
        for a in accs[1:]:
            total = total + a
        return jnp.sum(total, axis=0, keepdims=True)

    k_f = float(topk)
    n0 = count(jnp.zeros((1, tq), jnp.int32), False)
    nonneg = n0 >= k_f
    tau0 = jnp.where(nonneg, jnp.int32(0), jnp.int32(INT_MIN))
    cnt0 = jnp.where(nonneg, n0, (n_tiles * tk).astype(F32))

    def refine(it, tau, cnt):
        trial = tau | lax.shift_left(jnp.int32(1), 30 - it)
        c = count(trial, False)
        ok = c >= k_f
        return jnp.where(ok, trial, tau), jnp.where(ok, c, cnt)

    tau, cnt = lax.fori_loop(0, TAU_PROBE_BIT, lambda it, tc: refine(it, *tc), (tau0, cnt0))
    locked = count(tau + 1, False) < k_f

    def unsettled(tau_cnt):
        return jnp.max(jnp.where(locked | (tau_cnt[1] == k_f), 0.0, 1.0))

    def tail_cond(carry):
        it, _, _, open_rows = carry
        return jnp.logical_and(it < 31, open_rows > 0.0)

    def tail_body(carry):
        it, tau, cnt, _ = carry
        for step in range(TAU_TAIL_STEPS):
            tau, cnt = refine(it + step, tau, cnt)
        return it + TAU_TAIL_STEPS, tau, cnt, unsettled((tau, cnt))

    _, tau, _, _ = lax.while_loop(tail_cond, tail_body,
                                  (jnp.int32(TAU_PROBE_BIT), tau, cnt, unsettled((tau, cnt))))
    need = k_f - count(tau, True)

    lower = (lax.broadcasted_iota(jnp.int32, (tk, tk), 1)
             <= lax.broadcasted_iota(jnp.int32, (tk, tk), 0)).astype(MXU_DTYPE)

    def bias_body(t, ties_before):
        key = key_ref[t]
        tie = jnp.where(key == tau, 1.0, 0.0)
        rank = ties_before + _dot(lower, tie.astype(MXU_DTYPE))
        sel = (key > tau) | ((key == tau) & (rank <= need))
        sel = sel & (key > jnp.int32(INT_MIN))
        bias_ref[t] = jnp.where(sel, 0.0, NEG_BIAS)
        return rank[tk - 1:tk, :]

    lax.fori_loop(0, n_tiles, bias_body, jnp.zeros((1, tq), F32))

    qt = qt_ref[...]
    wq = jnp.concatenate([jnp.concatenate([qt[h * HEAD_DIM:(h + 1) * HEAD_DIM, :], zeros_half], axis=0)
                          for h in range(N_HEADS)], axis=1)

    def logits(t):
        return _dot(kk_ref[t], wq) + jnp.concatenate([bias_ref[t]] * N_HEADS, axis=1)

    safe = jnp.max(qn_ref[...] * kn_ref[...]) <= LOGIT_SAFE * LOGIT_SAFE

    def exact_max():
        def max_body(t, m):
            s = logits(t)
            part = s[:8, :]
            for r in range(8, tk, 8):
                part = jnp.maximum(part, s[r:r + 8, :])
            return jnp.maximum(m, part)
        m = lax.fori_loop(0, n_tiles, max_body, jnp.full((8, N_HEADS * tq), NEG_BIAS, F32))
        return jnp.max(m, axis=0, keepdims=True)

    offset = lax.cond(safe, lambda: jnp.zeros((1, N_HEADS * tq), F32), exact_max)
    acc_ref[...] = jnp.zeros(acc_ref.shape, F32)

    def att_body(t, carry):
        p = jnp.exp(logits(t) - offset)
        acc_ref[...] += _dot(vt_ref[t], p.astype(MXU_DTYPE))
        return carry

    lax.fori_loop(0, n_tiles, att_body, 0)
    acc = acc_ref[...]
    o_ref[...] = (acc[:HEAD_DIM, :] / acc[HEAD_DIM:HEAD_DIM + 1, :]).astype(o_ref.dtype)


def _dsa_call(qt, qit, wt, qn2, kn2, kk, vt):
    b, _, s = qt.shape
    tq = min(Q_TILE, s)
    tk = vt.shape[3]
    n_kt = s // tk
    n_q = s // tq
    topk = min(TOPK_MAX, s // 4)
    assert tk % tq == 0 and s % (tq * qn2.shape[1]) == 0
    q_per_norm_tile = s // qn2.shape[1] // tq
    kk_t = kk.reshape(b, n_kt, tk, LANES)
    kn2 = jnp.max(kn2, axis=1)
    per_batch = lambda shape: pl.BlockSpec((None,) + shape, lambda bi, i: (bi, 0, 0, 0))
    qcol = lambda height: pl.BlockSpec((None, height, tq), lambda bi, i: (bi, 0, i))
    out_t = pl.pallas_call(
        functools.partial(_dsa_kernel, topk=topk),
        out_shape=jax.ShapeDtypeStruct((b, n_q, HEAD_DIM, N_HEADS * tq), MXU_DTYPE),
        grid=(b, n_q),
        in_specs=[qcol(ATTN_W), qcol(IDX_HEADS * IDX_DIM), qcol(8),
                  pl.BlockSpec((None, None, 1, 1), lambda bi, i: (bi, i // q_per_norm_tile, 0, 0)),
                  pl.BlockSpec((None, 1, 1), lambda bi, i: (bi, 0, 0)),
                  per_batch((n_kt, tk, LANES)), per_batch((n_kt, VT_ROWS, tk))],
        out_specs=pl.BlockSpec((None, None, HEAD_DIM, N_HEADS * tq), lambda bi, i: (bi, i, 0, 0)),
        scratch_shapes=[pltpu.VMEM((n_kt, tk, tq), jnp.int32),
                        pltpu.VMEM((n_kt, tk, tq), F32),
                        pltpu.VMEM((VT_ROWS, N_HEADS * tq), F32)],
        compiler_params=_params("parallel", "parallel"),
        name="dsa_attention",
    )(qt, qit, wt, qn2, kn2, kk_t, vt)
    out = out_t.reshape(b, n_q, HEAD_DIM, N_HEADS, tq).transpose(0, 1, 4, 3, 2)
    return out.reshape(b, s, ATTN_W)


def _ssm_kernel(u_ref, toep_ref, win_ref, wout_ref, apow_ref, y_ref, hloc_ref, hswap_ref, hprev_ref, *, batch):
    rows, width = u_ref.shape
    sub = toep_ref.shape[0]
    n_sub = width // sub
    rot = lambda h, r: apow_ref[r:r + 1, :] * h + apow_ref[r + 1:r + 2, :] * pltpu.roll(h, SSM_STATE, 1)
    h = jnp.zeros((rows, 2 * SSM_STATE), F32)
    local = []
    for j in range(n_sub):
        h = rot(h, 0) + _dot(u_ref[:, j * sub:(j + 1) * sub], win_ref[...])
        local.append(h.astype(MXU_DTYPE))
    hloc_ref[...] = h
    hswap_ref[...] = pltpu.roll(h, SSM_STATE, 1)
    a_same, a_swap = apow_ref[2:3, :], apow_ref[3:4, :]
    a_swap_rolled = pltpu.roll(a_swap, SSM_STATE, 1)
    h = jnp.zeros((batch, 2 * SSM_STATE), F32)
    h_sw = jnp.zeros((batch, 2 * SSM_STATE), F32)
    for c in range(rows // batch):
        hprev_ref[c * batch:(c + 1) * batch, :] = h
        h, h_sw = (a_same * h + a_swap * h_sw + hloc_ref[c * batch:(c + 1) * batch, :],
                   a_same * h_sw + a_swap_rolled * h + hswap_ref[c * batch:(c + 1) * batch, :])
    hprev = hprev_ref[...].astype(MXU_DTYPE)
    for j in range(n_sub):
        acc = _dot(hprev, wout_ref[:, j * sub:(j + 1) * sub]) + _dot(u_ref[:, j * sub:(j + 1) * sub], toep_ref[...])
        if j > 0:
            acc = acc + _dot(local[j - 1], wout_ref[:, :sub])
        y_ref[:, j * sub:(j + 1) * sub] = acc.astype(y_ref.dtype)


def _ssm_call(u_g, toep, win, wout, apow, layer, batch):
    g, rows, width = u_g.shape
    sub = toep.shape[2]
    per_group = lambda *shape: pl.BlockSpec((None, None) + shape, lambda gi: (layer, gi) + (0,) * len(shape))
    return pl.pallas_call(
        functools.partial(_ssm_kernel, batch=batch),
        out_shape=jax.ShapeDtypeStruct((g, rows, width), MXU_DTYPE),
        grid=(g,),
        in_specs=[pl.BlockSpec((None, rows, width), lambda gi: (gi, 0, 0)),
                  per_group(sub, sub), per_group(sub, 2 * SSM_STATE),
                  per_group(2 * SSM_STATE, width), per_group(4, 2 * SSM_STATE)],
        out_specs=pl.BlockSpec((None, rows, width), lambda gi: (gi, 0, 0)),
        scratch_shapes=[pltpu.VMEM((rows, 2 * SSM_STATE), F32)] * 3,
        compiler_params=_params("parallel"),
        name="s5_scan",
    )(u_g, toep, win, wout, apow)


def _ssm_operators(a_re, a_im, log_dt, b_re, b_im, c_re, c_im, d_skip, chunk):
    cmul = lambda xr, xi, yr, yi: (xr * yr - xi * yi, xr * yi + xi * yr)
    dt = jnp.exp(log_dt)[..., None]
    lam_re, lam_im = a_re * dt, a_im * dt
    lags = jnp.arange(chunk + 1, dtype=F32)[None, None, :, None]
    mag = jnp.exp(lam_re[:, :, None, :] * lags)
    ang = lam_im[:, :, None, :] * lags
    ap_re, ap_im = mag * jnp.cos(ang), mag * jnp.sin(ang)
    num_re, num_im = ap_re[:, :, 1] - 1.0, ap_im[:, :, 1]
    den = a_re * a_re + a_im * a_im
    coef_re, coef_im = (num_re * a_re + num_im * a_im) / den, (num_im * a_re - num_re * a_im) / den
    bb_re, bb_im = cmul(coef_re[..., None], coef_im[..., None], b_re, b_im)
    depth, g, p = a_re.shape
    gc = b_re.shape[-1]
    sub = SSM_SUB
    ca_re, ca_im = cmul(c_re[:, :, None], c_im[:, :, None], ap_re[:, :, :sub, None, :], ap_im[:, :, :sub, None, :])
    taps = (jnp.einsum('dgtop,dgpi->dgtoi', ca_re, bb_re, precision=lax.Precision.HIGHEST)
            - jnp.einsum('dgtop,dgpi->dgtoi', ca_im, bb_im, precision=lax.Precision.HIGHEST))
    taps = taps.at[:, :, 0].add(d_skip[..., None] * jnp.eye(gc, dtype=F32))
    taps = jnp.concatenate([jnp.zeros((depth, g, sub - 1, gc, gc), F32), taps], axis=2)
    lag_idx = jnp.arange(sub)[None, :] - jnp.arange(sub)[:, None] + (sub - 1)
    toep = taps[:, :, lag_idx]
    toep = toep.transpose(0, 1, 2, 5, 3, 4).reshape(depth, g, sub * gc, sub * gc)
    rev_re, rev_im = ap_re[:, :, sub - 1::-1][:, :, :sub], ap_im[:, :, sub - 1::-1][:, :, :sub]
    wi_re, wi_im = cmul(rev_re[..., None], rev_im[..., None], bb_re[:, :, None], bb_im[:, :, None])
    w_in = jnp.concatenate([wi_re, wi_im], axis=3)
    w_in = w_in.transpose(0, 1, 2, 4, 3).reshape(depth, g, sub * gc, 2 * p)
    wo_re, wo_im = cmul(c_re[:, :, None], c_im[:, :, None], ap_re[:, :, 1:, None, :], ap_im[:, :, 1:, None, :])
    w_out = jnp.concatenate([wo_re, -wo_im], axis=4)
    w_out = w_out.transpose(0, 1, 4, 2, 3).reshape(depth, g, 2 * p, chunk * gc)
    rot_rows = lambda lag: [jnp.concatenate([ap_re[:, :, lag], ap_re[:, :, lag]], -1),
                            jnp.concatenate([-ap_im[:, :, lag], ap_im[:, :, lag]], -1)]
    apow = jnp.stack(rot_rows(sub) + rot_rows(chunk), axis=2)
    return toep.astype(MXU_DTYPE), w_in.astype(MXU_DTYPE), w_out.astype(MXU_DTYPE), apow


def _outproj_kernel(x_ref, attn_ref, y_ref, mod_ref, ag_ref, sg_ref, gw_ref, gb_ref, woa_ref, wos_ref, o_ref):
    a = _rms(attn_ref[...].astype(F32), ag_ref[...]).astype(MXU_DTYPE)
    y = y_ref[...].astype(F32)
    y = 0.5 * y * (1.0 + jnp.tanh(math.sqrt(2.0 / math.pi) * (y + 0.044715 * (y * y * y))))
    y = y * jax.nn.sigmoid(_dot(y.astype(MXU_DTYPE), gw_ref[...]) + gb_ref[...])
    y = _rms(y, sg_ref[...]).astype(MXU_DTYPE)
    mixed = _dot(a, woa_ref[...]) + _dot(y, wos_ref[...])
    o_ref[...] = x_ref[...] + mod_ref[5:6, :] * mixed


def _outproj_call(x, attn, y, mod_l, attn_gain, ssm_gain, glu_w, glu_b, wo_a, wo_s, layer):
    b, s, d = x.shape
    tm = min(ROW_TILE, s)
    aw, sw = attn.shape[2], y.shape[2]
    row = lambda w: pl.BlockSpec((None, tm, w), lambda bi, i: (bi, i, 0))
    vec = lambda w: pl.BlockSpec((None, 1, w), lambda bi, i: (layer, 0, 0))
    wres = lambda r, c: _resident((None, r, c), lambda bi, i: (layer, 0, 0))
    return pl.pallas_call(
        _outproj_kernel,
        out_shape=jax.ShapeDtypeStruct(x.shape, F32),
        grid=(b, s // tm),
        in_specs=[row(d), row(aw), row(sw),
                  pl.BlockSpec((None, N_MOD, d), lambda bi, i: (bi, 0, 0)),
                  vec(aw), vec(sw), wres(sw, sw), vec(sw), wres(aw, d), wres(sw, d)],
        out_specs=row(d),
        compiler_params=_params("parallel", "parallel"),
        name="mixer_outproj",
    )(x, attn, y, mod_l, attn_gain[:, None, :], ssm_gain[:, None, :], glu_w, glu_b[:, None, :], wo_a, wo_s)


def kernel(x, c, positions, mod_w, mod_b, norm_g, ffn1_w1, ffn1_w3, ffn1_w2, ffn2_w1, ffn2_w3, ffn2_w2, w_in, w_out, attn_gain, ssm_gain, ssm_a_re, ssm_a_im, ssm_log_dt, ssm_b_re, ssm_b_im, ssm_c_re, ssm_c_im, ssm_d, glu_w, glu_b, final_g):
    b, s, d = x.shape
    depth = mod_w.shape[0]
    ssm_w = d - ATTN_W
    groups = ssm_w // SSM_GROUP
    chunk = min(SCAN_L, s)
    n_chunks = s // chunk
    cast = lambda a: a.astype(MXU_DTYPE)

    mod = _mod_call(c, mod_w, mod_b)
    tables = _rope_tables(positions)
    ffn1 = (cast(ffn1_w1), cast(ffn1_w3), cast(ffn1_w2))
    ffn2 = (cast(ffn2_w1), cast(ffn2_w3), cast(ffn2_w2))
    o_k = ATTN_W
    o_v = o_k + HEAD_DIM
    o_qi = o_v + HEAD_DIM
    o_ki = o_qi + IDX_HEADS * IDX_DIM
    o_wi = o_ki + IDX_DIM
    o_u = o_wi + IDX_HEADS
    w_in_t = w_in.transpose(0, 2, 1)
    pad_rows = lambda a, rows: jnp.concatenate([a, jnp.zeros((depth, rows - a.shape[1], d), a.dtype)], axis=1)
    wqt = cast(w_in_t[:, :o_k])
    wqit = cast(w_in_t[:, o_qi:o_ki])
    wkk = cast(jnp.concatenate([w_in[:, :, o_k:o_v], w_in[:, :, o_ki:o_wi]], axis=2))
    wvt = cast(pad_rows(w_in_t[:, o_v:o_qi], VT_ROWS))
    wwt = cast(pad_rows(w_in_t[:, o_wi:o_u], 8))
    wu = cast(w_in[:, :, o_u:])
    wo_a, wo_s = cast(w_out[:, :ATTN_W]), cast(w_out[:, ATTN_W:])
    glu_wc = cast(glu_w)
    toep, s_win, s_wout, apow = _ssm_operators(ssm_a_re, ssm_a_im, ssm_log_dt, ssm_b_re, ssm_b_im,
                                               ssm_c_re, ssm_c_im, ssm_d, chunk)

    for l in range(depth):
        x = _ffn_call(x, mod[l], norm_g[l, 0], *ffn1, l, 0)
        qt, qit, kk, vt, wt, u, qn2, kn2 = _inproj_call(x, mod[l], norm_g[l, 1], wqt, wqit, wkk, wvt, wwt, wu,
                                                        tables, l)
        attn = _dsa_call(qt, qit, wt, qn2, kn2, kk, vt)
        u_g = u.reshape(b, n_chunks, chunk, groups, SSM_GROUP).transpose(3, 1, 0, 2, 4)
        u_g = u_g.reshape(groups, n_chunks * b, chunk * SSM_GROUP)
        y_g = _ssm_call(u_g, toep, s_win, s_wout, apow, l, b)
        y = y_g.reshape(groups, n_chunks, b, chunk, SSM_GROUP).transpose(2, 1, 3, 0, 4).reshape(b, s, ssm_w)
        x = _outproj_call(x, attn, y, mod[l], attn_gain, ssm_gain, glu_wc, glu_b, wo_a, wo_s, l)
        x = _ffn_call(x, mod[l], norm_g[l, 2], *ffn2, l, 6, final_g=final_g if l == depth - 1 else None)
    return x
```

```python
import functools
import math

import jax
import jax.numpy as jnp
from jax import lax
from jax.experimental import pallas as pl
from jax.experimental.pallas import tpu as pltpu

F32 = jnp.float32
MXU_DTYPE = jnp.bfloat16

CHUNK = 64
N_HEADS = 8
HEAD_DIM = 64
ATTN_W = N_HEADS * HEAD_DIM
IDX_HEADS = 4
IDX_DIM = 64
TOPK_MAX = 256
SSM_GROUP = 16
SSM_STATE = 64
SCAN_L = 128
ROPE_THETA = 500000.0
ROT_DIM = HEAD_DIM // 4
EPS = 1e-6
N_MOD = 9

LANES = 128
MXU_WIDTH = 256
BF16_SUBLANES = 16
VT_ROWS = -(-(HEAD_DIM + 1) // BF16_SUBLANES) * BF16_SUBLANES
VMEM_LIMIT_BYTES = 56 * 1024 * 1024

INT_MIN = -(2 ** 31)
NEG_BIAS = -1e30

ROW_TILE = 512
Q_TILE = 512
LOGIT_SAFE = 0.004
COUNT_CHAINS = 4
TAU_PROBE_BIT = 23
TAU_TAIL_STEPS = 2
assert (31 - TAU_PROBE_BIT) % TAU_TAIL_STEPS == 0
SSM_SUB = 16


def _params(*sem):
    return pltpu.CompilerParams(dimension_semantics=sem, vmem_limit_bytes=VMEM_LIMIT_BYTES)


def _dot(a, b):
    return jnp.dot(a, b, preferred_element_type=F32)


def _rms(x, g):
    return x * lax.rsqrt(jnp.mean(x * x, axis=-1, keepdims=True) + EPS) * g


def _modulated_norm(x, g, mod_ref, base):
    shift = mod_ref[base:base + 1, :]
    scale = mod_ref[base + 1:base + 2, :]
    return _rms(x, g) * (1.0 + scale) + shift


def _mod_kernel(c_ref, w_ref, b_ref, o_ref):
    c = c_ref[...]
    o_ref[...] = _dot(c * jax.nn.sigmoid(c), w_ref[...]) + b_ref[...]


def _mod_call(c, mod_w, mod_b):
    depth, d, nd = mod_w.shape
    b = c.shape[0]
    rows = -(-b // 8) * 8
    c_pad = jnp.zeros((rows, d), F32).at[:b].set(c)
    out = pl.pallas_call(
        _mod_kernel,
        out_shape=jax.ShapeDtypeStruct((depth, rows, nd), F32),
        grid=(depth, nd // d),
        in_specs=[pl.BlockSpec((rows, d), lambda l, j: (0, 0)),
                  pl.BlockSpec((None, d, d), lambda l, j: (l, 0, j)),
                  pl.BlockSpec((None, 1, d), lambda l, j: (l, 0, j))],
        out_specs=pl.BlockSpec((None, rows, d), lambda l, j: (l, 0, j)),
        compiler_params=_params("parallel", "parallel"),
        name="adaln_mod",
    )(c_pad, mod_w, mod_b.reshape(depth, 1, nd))
    return out[:, :b].reshape(depth, b, N_MOD, d)


def _ffn_kernel(x_ref, mod_ref, g_ref, w1_ref, w3_ref, w2_ref, o_ref, *, base, ff_cuts, final_norm):
    x = x_ref[...]
    h = _modulated_norm(x, g_ref[0:1, :], mod_ref, base).astype(MXU_DTYPE)
    acc = jnp.zeros(x.shape, F32)
    for c0, c1 in zip(ff_cuts[:-1], ff_cuts[1:]):
        a = _dot(h, w1_ref[:, c0:c1])
        b = _dot(h, w3_ref[:, c0:c1])
        t = (a * jax.nn.sigmoid(a) * b).astype(MXU_DTYPE)
        acc = acc + _dot(t, w2_ref[c0:c1, :])
    y = x + 0.5 * mod_ref[base + 2:base + 3, :] * acc
    o_ref[...] = _rms(y, g_ref[1:2, :]) if final_norm else y


def _resident(shape, index_map):
    return pl.BlockSpec(shape, index_map, pipeline_mode=pl.Buffered(1))


def _ffn_call(x, mod_l, g, w1, w3, w2, layer, base, final_g=None):
    b, s, d = x.shape
    gains = jnp.stack([g, g if final_g is None else final_g])
    ff = w1.shape[2]
    tm = min(ROW_TILE, s)
    mid = -(-ff // (2 * MXU_WIDTH)) * MXU_WIDTH
    ff_cuts = (0, mid, ff) if 0 < mid < ff else (0, ff)
    wspec_in = _resident((None, d, ff), lambda bi, i: (layer, 0, 0))
    wspec_out = _resident((None, ff, d), lambda bi, i: (layer, 0, 0))
    return pl.pallas_call(
        functools.partial(_ffn_kernel, base=base, ff_cuts=ff_cuts, final_norm=final_g is not None),
        out_shape=jax.ShapeDtypeStruct(x.shape, F32),
        grid=(b, s // tm),
        in_specs=[pl.BlockSpec((None, tm, d), lambda bi, i: (bi, i, 0)),
                  pl.BlockSpec((None, N_MOD, d), lambda bi, i: (bi, 0, 0)),
                  pl.BlockSpec((2, d), lambda bi, i: (0, 0)),
                  wspec_in, wspec_in, wspec_out],
        out_specs=pl.BlockSpec((None, tm, d), lambda bi, i: (bi, i, 0)),
        compiler_params=_params("parallel", "parallel"),
        name="ffn",
    )(x, mod_l, gains, w1, w3, w2)


def _rope128(t, cos, s_lo, s_hi):
    half = ROT_DIM // 2
    return t * cos + pltpu.roll(t, LANES - half, 1) * s_lo + pltpu.roll(t, half, 1) * s_hi


def _dot_nt(a, b):
    return lax.dot_general(a, b, (((1,), (1,)), ((), ())), preferred_element_type=F32)


def _rope_rows(t, cos_t, sin_t):
    half = ROT_DIM // 2
    parts = []
    for base in range(0, t.shape[0], HEAD_DIM):
        t1, t2 = t[base:base + half], t[base + half:base + ROT_DIM]
        parts += [t1 * cos_t - t2 * sin_t, t2 * cos_t + t1 * sin_t, t[base + ROT_DIM:base + HEAD_DIM]]
    return jnp.concatenate(parts, axis=0)


def _inproj_kernel(x_ref, mod_ref, g_ref, wqt_ref, wqit_ref, wkk_ref, wvt_ref, wwt_ref, wu_ref,
                   cos_ref, slo_ref, shi_ref, cost_ref, sint_ref,
                   qt_ref, qit_ref, kk_ref, vt_ref, wt_ref, u_ref, qn_ref, kn_ref):
    h = _modulated_norm(x_ref[...], g_ref[...], mod_ref, 3).astype(MXU_DTYPE)
    cos_t, sin_t = cost_ref[...], sint_ref[...]
    qt = (_rope_rows(_dot_nt(wqt_ref[...], h), cos_t, sin_t) * HEAD_DIM ** -0.5).astype(qt_ref.dtype)
    qt_ref[...] = qt
    qit_ref[...] = _rope_rows(_dot_nt(wqit_ref[...], h), cos_t, sin_t).astype(qit_ref.dtype)
    kk = _rope128(_dot(h, wkk_ref[...]), cos_ref[...], slo_ref[...], shi_ref[...]).astype(kk_ref.dtype)
    kk_ref[...] = kk
    vt = _dot_nt(wvt_ref[...], h)
    row = lax.broadcasted_iota(jnp.int32, vt.shape, 0)
    vt_ref[...] = jnp.where(row == HEAD_DIM, 1.0, vt).astype(vt_ref.dtype)
    wt_ref[...] = _dot_nt(wwt_ref[...], h)
    u_ref[...] = _dot(h, wu_ref[...]).astype(u_ref.dtype)
    q2 = qt.astype(F32) * qt.astype(F32)
    qn_ref[...] = jnp.max(jnp.concatenate(
        [jnp.sum(q2[b0:b0 + HEAD_DIM], axis=0, keepdims=True) for b0 in range(0, q2.shape[0], HEAD_DIM)], axis=0),
        keepdims=True)
    k2 = kk.astype(F32) * kk.astype(F32)
    lane = lax.broadcasted_iota(jnp.int32, k2.shape, 1)
    kn_ref[...] = jnp.max(jnp.sum(jnp.where(lane < HEAD_DIM, k2, 0.0), axis=1, keepdims=True), keepdims=True)


def _inproj_call(x, mod_l, g, wqt, wqit, wkk, wvt, wwt, wu, tables, layer):
    b, s, d = x.shape
    tm = min(ROW_TILE, s)
    n_t = s // tm
    row = lambda w: pl.BlockSpec((None, tm, w), lambda bi, i: (bi, i, 0))
    col = lambda hgt: pl.BlockSpec((None, hgt, tm), lambda bi, i: (bi, 0, i))
    wres = lambda r, c: _resident((None, r, c), lambda bi, i: (layer, 0, 0))
    one = pl.BlockSpec((None, None, 1, 1), lambda bi, i: (bi, i, 0, 0))
    ssm_w = wu.shape[2]
    return pl.pallas_call(
        _inproj_kernel,
        out_shape=(jax.ShapeDtypeStruct((b, ATTN_W, s), MXU_DTYPE),
                   jax.ShapeDtypeStruct((b, IDX_HEADS * IDX_DIM, s), MXU_DTYPE),
                   jax.ShapeDtypeStruct((b, s, LANES), MXU_DTYPE),
                   jax.ShapeDtypeStruct((b, n_t, VT_ROWS, tm), MXU_DTYPE),
                   jax.ShapeDtypeStruct((b, 8, s), F32),
                   jax.ShapeDtypeStruct((b, s, ssm_w), MXU_DTYPE),
                   jax.ShapeDtypeStruct((b, n_t, 1, 1), F32),
                   jax.ShapeDtypeStruct((b, n_t, 1, 1), F32)),
        grid=(b, n_t),
        in_specs=[row(d),
                  pl.BlockSpec((None, N_MOD, d), lambda bi, i: (bi, 0, 0)),
                  pl.BlockSpec((1, d), lambda bi, i: (0, 0)),
                  wres(ATTN_W, d), wres(IDX_HEADS * IDX_DIM, d), wres(d, LANES), wres(VT_ROWS, d), wres(8, d),
                  wres(d, ssm_w),
                  row(LANES), row(LANES), row(LANES), col(ROT_DIM // 2), col(ROT_DIM // 2)],
        out_specs=(col(ATTN_W), col(IDX_HEADS * IDX_DIM), row(LANES),
                   pl.BlockSpec((None, None, VT_ROWS, tm), lambda bi, i: (bi, i, 0, 0)),
                   col(8), row(ssm_w), one, one),
        compiler_params=_params("parallel", "parallel"),
        name="mixer_inproj",
    )(x, mod_l, g.reshape(1, d), wqt, wqit, wkk, wvt, wwt, wu, *tables)


def _rope_tables(positions):
    inv_freq = 1.0 / (ROPE_THETA ** (jnp.arange(0, ROT_DIM, 2, dtype=F32) / ROT_DIM))
    ang = positions.astype(F32)[..., None] * inv_freq
    cos, sin = jnp.cos(ang), jnp.sin(ang)
    half = ROT_DIM // 2
    rest = HEAD_DIM - ROT_DIM
    pad = lambda *parts: jnp.tile(jnp.concatenate(parts, axis=-1), (1, 1, LANES // HEAD_DIM))
    zeros = lambda n: jnp.zeros(cos.shape[:-1] + (n,), F32)
    ones = jnp.ones(cos.shape[:-1] + (rest,), F32)
    return (pad(cos, cos, ones),
            pad(-sin, zeros(half), zeros(rest)),
            pad(zeros(half), sin, zeros(rest)),
            cos.transpose(0, 2, 1), sin.transpose(0, 2, 1))


def _dsa_kernel(qt_ref, qit_ref, wt_ref, qn_ref, kn_ref, kk_ref, vt_ref, o_ref,
                key_ref, bias_ref, acc_ref, *, topk):
    tq = qt_ref.shape[1]
    tk = kk_ref.shape[1]
    i = pl.program_id(1)
    n_tiles = ((i + 1) * tq + tk - 1) // tk
    col = lax.broadcasted_iota(jnp.int32, (1, tq), 1) + i * tq
    q_lim = (col // CHUNK + 1) * CHUNK
    key_pos = lax.broadcasted_iota(jnp.int32, (tk, tq), 0)
    zeros_half = jnp.zeros((HEAD_DIM, tq), MXU_DTYPE)

    qit = qit_ref[...]
    wqi = jnp.concatenate([jnp.concatenate([zeros_half, qit[h * IDX_DIM:(h + 1) * IDX_DIM, :]], axis=0)
                           for h in range(IDX_HEADS)], axis=1)
    w = wt_ref[...] * (IDX_DIM ** -0.5 * IDX_HEADS ** -0.5)
    w_row = jnp.concatenate([w[h:h + 1, :] for h in range(IDX_HEADS)], axis=1)

    def write_keys(t, masked):
        rel = jnp.maximum(_dot(kk_ref[t], wqi), 0.0) * w_row
        s = rel[:, :tq]
        for h in range(1, IDX_HEADS):
            s = s + rel[:, h * tq:(h + 1) * tq]
        s = s + 0.0
        bits = pltpu.bitcast(s, jnp.int32)
        key = jnp.where(bits < 0, bits ^ jnp.int32(0x7FFFFFFF), bits)
        if masked:
            key = jnp.where(key_pos + t * tk < q_lim, key, jnp.int32(INT_MIN))
        key_ref[t] = key

    def score_body(t, carry):
        write_keys(t, False)
        return carry

    lax.fori_loop(0, n_tiles - 1, score_body, 0)
    write_keys(n_tiles - 1, True)

    def count(thr, strict):
        thr_b = jnp.broadcast_to(thr, (8, tq))

        def body(t, accs):
            accs = list(accs)
            for r in range(0, tk, 8):
                key = key_ref[t, r:r + 8, :]
                hit = jnp.where(key > thr_b if strict else key >= thr_b, 1.0, 0.0)
                accs[(r // 8) % COUNT_CHAINS] = accs[(r // 8) % COUNT_CHAINS] + hit
            return tuple(accs)

        accs = lax.fori_loop(0, n_tiles, body, tuple(jnp.zeros((8, tq), F32) for _ in range(COUNT_CHAINS)))
        total = accs[0]
        for a in accs[1:]:
            total = total + a
        return jnp.sum(total, axis=0, keepdims=True)

    k_f = float(topk)
    n0 = count(jnp.zeros((1, tq), jnp.int32), False)
    nonneg = n0 >= k_f
    tau0 = jnp.where(nonneg, jnp.int32(0), jnp.int32(INT_MIN))
    cnt0 = jnp.where(nonneg, n0, (n_tiles * tk).astype(F32))

    def refine(it, tau, cnt):
        trial = tau | lax.shift_left(jnp.int32(1), 30 - it)
        c = count(trial, False)
        ok = c >= k_f
        return jnp.where(ok, trial, tau), jnp.where(ok, c, cnt)

    tau, cnt = lax.fori_loop(0, TAU_PROBE_BIT, lambda it, tc: refine(it, *tc), (tau0, cnt0))
    locked = count(tau + 1, False) < k_f

    def unsettled(tau_cnt):
        return jnp.max(jnp.where(locked | (tau_cnt[1] == k_f), 0.0, 1.0))

    def tail_cond(carry):
        it, _, _, open_rows = carry
        return jnp.logical_and(it < 31, open_rows > 0.0)

    def tail_body(carry):
        it, tau, cnt, _ = carry
        for step in range(TAU_TAIL_STEPS):
            tau, cnt = refine(it + step, tau, cnt)
        return it + TAU_TAIL_STEPS, tau, cnt, unsettled((tau, cnt))

    _, tau, _, _ = lax.while_loop(tail_cond, tail_body,
                                  (jnp.int32(TAU_PROBE_BIT), tau, cnt, unsettled((tau, cnt))))
    need = k_f - count(tau, True)

    lower = (lax.broadcasted_iota(jnp.int32, (tk, tk), 1)
             <= lax.broadcasted_iota(jnp.int32, (tk, tk), 0)).astype(MXU_DTYPE)

    def bias_body(t, ties_before):
        key = key_ref[t]
        tie = jnp.where(key == tau, 1.0, 0.0)
        rank = ties_before + _dot(lower, tie.astype(MXU_DTYPE))
        sel = (key > tau) | ((key == tau) & (rank <= need))
        sel = sel & (key > jnp.int32(INT_MIN))
        bias_ref[t] = jnp.where(sel, 0.0, NEG_BIAS)
        return rank[tk - 1:tk, :]

    lax.fori_loop(0, n_tiles, bias_body, jnp.zeros((1, tq), F32))

    qt = qt_ref[...]
    wq = jnp.concatenate([jnp.concatenate([qt[h * HEAD_DIM:(h + 1) * HEAD_DIM, :], zeros_half], axis=0)
                          for h in range(N_HEADS)], axis=1)

    def logits(t):
        return _dot(kk_ref[t], wq) + jnp.concatenate([bias_ref[t]] * N_HEADS, axis=1)

    safe = jnp.max(qn_ref[...] * kn_ref[...]) <= LOGIT_SAFE * LOGIT_SAFE

    def exact_max():
        def max_body(t, m):
            s = logits(t)
            part = s[:8, :]
            for r in range(8, tk, 8):
                part = jnp.maximum(part, s[r:r + 8, :])
            return jnp.maximum(m, part)
        m = lax.fori_loop(0, n_tiles, max_body, jnp.full((8, N_HEADS * tq), NEG_BIAS, F32))
        return jnp.max(m, axis=0, keepdims=True)

    offset = lax.cond(safe, lambda: jnp.zeros((1, N_HEADS * tq), F32), exact_max)
    acc_ref[...] = jnp.zeros(acc_ref.shape, F32)

    def att_body(t, carry):
        p = jnp.exp(logits(t) - offset)
        acc_ref[...] += _dot(vt_ref[t], p.astype(MXU_DTYPE))
        return carry

    lax.fori_loop(0, n_tiles, att_body, 0)
    acc = acc_ref[...]
    o_ref[...] = (acc[:HEAD_DIM, :] / acc[HEAD_DIM:HEAD_DIM + 1, :]).astype(o_ref.dtype)


def _dsa_call(qt, qit, wt, qn2, kn2, kk, vt):
    b, _, s = qt.shape
    tq = min(Q_TILE, s)
    tk = vt.shape[3]
    n_kt = s // tk
    n_q = s // tq
    topk = min(TOPK_MAX, s // 4)
    assert tk % tq == 0 and s % (tq * qn2.shape[1]) == 0
    q_per_norm_tile = s // qn2.shape[1] // tq
    kk_t = kk.reshape(b, n_kt, tk, LANES)
    kn2 = jnp.max(kn2, axis=1)
    per_batch = lambda shape: pl.BlockSpec((None,) + shape, lambda bi, i: (bi, 0, 0, 0))
    qcol = lambda height: pl.BlockSpec((None, height, tq), lambda bi, i: (bi, 0, i))
    out_t = pl.pallas_call(
        functools.partial(_dsa_kernel, topk=topk),
        out_shape=jax.ShapeDtypeStruct((b, n_q, HEAD_DIM, N_HEADS * tq), MXU_DTYPE),
        grid=(b, n_q),
        in_specs=[qcol(ATTN_W), qcol(IDX_HEADS * IDX_DIM), qcol(8),
                  pl.BlockSpec((None, None, 1, 1), lambda bi, i: (bi, i // q_per_norm_tile, 0, 0)),
                  pl.BlockSpec((None, 1, 1), lambda bi, i: (bi, 0, 0)),
                  per_batch((n_kt, tk, LANES)), per_batch((n_kt, VT_ROWS, tk))],
        out_specs=pl.BlockSpec((None, None, HEAD_DIM, N_HEADS * tq), lambda bi, i: (bi, i, 0, 0)),
        scratch_shapes=[pltpu.VMEM((n_kt, tk, tq), jnp.int32),
                        pltpu.VMEM((n_kt, tk, tq), F32),
                        pltpu.VMEM((VT_ROWS, N_HEADS * tq), F32)],
        compiler_params=_params("parallel", "parallel"),
        name="dsa_attention",
    )(qt, qit, wt, qn2, kn2, kk_t, vt)
    out = out_t.reshape(b, n_q, HEAD_DIM, N_HEADS, tq).transpose(0, 1, 4, 3, 2)
    return out.reshape(b, s, ATTN_W)


def _ssm_kernel(u_ref, toep_ref, win_ref, wout_ref, apow_ref, y_ref, hloc_ref, hswap_ref, hprev_ref, *, batch):
    rows, width = u_ref.shape
    sub = toep_ref.shape[0]
    n_sub = width // sub
    rot = lambda h, r: apow_ref[r:r + 1, :] * h + apow_ref[r + 1:r + 2, :] * pltpu.roll(h, SSM_STATE, 1)
    h = jnp.zeros((rows, 2 * SSM_STATE), F32)
    local = []
    for j in range(n_sub):
        h = rot(h, 0) + _dot(u_ref[:, j * sub:(j + 1) * sub], win_ref[...])
        local.append(h.astype(MXU_DTYPE))
    hloc_ref[...] = h
    hswap_ref[...] = pltpu.roll(h, SSM_STATE, 1)
    a_same, a_swap = apow_ref[2:3, :], apow_ref[3:4, :]
    a_swap_rolled = pltpu.roll(a_swap, SSM_STATE, 1)
    h = jnp.zeros((batch, 2 * SSM_STATE), F32)
    h_sw = jnp.zeros((batch, 2 * SSM_STATE), F32)
    for c in range(rows // batch):
        hprev_ref[c * batch:(c + 1) * batch, :] = h
        h, h_sw = (a_same * h + a_swap * h_sw + hloc_ref[c * batch:(c + 1) * batch, :],
                   a_same * h_sw + a_swap_rolled * h + hswap_ref[c * batch:(c + 1) * batch, :])
    hprev = hprev_ref[...].astype(MXU_DTYPE)
    for j in range(n_sub):
        acc = _dot(hprev, wout_ref[:, j * sub:(j + 1) * sub]) + _dot(u_ref[:, j * sub:(j + 1) * sub], toep_ref[...])
        if j > 0:
            acc = acc + _dot(local[j - 1], wout_ref[:, :sub])
        y_ref[:, j * sub:(j + 1) * sub] = acc.astype(y_ref.dtype)


def _ssm_call(u_g, toep, win, wout, apow, layer, batch):
    g, rows, width = u_g.shape
    sub = toep.shape[2]
    per_group = lambda *shape: pl.BlockSpec((None, None) + shape, lambda gi: (layer, gi) + (0,) * len(shape))
    return pl.pallas_call(
        functools.partial(_ssm_kernel, batch=batch),
        out_shape=jax.ShapeDtypeStruct((g, rows, width), MXU_DTYPE),
        grid=(g,),
        in_specs=[pl.BlockSpec((None, rows, width), lambda gi: (gi, 0, 0)),
                  per_group(sub, sub), per_group(sub, 2 * SSM_STATE),
                  per_group(2 * SSM_STATE, width), per_group(4, 2 * SSM_STATE)],
        out_specs=pl.BlockSpec((None, rows, width), lambda gi: (gi, 0, 0)),
        scratch_shapes=[pltpu.VMEM((rows, 2 * SSM_STATE), F32)] * 3,
        compiler_params=_params("parallel"),
        name="s5_scan",
    )(u_g, toep, win, wout, apow)


def _ssm_operators(a_re, a_im, log_dt, b_re, b_im, c_re, c_im, d_skip, chunk):
    cmul = lambda xr, xi, yr, yi: (xr * yr - xi * yi, xr * yi + xi * yr)
    dt = jnp.exp(log_dt)[..., None]
    lam_re, lam_im = a_re * dt, a_im * dt
    lags = jnp.arange(chunk + 1, dtype=F32)[None, None, :, None]
    mag = jnp.exp(lam_re[:, :, None, :] * lags)
    ang = lam_im[:, :, None, :] * lags
    ap_re, ap_im = mag * jnp.cos(ang), mag * jnp.sin(ang)
    num_re, num_im = ap_re[:, :, 1] - 1.0, ap_im[:, :, 1]
    den = a_re * a_re + a_im * a_im
    coef_re, coef_im = (num_re * a_re + num_im * a_im) / den, (num_im * a_re - num_re * a_im) / den
    bb_re, bb_im = cmul(coef_re[..., None], coef_im[..., None], b_re, b_im)
    depth, g, p = a_re.shape
    gc = b_re.shape[-1]
    sub = SSM_SUB
    ca_re, ca_im = cmul(c_re[:, :, None], c_im[:, :, None], ap_re[:, :, :sub, None, :], ap_im[:, :, :sub, None, :])
    taps = (jnp.einsum('dgtop,dgpi->dgtoi', ca_re, bb_re, precision=lax.Precision.HIGHEST)
            - jnp.einsum('dgtop,dgpi->dgtoi', ca_im, bb_im, precision=lax.Precision.HIGHEST))
    taps = taps.at[:, :, 0].add(d_skip[..., None] * jnp.eye(gc, dtype=F32))
    taps = jnp.concatenate([jnp.zeros((depth, g, sub - 1, gc, gc), F32), taps], axis=2)
    lag_idx = jnp.arange(sub)[None, :] - jnp.arange(sub)[:, None] + (sub - 1)
    toep = taps[:, :, lag_idx]
    toep = toep.transpose(0, 1, 2, 5, 3, 4).reshape(depth, g, sub * gc, sub * gc)
    rev_re, rev_im = ap_re[:, :, sub - 1::-1][:, :, :sub], ap_im[:, :, sub - 1::-1][:, :, :sub]
    wi_re, wi_im = cmul(rev_re[..., None], rev_im[..., None], bb_re[:, :, None], bb_im[:, :, None])
    w_in = jnp.concatenate([wi_re, wi_im], axis=3)
    w_in = w_in.transpose(0, 1, 2, 4, 3).reshape(depth, g, sub * gc, 2 * p)
    wo_re, wo_im = cmul(c_re[:, :, None], c_im[:, :, None], ap_re[:, :, 1:, None, :], ap_im[:, :, 1:, None, :])
    w_out = jnp.concatenate([wo_re, -wo_im], axis=4)
    w_out = w_out.transpose(0, 1, 4, 2, 3).reshape(depth, g, 2 * p, chunk * gc)
    rot_rows = lambda lag: [jnp.concatenate([ap_re[:, :, lag], ap_re[:, :, lag]], -1),
                            jnp.concatenate([-ap_im[:, :, lag], ap_im[:, :, lag]], -1)]
    apow = jnp.stack(rot_rows(sub) + rot_rows(chunk), axis=2)
    return toep.astype(MXU_DTYPE), w_in.astype(MXU_DTYPE), w_out.astype(MXU_DTYPE), apow


def _outproj_kernel(x_ref, attn_ref, y_ref, mod_ref, ag_ref, sg_ref, gw_ref, gb_ref, woa_ref, wos_ref, o_ref):
    a = _rms(attn_ref[...].astype(F32), ag_ref[...]).astype(MXU_DTYPE)
    y = y_ref[...].astype(F32)
    y = 0.5 * y * (1.0 + jnp.tanh(math.sqrt(2.0 / math.pi) * (y + 0.044715 * (y * y * y))))
    y = y * jax.nn.sigmoid(_dot(y.astype(MXU_DTYPE), gw_ref[...]) + gb_ref[...])
    y = _rms(y, sg_ref[...]).astype(MXU_DTYPE)
    mixed = _dot(a, woa_ref[...]) + _dot(y, wos_ref[...])
    o_ref[...] = x_ref[...] + mod_ref[5:6, :] * mixed


def _outproj_call(x, attn, y, mod_l, attn_gain, ssm_gain, glu_w, glu_b, wo_a, wo_s, layer):
    b, s, d = x.shape
    tm = min(ROW_TILE, s)
    aw, sw = attn.shape[2], y.shape[2]
    row = lambda w: pl.BlockSpec((None, tm, w), lambda bi, i: (bi, i, 0))
    vec = lambda w: pl.BlockSpec((None, 1, w), lambda bi, i: (layer, 0, 0))
    wres = lambda r, c: _resident((None, r, c), lambda bi, i: (layer, 0, 0))
    return pl.pallas_call(
        _outproj_kernel,
        out_shape=jax.ShapeDtypeStruct(x.shape, F32),
        grid=(b, s // tm),
        in_specs=[row(d), row(aw), row(sw),
                  pl.BlockSpec((None, N_MOD, d), lambda bi, i: (bi, 0, 0)),
                  vec(aw), vec(sw), wres(sw, sw), vec(sw), wres(aw, d), wres(sw, d)],
        out_specs=row(d),
        compiler_params=_params("parallel", "parallel"),
        name="mixer_outproj",
    )(x, attn, y, mod_l, attn_gain[:, None, :], ssm_gain[:, None, :], glu_w, glu_b[:, None, :], wo_a, wo_s)


def kernel(x, c, positions, mod_w, mod_b, norm_g, ffn1_w1, ffn1_w3, ffn1_w2, ffn2_w1, ffn2_w3, ffn2_w2, w_in, w_out, attn_gain, ssm_gain, ssm_a_re, ssm_a_im, ssm_log_dt, ssm_b_re, ssm_b_im, ssm_c_re, ssm_c_im, ssm_d, glu_w, glu_b, final_g):
    b, s, d = x.shape
    depth = mod_w.shape[0]
    ssm_w = d - ATTN_W
    groups = ssm_w // SSM_GROUP
    chunk = min(SCAN_L, s)
    n_chunks = s // chunk
    cast = lambda a: a.astype(MXU_DTYPE)

    mod = _mod_call(c, mod_w, mod_b)
    tables = _rope_tables(positions)
    ffn1 = (cast(ffn1_w1), cast(ffn1_w3), cast(ffn1_w2))
    ffn2 = (cast(ffn2_w1), cast(ffn2_w3), cast(ffn2_w2))
    o_k = ATTN_W
    o_v = o_k + HEAD_DIM
    o_qi = o_v + HEAD_DIM
    o_ki = o_qi + IDX_HEADS * IDX_DIM
    o_wi = o_ki + IDX_DIM
    o_u = o_wi + IDX_HEADS
    w_in_t = w_in.transpose(0, 2, 1)
    pad_rows = lambda a, rows: jnp.concatenate([a, jnp.zeros((depth, rows - a.shape[1], d), a.dtype)], axis=1)
    wqt = cast(w_in_t[:, :o_k])
    wqit = cast(w_in_t[:, o_qi:o_ki])
    wkk = cast(jnp.concatenate([w_in[:, :, o_k:o_v], w_in[:, :, o_ki:o_wi]], axis=2))
    wvt = cast(pad_rows(w_in_t[:, o_v:o_qi], VT_ROWS))
    wwt = cast(pad_rows(w_in_t[:, o_wi:o_u], 8))
    wu = cast(w_in[:, :, o_u:])
    wo_a, wo_s = cast(w_out[:, :ATTN_W]), cast(w_out[:, ATTN_W:])
    glu_wc = cast(glu_w)
    toep, s_win, s_wout, apow = _ssm_operators(ssm_a_re, ssm_a_im, ssm_log_dt, ssm_b_re, ssm_b_im,
                                               ssm_c_re, ssm_c_im, ssm_d, chunk)

    for l in range(depth):
        x = _ffn_call(x, mod[l], norm_g[l, 0], *ffn1, l, 0)
        qt, qit, kk, vt, wt, u, qn2, kn2 = _inproj_call(x, mod[l], norm_g[l, 1], wqt, wqit, wkk, wvt, wwt, wu,
                                                        tables, l)
        attn = _dsa_call(qt, qit, wt, qn2, kn2, kk, vt)
        u_g = u.reshape(b, n_chunks, chunk, groups, SSM_GROUP).transpose(3, 1, 0, 2, 4)
        u_g = u_g.reshape(groups, n_chunks * b, chunk * SSM_GROUP)
        y_g = _ssm_call(u_g, toep, s_win, s_wout, apow, l, b)
        y = y_g.reshape(groups, n_chunks, b, chunk, SSM_GROUP).transpose(2, 1, 3, 0, 4).reshape(b, s, ssm_w)
        x = _outproj_call(x, attn, y, mod[l], attn_gain, ssm_gain, glu_wc, glu_b, wo_a, wo_s, l)
        x = _ffn_call(x, mod[l], norm_g[l, 2], *ffn2, l, 6, final_g=final_g if l == depth - 1 else None)
    return x
```

```python
import functools
import math

import jax
import jax.numpy as jnp
from jax import lax
from jax.experimental import pallas as pl
from jax.experimental.pallas import tpu as pltpu

F32 = jnp.float32
MXU_DTYPE = jnp.bfloat16

CHUNK = 64
N_HEADS = 8
HEAD_DIM = 64
ATTN_W = N_HEADS * HEAD_DIM
IDX_HEADS = 4
IDX_DIM = 64
TOPK_MAX = 256
SSM_GROUP = 16
SSM_STATE = 64
SCAN_L = 128
ROPE_THETA = 500000.0
ROT_DIM = HEAD_DIM // 4
EPS = 1e-6
N_MOD = 9

LANES = 128
MXU_WIDTH = 256
BF16_SUBLANES = 16
VT_ROWS = -(-(HEAD_DIM + 1) // BF16_SUBLANES) * BF16_SUBLANES
VMEM_LIMIT_BYTES = 56 * 1024 * 1024

INT_MIN = -(2 ** 31)
NEG_BIAS = -1e30

ROW_TILE = 512
Q_TILE = 512
LOGIT_SAFE = 40.0
COUNT_CHAINS = 4
TAU_PROBE_BIT = 23
TAU_TAIL_STEPS = 2
assert (31 - TAU_PROBE_BIT) % TAU_TAIL_STEPS == 0
SSM_SUB = 16


def _params(*sem):
    return pltpu.CompilerParams(dimension_semantics=sem, vmem_limit_bytes=VMEM_LIMIT_BYTES)


def _dot(a, b):
    return jnp.dot(a, b, preferred_element_type=F32)


def _rms(x, g):
    return x * lax.rsqrt(jnp.mean(x * x, axis=-1, keepdims=True) + EPS) * g


def _modulated_norm(x, g, mod_ref, base):
    shift = mod_ref[base:base + 1, :]
    scale = mod_ref[base + 1:base + 2, :]
    return _rms(x, g) * (1.0 + scale) + shift


def _mod_kernel(c_ref, w_ref, b_ref, o_ref):
    c = c_ref[...]
    o_ref[...] = _dot(c * jax.nn.sigmoid(c), w_ref[...]) + b_ref[...]


def _mod_call(c, mod_w, mod_b):
    depth, d, nd = mod_w.shape
    b = c.shape[0]
    rows = -(-b // 8) * 8
    c_pad = jnp.zeros((rows, d), F32).at[:b].set(c)
    out = pl.pallas_call(
        _mod_kernel,
        out_shape=jax.ShapeDtypeStruct((depth, rows, nd), F32),
        grid=(depth, nd // d),
        in_specs=[pl.BlockSpec((rows, d), lambda l, j: (0, 0)),
                  pl.BlockSpec((None, d, d), lambda l, j: (l, 0, j)),
                  pl.BlockSpec((None, 1, d), lambda l, j: (l, 0, j))],
        out_specs=pl.BlockSpec((None, rows, d), lambda l, j: (l, 0, j)),
        compiler_params=_params("parallel", "parallel"),
        name="adaln_mod",
    )(c_pad, mod_w, mod_b.reshape(depth, 1, nd))
    return out[:, :b].reshape(depth, b, N_MOD, d)


def _ffn_kernel(x_ref, mod_ref, g_ref, w1_ref, w3_ref, w2_ref, o_ref, *, base, ff_cuts, final_norm):
    x = x_ref[...]
    h = _modulated_norm(x, g_ref[0:1, :], mod_ref, base).astype(MXU_DTYPE)
    acc = jnp.zeros(x.shape, F32)
    for c0, c1 in zip(ff_cuts[:-1], ff_cuts[1:]):
        a = _dot(h, w1_ref[:, c0:c1])
        b = _dot(h, w3_ref[:, c0:c1])
        t = (a * jax.nn.sigmoid(a) * b).astype(MXU_DTYPE)
        acc = acc + _dot(t, w2_ref[c0:c1, :])
    y = x + 0.5 * mod_ref[base + 2:base + 3, :] * acc
    o_ref[...] = _rms(y, g_ref[1:2, :]) if final_norm else y


def _resident(shape, index_map):
    return pl.BlockSpec(shape, index_map, pipeline_mode=pl.Buffered(1))


def _ffn_call(x, mod_l, g, w1, w3, w2, layer, base, final_g=None):
    b, s, d = x.shape
    gains = jnp.stack([g, g if final_g is None else final_g])
    ff = w1.shape[2]
    tm = min(ROW_TILE, s)
    mid = -(-ff // (2 * MXU_WIDTH)) * MXU_WIDTH
    ff_cuts = (0, mid, ff) if 0 < mid < ff else (0, ff)
    wspec_in = _resident((None, d, ff), lambda bi, i: (layer, 0, 0))
    wspec_out = _resident((None, ff, d), lambda bi, i: (layer, 0, 0))
    return pl.pallas_call(
        functools.partial(_ffn_kernel, base=base, ff_cuts=ff_cuts, final_norm=final_g is not None),
        out_shape=jax.ShapeDtypeStruct(x.shape, F32),
        grid=(b, s // tm),
        in_specs=[pl.BlockSpec((None, tm, d), lambda bi, i: (bi, i, 0)),
                  pl.BlockSpec((None, N_MOD, d), lambda bi, i: (bi, 0, 0)),
                  pl.BlockSpec((2, d), lambda bi, i: (0, 0)),
                  wspec_in, wspec_in, wspec_out],
        out_specs=pl.BlockSpec((None, tm, d), lambda bi, i: (bi, i, 0)),
        compiler_params=_params("parallel", "parallel"),
        name="ffn",
    )(x, mod_l, gains, w1, w3, w2)


def _rope128(t, cos, s_lo, s_hi):
    half = ROT_DIM // 2
    return t * cos + pltpu.roll(t, LANES - half, 1) * s_lo + pltpu.roll(t, half, 1) * s_hi


def _dot_nt(a, b):
    return lax.dot_general(a, b, (((1,), (1,)), ((), ())), preferred_element_type=F32)


def _rope_rows(t, cos_t, sin_t):
    half = ROT_DIM // 2
    parts = []
    for base in range(0, t.shape[0], HEAD_DIM):
        t1, t2 = t[base:base + half], t[base + half:base + ROT_DIM]
        parts += [t1 * cos_t - t2 * sin_t, t2 * cos_t + t1 * sin_t, t[base + ROT_DIM:base + HEAD_DIM]]
    return jnp.concatenate(parts, axis=0)


def _inproj_kernel(x_ref, mod_ref, g_ref, wqt_ref, wqit_ref, wkk_ref, wvt_ref, wwt_ref, wu_ref,
                   cos_ref, slo_ref, shi_ref, cost_ref, sint_ref,
                   qt_ref, qit_ref, kk_ref, vt_ref, wt_ref, u_ref, qn_ref, kn_ref):
    h = _modulated_norm(x_ref[...], g_ref[...], mod_ref, 3).astype(MXU_DTYPE)
    cos_t, sin_t = cost_ref[...], sint_ref[...]
    qt = (_rope_rows(_dot_nt(wqt_ref[...], h), cos_t, sin_t) * HEAD_DIM ** -0.5).astype(qt_ref.dtype)
    qt_ref[...] = qt
    qit_ref[...] = _rope_rows(_dot_nt(wqit_ref[...], h), cos_t, sin_t).astype(qit_ref.dtype)
    kk = _rope128(_dot(h, wkk_ref[...]), cos_ref[...], slo_ref[...], shi_ref[...]).astype(kk_ref.dtype)
    kk_ref[...] = kk
    vt = _dot_nt(wvt_ref[...], h)
    row = lax.broadcasted_iota(jnp.int32, vt.shape, 0)
    vt_ref[...] = jnp.where(row == HEAD_DIM, 1.0, vt).astype(vt_ref.dtype)
    wt_ref[...] = _dot_nt(wwt_ref[...], h)
    u_ref[...] = _dot(h, wu_ref[...]).astype(u_ref.dtype)
    q2 = qt.astype(F32) * qt.astype(F32)
    qn_ref[...] = jnp.max(jnp.concatenate(
        [jnp.sum(q2[b0:b0 + HEAD_DIM], axis=0, keepdims=True) for b0 in range(0, q2.shape[0], HEAD_DIM)], axis=0),
        keepdims=True)
    k2 = kk.astype(F32) * kk.astype(F32)
    lane = lax.broadcasted_iota(jnp.int32, k2.shape, 1)
    kn_ref[...] = jnp.max(jnp.sum(jnp.where(lane < HEAD_DIM, k2, 0.0), axis=1, keepdims=True), keepdims=True)


def _inproj_call(x, mod_l, g, wqt, wqit, wkk, wvt, wwt, wu, tables, layer):
    b, s, d = x.shape
    tm = min(ROW_TILE, s)
    n_t = s // tm
    row = lambda w: pl.BlockSpec((None, tm, w), lambda bi, i: (bi, i, 0))
    col = lambda hgt: pl.BlockSpec((None, hgt, tm), lambda bi, i: (bi, 0, i))
    wres = lambda r, c: _resident((None, r, c), lambda bi, i: (layer, 0, 0))
    one = pl.BlockSpec((None, None, 1, 1), lambda bi, i: (bi, i, 0, 0))
    ssm_w = wu.shape[2]
    return pl.pallas_call(
        _inproj_kernel,
        out_shape=(jax.ShapeDtypeStruct((b, ATTN_W, s), MXU_DTYPE),
                   jax.ShapeDtypeStruct((b, IDX_HEADS * IDX_DIM, s), MXU_DTYPE),
                   jax.ShapeDtypeStruct((b, s, LANES), MXU_DTYPE),
                   jax.ShapeDtypeStruct((b, n_t, VT_ROWS, tm), MXU_DTYPE),
                   jax.ShapeDtypeStruct((b, 8, s), F32),
                   jax.ShapeDtypeStruct((b, s, ssm_w), MXU_DTYPE),
                   jax.ShapeDtypeStruct((b, n_t, 1, 1), F32),
                   jax.ShapeDtypeStruct((b, n_t, 1, 1), F32)),
        grid=(b, n_t),
        in_specs=[row(d),
                  pl.BlockSpec((None, N_MOD, d), lambda bi, i: (bi, 0, 0)),
                  pl.BlockSpec((1, d), lambda bi, i: (0, 0)),
                  wres(ATTN_W, d), wres(IDX_HEADS * IDX_DIM, d), wres(d, LANES), wres(VT_ROWS, d), wres(8, d),
                  wres(d, ssm_w),
                  row(LANES), row(LANES), row(LANES), col(ROT_DIM // 2), col(ROT_DIM // 2)],
        out_specs=(col(ATTN_W), col(IDX_HEADS * IDX_DIM), row(LANES),
                   pl.BlockSpec((None, None, VT_ROWS, tm), lambda bi, i: (bi, i, 0, 0)),
                   col(8), row(ssm_w), one, one),
        compiler_params=_params("parallel", "parallel"),
        name="mixer_inproj",
    )(x, mod_l, g.reshape(1, d), wqt, wqit, wkk, wvt, wwt, wu, *tables)


def _rope_tables(positions):
    inv_freq = 1.0 / (ROPE_THETA ** (jnp.arange(0, ROT_DIM, 2, dtype=F32) / ROT_DIM))
    ang = positions.astype(F32)[..., None] * inv_freq
    cos, sin = jnp.cos(ang), jnp.sin(ang)
    half = ROT_DIM // 2
    rest = HEAD_DIM - ROT_DIM
    pad = lambda *parts: jnp.tile(jnp.concatenate(parts, axis=-1), (1, 1, LANES // HEAD_DIM))
    zeros = lambda n: jnp.zeros(cos.shape[:-1] + (n,), F32)
    ones = jnp.ones(cos.shape[:-1] + (rest,), F32)
    return (pad(cos, cos, ones),
            pad(-sin, zeros(half), zeros(rest)),
            pad(zeros(half), sin, zeros(rest)),
            cos.transpose(0, 2, 1), sin.transpose(0, 2, 1))


def _dsa_kernel(qt_ref, qit_ref, wt_ref, qn_ref, kn_ref, kk_ref, vt_ref, o_ref,
                key_ref, bias_ref, acc_ref, *, topk):
    tq = qt_ref.shape[1]
    tk = kk_ref.shape[1]
    i = pl.program_id(1)
    n_tiles = ((i + 1) * tq + tk - 1) // tk
    col = lax.broadcasted_iota(jnp.int32, (1, tq), 1) + i * tq
    q_lim = (col // CHUNK + 1) * CHUNK
    key_pos = lax.broadcasted_iota(jnp.int32, (tk, tq), 0)
    zeros_half = jnp.zeros((HEAD_DIM, tq), MXU_DTYPE)

    qit = qit_ref[...]
    wqi = jnp.concatenate([jnp.concatenate([zeros_half, qit[h * IDX_DIM:(h + 1) * IDX_DIM, :]], axis=0)
                           for h in range(IDX_HEADS)], axis=1)
    w = wt_ref[...] * (IDX_DIM ** -0.5 * IDX_HEADS ** -0.5)
    w_row = jnp.concatenate([w[h:h + 1, :] for h in range(IDX_HEADS)], axis=1)

    def write_keys(t, masked):
        rel = jnp.maximum(_dot(kk_ref[t], wqi), 0.0) * w_row
        s = rel[:, :tq]
        for h in range(1, IDX_HEADS):
            s = s + rel[:, h * tq:(h + 1) * tq]
        s = s + 0.0
        bits = pltpu.bitcast(s, jnp.int32)
        key = jnp.where(bits < 0, bits ^ jnp.int32(0x7FFFFFFF), bits)
        if masked:
            key = jnp.where(key_pos + t * tk < q_lim, key, jnp.int32(INT_MIN))
        key_ref[t] = key

    def score_body(t, carry):
        write_keys(t, False)
        return carry

    lax.fori_loop(0, n_tiles - 1, score_body, 0)
    write_keys(n_tiles - 1, True)

    def count(thr, strict):
        thr_b = jnp.broadcast_to(thr, (8, tq))

        def body(t, accs):
            accs = list(accs)
            for r in range(0, tk, 8):
                key = key_ref[t, r:r + 8, :]
                hit = jnp.where(key > thr_b if strict else key >= thr_b, 1.0, 0.0)
                accs[(r // 8) % COUNT_CHAINS] = accs[(r // 8) % COUNT_CHAINS] + hit
            return tuple(accs)

        accs = lax.fori_loop(0, n_tiles, body, tuple(jnp.zeros((8, tq), F32) for _ in range(COUNT_CHAINS)))
        total = accs[0]
        for a in accs[1:]:
            total = total + a
        return jnp.sum(total, axis=0, keepdims=True)

    k_f = float(topk)
    n0 = count(jnp.zeros((1, tq), jnp.int32), False)
    nonneg = n0 >= k_f
    tau0 = jnp.where(nonneg, jnp.int32(0), jnp.int32(INT_MIN))
    cnt0 = jnp.where(nonneg, n0, (n_tiles * tk).astype(F32))

    def refine(it, tau, cnt):
        trial = tau | lax.shift_left(jnp.int32(1), 30 - it)
        c = count(trial, False)
        ok = c >= k_f
        return jnp.where(ok, trial, tau), jnp.where(ok, c, cnt)

    tau, cnt = lax.fori_loop(0, TAU_PROBE_BIT, lambda it, tc: refine(it, *tc), (tau0, cnt0))
    locked = count(tau + 1, False) < k_f

    def unsettled(tau_cnt):
        return jnp.max(jnp.where(locked | (tau_cnt[1] == k_f), 0.0, 1.0))

    def tail_cond(carry):
        it, _, _, open_rows = carry
        return jnp.logical_and(it < 31, open_rows > 0.0)

    def tail_body(carry):
        it, tau, cnt, _ = carry
        for step in range(TAU_TAIL_STEPS):
            tau, cnt = refine(it + step, tau, cnt)
        return it + TAU_TAIL_STEPS, tau, cnt, unsettled((tau, cnt))

    _, tau, _, _ = lax.while_loop(tail_cond, tail_body,
                                  (jnp.int32(TAU_PROBE_BIT), tau, cnt, unsettled((tau, cnt))))
    need = k_f - count(tau, True)

    lower = (lax.broadcasted_iota(jnp.int32, (tk, tk), 1)
             <= lax.broadcasted_iota(jnp.int32, (tk, tk), 0)).astype(MXU_DTYPE)

    def bias_body(t, ties_before):
        key = key_ref[t]
        tie = jnp.where(key == tau, 1.0, 0.0)
        rank = ties_before + _dot(lower, tie.astype(MXU_DTYPE))
        sel = (key > tau) | ((key == tau) & (rank <= need))
        sel = sel & (key > jnp.int32(INT_MIN))
        bias_ref[t] = jnp.where(sel, 0.0, NEG_BIAS)
        return rank[tk - 1:tk, :]

    lax.fori_loop(0, n_tiles, bias_body, jnp.zeros((1, tq), F32))

    qt = qt_ref[...]
    wq = jnp.concatenate([jnp.concatenate([qt[h * HEAD_DIM:(h + 1) * HEAD_DIM, :], zeros_half], axis=0)
                          for h in range(N_HEADS)], axis=1)

    def logits(t):
        return _dot(kk_ref[t], wq) + jnp.concatenate([bias_ref[t]] * N_HEADS, axis=1)

    safe = jnp.max(qn_ref[...] * kn_ref[...]) <= LOGIT_SAFE * LOGIT_SAFE

    def exact_max():
        def max_body(t, m):
            s = logits(t)
            part = s[:8, :]
            for r in range(8, tk, 8):
                part = jnp.maximum(part, s[r:r + 8, :])
            return jnp.maximum(m, part)
        m = lax.fori_loop(0, n_tiles, max_body, jnp.full((8, N_HEADS * tq), NEG_BIAS, F32))
        return jnp.max(m, axis=0, keepdims=True)

    acc_ref[...] = jnp.zeros(acc_ref.shape, F32)

    def accumulate(offset):
        def att_body(t, carry):
            s = logits(t) if offset is None else logits(t) - offset
            acc_ref[...] += _dot(vt_ref[t], jnp.exp(s).astype(MXU_DTYPE))
            return carry

        lax.fori_loop(0, n_tiles, att_body, 0)

    @pl.when(safe)
    def _():
        accumulate(None)

    @pl.when(jnp.logical_not(safe))
    def _():
        accumulate(exact_max())

    acc = acc_ref[...]
    o_ref[...] = (acc[:HEAD_DIM, :] / acc[HEAD_DIM:HEAD_DIM + 1, :]).astype(o_ref.dtype)


def _dsa_call(qt, qit, wt, qn2, kn2, kk, vt):
    b, _, s = qt.shape
    tq = min(Q_TILE, s)
    tk = vt.shape[3]
    n_kt = s // tk
    n_q = s // tq
    topk = min(TOPK_MAX, s // 4)
    assert tk % tq == 0 and s % (tq * qn2.shape[1]) == 0
    q_per_norm_tile = s // qn2.shape[1] // tq
    kk_t = kk.reshape(b, n_kt, tk, LANES)
    kn2 = jnp.max(kn2, axis=1)
    per_batch = lambda shape: pl.BlockSpec((None,) + shape, lambda bi, i: (bi, 0, 0, 0))
    qcol = lambda height: pl.BlockSpec((None, height, tq), lambda bi, i: (bi, 0, i))
    out_t = pl.pallas_call(
        functools.partial(_dsa_kernel, topk=topk),
        out_shape=jax.ShapeDtypeStruct((b, n_q, HEAD_DIM, N_HEADS * tq), MXU_DTYPE),
        grid=(b, n_q),
        in_specs=[qcol(ATTN_W), qcol(IDX_HEADS * IDX_DIM), qcol(8),
                  pl.BlockSpec((None, None, 1, 1), lambda bi, i: (bi, i // q_per_norm_tile, 0, 0)),
                  pl.BlockSpec((None, 1, 1), lambda bi, i: (bi, 0, 0)),
                  per_batch((n_kt, tk, LANES)), per_batch((n_kt, VT_ROWS, tk))],
        out_specs=pl.BlockSpec((None, None, HEAD_DIM, N_HEADS * tq), lambda bi, i: (bi, i, 0, 0)),
        scratch_shapes=[pltpu.VMEM((n_kt, tk, tq), jnp.int32),
                        pltpu.VMEM((n_kt, tk, tq), F32),
                        pltpu.VMEM((VT_ROWS, N_HEADS * tq), F32)],
        compiler_params=_params("parallel", "parallel"),
        name="dsa_attention",
    )(qt, qit, wt, qn2, kn2, kk_t, vt)
    out = out_t.reshape(b, n_q, HEAD_DIM, N_HEADS, tq).transpose(0, 1, 4, 3, 2)
    return out.reshape(b, s, ATTN_W)


def _ssm_kernel(u_ref, toep_ref, win_ref, wout_ref, apow_ref, y_ref, hloc_ref, hswap_ref, hprev_ref, *, batch):
    rows, width = u_ref.shape
    sub = toep_ref.shape[0]
    n_sub = width // sub
    rot = lambda h, r: apow_ref[r:r + 1, :] * h + apow_ref[r + 1:r + 2, :] * pltpu.roll(h, SSM_STATE, 1)
    h = jnp.zeros((rows, 2 * SSM_STATE), F32)
    local = []
    for j in range(n_sub):
        h = rot(h, 0) + _dot(u_ref[:, j * sub:(j + 1) * sub], win_ref[...])
        local.append(h.astype(MXU_DTYPE))
    hloc_ref[...] = h
    hswap_ref[...] = pltpu.roll(h, SSM_STATE, 1)
    a_same, a_swap = apow_ref[2:3, :], apow_ref[3:4, :]
    a_swap_rolled = pltpu.roll(a_swap, SSM_STATE, 1)
    h = jnp.zeros((batch, 2 * SSM_STATE), F32)
    h_sw = jnp.zeros((batch, 2 * SSM_STATE), F32)
    for c in range(rows // batch):
        hprev_ref[c * batch:(c + 1) * batch, :] = h
        h, h_sw = (a_same * h + a_swap * h_sw + hloc_ref[c * batch:(c + 1) * batch, :],
                   a_same * h_sw + a_swap_rolled * h + hswap_ref[c * batch:(c + 1) * batch, :])
    hprev = hprev_ref[...].astype(MXU_DTYPE)
    for j in range(n_sub):
        acc = _dot(hprev, wout_ref[:, j * sub:(j + 1) * sub]) + _dot(u_ref[:, j * sub:(j + 1) * sub], toep_ref[...])
        if j > 0:
            acc = acc + _dot(local[j - 1], wout_ref[:, :sub])
        y_ref[:, j * sub:(j + 1) * sub] = acc.astype(y_ref.dtype)


def _ssm_call(u_g, toep, win, wout, apow, layer, batch):
    g, rows, width = u_g.shape
    sub = toep.shape[2]
    per_group = lambda *shape: pl.BlockSpec((None, None) + shape, lambda gi: (layer, gi) + (0,) * len(shape))
    return pl.pallas_call(
        functools.partial(_ssm_kernel, batch=batch),
        out_shape=jax.ShapeDtypeStruct((g, rows, width), MXU_DTYPE),
        grid=(g,),
        in_specs=[pl.BlockSpec((None, rows, width), lambda gi: (gi, 0, 0)),
                  per_group(sub, sub), per_group(sub, 2 * SSM_STATE),
                  per_group(2 * SSM_STATE, width), per_group(4, 2 * SSM_STATE)],
        out_specs=pl.BlockSpec((None, rows, width), lambda gi: (gi, 0, 0)),
        scratch_shapes=[pltpu.VMEM((rows, 2 * SSM_STATE), F32)] * 3,
        compiler_params=_params("parallel"),
        name="s5_scan",
    )(u_g, toep, win, wout, apow)


def _ssm_operators(a_re, a_im, log_dt, b_re, b_im, c_re, c_im, d_skip, chunk):
    cmul = lambda xr, xi, yr, yi: (xr * yr - xi * yi, xr * yi + xi * yr)
    dt = jnp.exp(log_dt)[..., None]
    lam_re, lam_im = a_re * dt, a_im * dt
    lags = jnp.arange(chunk + 1, dtype=F32)[None, None, :, None]
    mag = jnp.exp(lam_re[:, :, None, :] * lags)
    ang = lam_im[:, :, None, :] * lags
    ap_re, ap_im = mag * jnp.cos(ang), mag * jnp.sin(ang)
    num_re, num_im = ap_re[:, :, 1] - 1.0, ap_im[:, :, 1]
    den = a_re * a_re + a_im * a_im
    coef_re, coef_im = (num_re * a_re + num_im * a_im) / den, (num_im * a_re - num_re * a_im) / den
    bb_re, bb_im = cmul(coef_re[..., None], coef_im[..., None], b_re, b_im)
    depth, g, p = a_re.shape
    gc = b_re.shape[-1]
    sub = SSM_SUB
    ca_re, ca_im = cmul(c_re[:, :, None], c_im[:, :, None], ap_re[:, :, :sub, None, :], ap_im[:, :, :sub, None, :])
    taps = (jnp.einsum('dgtop,dgpi->dgtoi', ca_re, bb_re, precision=lax.Precision.HIGHEST)
            - jnp.einsum('dgtop,dgpi->dgtoi', ca_im, bb_im, precision=lax.Precision.HIGHEST))
    taps = taps.at[:, :, 0].add(d_skip[..., None] * jnp.eye(gc, dtype=F32))
    taps = jnp.concatenate([jnp.zeros((depth, g, sub - 1, gc, gc), F32), taps], axis=2)
    lag_idx = jnp.arange(sub)[None, :] - jnp.arange(sub)[:, None] + (sub - 1)
    toep = taps[:, :, lag_idx]
    toep = toep.transpose(0, 1, 2, 5, 3, 4).reshape(depth, g, sub * gc, sub * gc)
    rev_re, rev_im = ap_re[:, :, sub - 1::-1][:, :, :sub], ap_im[:, :, sub - 1::-1][:, :, :sub]
    wi_re, wi_im = cmul(rev_re[..., None], rev_im[..., None], bb_re[:, :, None], bb_im[:, :, None])
    w_in = jnp.concatenate([wi_re, wi_im], axis=3)
    w_in = w_in.transpose(0, 1, 2, 4, 3).reshape(depth, g, sub * gc, 2 * p)
    wo_re, wo_im = cmul(c_re[:, :, None], c_im[:, :, None], ap_re[:, :, 1:, None, :], ap_im[:, :, 1:, None, :])
    w_out = jnp.concatenate([wo_re, -wo_im], axis=4)
    w_out = w_out.transpose(0, 1, 4, 2, 3).reshape(depth, g, 2 * p, chunk * gc)
    rot_rows = lambda lag: [jnp.concatenate([ap_re[:, :, lag], ap_re[:, :, lag]], -1),
                            jnp.concatenate([-ap_im[:, :, lag], ap_im[:, :, lag]], -1)]
    apow = jnp.stack(rot_rows(sub) + rot_rows(chunk), axis=2)
    return toep.astype(MXU_DTYPE), w_in.astype(MXU_DTYPE), w_out.astype(MXU_DTYPE), apow


def _outproj_kernel(x_ref, attn_ref, y_ref, mod_ref, ag_ref, sg_ref, gw_ref, gb_ref, woa_ref, wos_ref, o_ref):
    a = _rms(attn_ref[...].astype(F32), ag_ref[...]).astype(MXU_DTYPE)
    y = y_ref[...].astype(F32)
    y = 0.5 * y * (1.0 + jnp.tanh(math.sqrt(2.0 / math.pi) * (y + 0.044715 * (y * y * y))))
    y = y * jax.nn.sigmoid(_dot(y.astype(MXU_DTYPE), gw_ref[...]) + gb_ref[...])
    y = _rms(y, sg_ref[...]).astype(MXU_DTYPE)
    mixed = _dot(a, woa_ref[...]) + _dot(y, wos_ref[...])
    o_ref[...] = x_ref[...] + mod_ref[5:6, :] * mixed


def _outproj_call(x, attn, y, mod_l, attn_gain, ssm_gain, glu_w, glu_b, wo_a, wo_s, layer):
    b, s, d = x.shape
    tm = min(ROW_TILE, s)
    aw, sw = attn.shape[2], y.shape[2]
    row = lambda w: pl.BlockSpec((None, tm, w), lambda bi, i: (bi, i, 0))
    vec = lambda w: pl.BlockSpec((None, 1, w), lambda bi, i: (layer, 0, 0))
    wres = lambda r, c: _resident((None, r, c), lambda bi, i: (layer, 0, 0))
    return pl.pallas_call(
        _outproj_kernel,
        out_shape=jax.ShapeDtypeStruct(x.shape, F32),
        grid=(b, s // tm),
        in_specs=[row(d), row(aw), row(sw),
                  pl.BlockSpec((None, N_MOD, d), lambda bi, i: (bi, 0, 0)),
                  vec(aw), vec(sw), wres(sw, sw), vec(sw), wres(aw, d), wres(sw, d)],
        out_specs=row(d),
        compiler_params=_params("parallel", "parallel"),
        name="mixer_outproj",
    )(x, attn, y, mod_l, attn_gain[:, None, :], ssm_gain[:, None, :], glu_w, glu_b[:, None, :], wo_a, wo_s)


def kernel(x, c, positions, mod_w, mod_b, norm_g, ffn1_w1, ffn1_w3, ffn1_w2, ffn2_w1, ffn2_w3, ffn2_w2, w_in, w_out, attn_gain, ssm_gain, ssm_a_re, ssm_a_im, ssm_log_dt, ssm_b_re, ssm_b_im, ssm_c_re, ssm_c_im, ssm_d, glu_w, glu_b, final_g):
    b, s, d = x.shape
    depth = mod_w.shape[0]
    ssm_w = d - ATTN_W
    groups = ssm_w // SSM_GROUP
    chunk = min(SCAN_L, s)
    n_chunks = s // chunk
    cast = lambda a: a.astype(MXU_DTYPE)

    mod = _mod_call(c, mod_w, mod_b)
    tables = _rope_tables(positions)
    ffn1 = (cast(ffn1_w1), cast(ffn1_w3), cast(ffn1_w2))
    ffn2 = (cast(ffn2_w1), cast(ffn2_w3), cast(ffn2_w2))
    o_k = ATTN_W
    o_v = o_k + HEAD_DIM
    o_qi = o_v + HEAD_DIM
    o_ki = o_qi + IDX_HEADS * IDX_DIM
    o_wi = o_ki + IDX_DIM
    o_u = o_wi + IDX_HEADS
    w_in_t = w_in.transpose(0, 2, 1)
    pad_rows = lambda a, rows: jnp.concatenate([a, jnp.zeros((depth, rows - a.shape[1], d), a.dtype)], axis=1)
    wqt = cast(w_in_t[:, :o_k])
    wqit = cast(w_in_t[:, o_qi:o_ki])
    wkk = cast(jnp.concatenate([w_in[:, :, o_k:o_v], w_in[:, :, o_ki:o_wi]], axis=2))
    wvt = cast(pad_rows(w_in_t[:, o_v:o_qi], VT_ROWS))
    wwt = cast(pad_rows(w_in_t[:, o_wi:o_u], 8))
    wu = cast(w_in[:, :, o_u:])
    wo_a, wo_s = cast(w_out[:, :ATTN_W]), cast(w_out[:, ATTN_W:])
    glu_wc = cast(glu_w)
    toep, s_win, s_wout, apow = _ssm_operators(ssm_a_re, ssm_a_im, ssm_log_dt, ssm_b_re, ssm_b_im,
                                               ssm_c_re, ssm_c_im, ssm_d, chunk)

    for l in range(depth):
        x = _ffn_call(x, mod[l], norm_g[l, 0], *ffn1, l, 0)
        qt, qit, kk, vt, wt, u, qn2, kn2 = _inproj_call(x, mod[l], norm_g[l, 1], wqt, wqit, wkk, wvt, wwt, wu,
                                                        tables, l)
        attn = _dsa_call(qt, qit, wt, qn2, kn2, kk, vt)
        u_g = u.reshape(b, n_chunks, chunk, groups, SSM_GROUP).transpose(3, 1, 0, 2, 4)
        u_g = u_g.reshape(groups, n_chunks * b, chunk * SSM_GROUP)
        y_g = _ssm_call(u_g, toep, s_win, s_wout, apow, l, b)
        y = y_g.reshape(groups, n_chunks, b, chunk, SSM_GROUP).transpose(2, 1, 3, 0, 4).reshape(b, s, ssm_w)
        x = _outproj_call(x, attn, y, mod[l], attn_gain, ssm_gain, glu_wc, glu_b, wo_a, wo_s, l)
        x = _ffn_call(x, mod[l], norm_g[l, 2], *ffn2, l, 6, final_g=final_g if l == depth - 1 else None)
    return x
```

```python
import functools
import math

import jax
import jax.numpy as jnp
from jax import lax
from jax.experimental import pallas as pl
from jax.experimental.pallas import tpu as pltpu

F32 = jnp.float32
MXU_DTYPE = jnp.bfloat16

CHUNK = 64
N_HEADS = 8
HEAD_DIM = 64
ATTN_W = N_HEADS * HEAD_DIM
IDX_HEADS = 4
IDX_DIM = 64
TOPK_MAX = 256
SSM_GROUP = 16
SSM_STATE = 64
SCAN_L = 128
ROPE_THETA = 500000.0
ROT_DIM = HEAD_DIM // 4
EPS = 1e-6
N_MOD = 9

LANES = 128
MXU_WIDTH = 256
BF16_SUBLANES = 16
VT_ROWS = -(-(HEAD_DIM + 1) // BF16_SUBLANES) * BF16_SUBLANES
VMEM_LIMIT_BYTES = 56 * 1024 * 1024

INT_MIN = -(2 ** 31)
NEG_BIAS = -1e30

ROW_TILE = 512
Q_TILE = 512
LOGIT_SAFE = 40.0
COUNT_CHAINS = 4
TAU_PROBE_BIT = 23
TAU_TAIL_STEPS = 2
assert (31 - TAU_PROBE_BIT) % TAU_TAIL_STEPS == 0
SSM_SUB = 16


def _params(*sem):
    return pltpu.CompilerParams(dimension_semantics=sem, vmem_limit_bytes=VMEM_LIMIT_BYTES)


def _dot(a, b):
    return jnp.dot(a, b, preferred_element_type=F32)


def _rms(x, g):
    return x * lax.rsqrt(jnp.mean(x * x, axis=-1, keepdims=True) + EPS) * g


def _modulated_norm(x, g, mod_ref, base):
    shift = mod_ref[base:base + 1, :]
    scale = mod_ref[base + 1:base + 2, :]
    return _rms(x, g) * (1.0 + scale) + shift


def _mod_kernel(c_ref, w_ref, b_ref, o_ref):
    c = c_ref[...]
    o_ref[...] = _dot(c * jax.nn.sigmoid(c), w_ref[...]) + b_ref[...]


def _mod_call(c, mod_w, mod_b):
    depth, d, nd = mod_w.shape
    b = c.shape[0]
    rows = -(-b // 8) * 8
    c_pad = jnp.zeros((rows, d), F32).at[:b].set(c)
    out = pl.pallas_call(
        _mod_kernel,
        out_shape=jax.ShapeDtypeStruct((depth, rows, nd), F32),
        grid=(depth, nd // d),
        in_specs=[pl.BlockSpec((rows, d), lambda l, j: (0, 0)),
                  pl.BlockSpec((None, d, d), lambda l, j: (l, 0, j)),
                  pl.BlockSpec((None, 1, d), lambda l, j: (l, 0, j))],
        out_specs=pl.BlockSpec((None, rows, d), lambda l, j: (l, 0, j)),
        compiler_params=_params("parallel", "parallel"),
        name="adaln_mod",
    )(c_pad, mod_w, mod_b.reshape(depth, 1, nd))
    return out[:, :b].reshape(depth, b, N_MOD, d)


def _ffn_kernel(x_ref, mod_ref, g_ref, w1_ref, w3_ref, w2_ref, o_ref, *, base, ff_cuts, final_norm):
    x = x_ref[...]
    h = _modulated_norm(x, g_ref[0:1, :], mod_ref, base).astype(MXU_DTYPE)
    acc = jnp.zeros(x.shape, F32)
    for c0, c1 in zip(ff_cuts[:-1], ff_cuts[1:]):
        a = _dot(h, w1_ref[:, c0:c1])
        b = _dot(h, w3_ref[:, c0:c1])
        t = (a * jax.nn.sigmoid(a) * b).astype(MXU_DTYPE)
        acc = acc + _dot(t, w2_ref[c0:c1, :])
    y = x + 0.5 * mod_ref[base + 2:base + 3, :] * acc
    o_ref[...] = _rms(y, g_ref[1:2, :]) if final_norm else y


def _resident(shape, index_map):
    return pl.BlockSpec(shape, index_map, pipeline_mode=pl.Buffered(1))


def _ffn_call(x, mod_l, g, w1, w3, w2, layer, base, final_g=None):
    b, s, d = x.shape
    gains = jnp.stack([g, g if final_g is None else final_g])
    ff = w1.shape[2]
    tm = min(ROW_TILE, s)
    mid = -(-ff // (2 * MXU_WIDTH)) * MXU_WIDTH
    ff_cuts = (0, mid, ff) if 0 < mid < ff else (0, ff)
    wspec_in = _resident((None, d, ff), lambda bi, i: (layer, 0, 0))
    wspec_out = _resident((None, ff, d), lambda bi, i: (layer, 0, 0))
    return pl.pallas_call(
        functools.partial(_ffn_kernel, base=base, ff_cuts=ff_cuts, final_norm=final_g is not None),
        out_shape=jax.ShapeDtypeStruct(x.shape, F32),
        grid=(b, s // tm),
        in_specs=[pl.BlockSpec((None, tm, d), lambda bi, i: (bi, i, 0)),
                  pl.BlockSpec((None, N_MOD, d), lambda bi, i: (bi, 0, 0)),
                  pl.BlockSpec((2, d), lambda bi, i: (0, 0)),
                  wspec_in, wspec_in, wspec_out],
        out_specs=pl.BlockSpec((None, tm, d), lambda bi, i: (bi, i, 0)),
        compiler_params=_params("parallel", "parallel"),
        name="ffn",
    )(x, mod_l, gains, w1, w3, w2)


def _rope128(t, cos, s_lo, s_hi):
    half = ROT_DIM // 2
    return t * cos + pltpu.roll(t, LANES - half, 1) * s_lo + pltpu.roll(t, half, 1) * s_hi


def _dot_nt(a, b):
    return lax.dot_general(a, b, (((1,), (1,)), ((), ())), preferred_element_type=F32)


def _rope_rows(t, cos_t, sin_t):
    half = ROT_DIM // 2
    parts = []
    for base in range(0, t.shape[0], HEAD_DIM):
        t1, t2 = t[base:base + half], t[base + half:base + ROT_DIM]
        parts += [t1 * cos_t - t2 * sin_t, t2 * cos_t + t1 * sin_t, t[base + ROT_DIM:base + HEAD_DIM]]
    return jnp.concatenate(parts, axis=0)


def _inproj_kernel(x_ref, mod_ref, g_ref, wqt_ref, wqit_ref, wkk_ref, wvt_ref, wwt_ref, wu_ref,
                   cos_ref, slo_ref, shi_ref, cost_ref, sint_ref,
                   qt_ref, qit_ref, kk_ref, vt_ref, wt_ref, u_ref, qn_ref, kn_ref):
    h = _modulated_norm(x_ref[...], g_ref[...], mod_ref, 3).astype(MXU_DTYPE)
    cos_t, sin_t = cost_ref[...], sint_ref[...]
    qt = (_rope_rows(_dot_nt(wqt_ref[...], h), cos_t, sin_t) * HEAD_DIM ** -0.5).astype(qt_ref.dtype)
    qt_ref[...] = qt
    qit_ref[...] = _rope_rows(_dot_nt(wqit_ref[...], h), cos_t, sin_t).astype(qit_ref.dtype)
    kk = _rope128(_dot(h, wkk_ref[...]), cos_ref[...], slo_ref[...], shi_ref[...]).astype(kk_ref.dtype)
    kk_ref[...] = kk
    vt = _dot_nt(wvt_ref[...], h)
    row = lax.broadcasted_iota(jnp.int32, vt.shape, 0)
    vt_ref[...] = jnp.where(row == HEAD_DIM, 1.0, vt).astype(vt_ref.dtype)
    wt_ref[...] = _dot_nt(wwt_ref[...], h)
    u_ref[...] = _dot(h, wu_ref[...]).astype(u_ref.dtype)
    q2 = qt.astype(F32) * qt.astype(F32)
    qn_ref[...] = jnp.max(jnp.concatenate(
        [jnp.sum(q2[b0:b0 + HEAD_DIM], axis=0, keepdims=True) for b0 in range(0, q2.shape[0], HEAD_DIM)], axis=0),
        keepdims=True)
    k2 = kk.astype(F32) * kk.astype(F32)
    lane = lax.broadcasted_iota(jnp.int32, k2.shape, 1)
    kn_ref[...] = jnp.max(jnp.sum(jnp.where(lane < HEAD_DIM, k2, 0.0), axis=1, keepdims=True), keepdims=True)


def _inproj_call(x, mod_l, g, wqt, wqit, wkk, wvt, wwt, wu, tables, layer):
    b, s, d = x.shape
    tm = min(ROW_TILE, s)
    n_t = s // tm
    row = lambda w: pl.BlockSpec((None, tm, w), lambda bi, i: (bi, i, 0))
    col = lambda hgt: pl.BlockSpec((None, hgt, tm), lambda bi, i: (bi, 0, i))
    wres = lambda r, c: _resident((None, r, c), lambda bi, i: (layer, 0, 0))
    one = pl.BlockSpec((None, None, 1, 1), lambda bi, i: (bi, i, 0, 0))
    ssm_w = wu.shape[2]
    return pl.pallas_call(
        _inproj_kernel,
        out_shape=(jax.ShapeDtypeStruct((b, ATTN_W, s), MXU_DTYPE),
                   jax.ShapeDtypeStruct((b, IDX_HEADS * IDX_DIM, s), MXU_DTYPE),
                   jax.ShapeDtypeStruct((b, s, LANES), MXU_DTYPE),
                   jax.ShapeDtypeStruct((b, n_t, VT_ROWS, tm), MXU_DTYPE),
                   jax.ShapeDtypeStruct((b, 8, s), F32),
                   jax.ShapeDtypeStruct((b, s, ssm_w), MXU_DTYPE),
                   jax.ShapeDtypeStruct((b, n_t, 1, 1), F32),
                   jax.ShapeDtypeStruct((b, n_t, 1, 1), F32)),
        grid=(b, n_t),
        in_specs=[row(d),
                  pl.BlockSpec((None, N_MOD, d), lambda bi, i: (bi, 0, 0)),
                  pl.BlockSpec((1, d), lambda bi, i: (0, 0)),
                  wres(ATTN_W, d), wres(IDX_HEADS * IDX_DIM, d), wres(d, LANES), wres(VT_ROWS, d), wres(8, d),
                  wres(d, ssm_w),
                  row(LANES), row(LANES), row(LANES), col(ROT_DIM // 2), col(ROT_DIM // 2)],
        out_specs=(col(ATTN_W), col(IDX_HEADS * IDX_DIM), row(LANES),
                   pl.BlockSpec((None, None, VT_ROWS, tm), lambda bi, i: (bi, i, 0, 0)),
                   col(8), row(ssm_w), one, one),
        compiler_params=_params("parallel", "parallel"),
        name="mixer_inproj",
    )(x, mod_l, g.reshape(1, d), wqt, wqit, wkk, wvt, wwt, wu, *tables)


def _rope_tables(positions):
    inv_freq = 1.0 / (ROPE_THETA ** (jnp.arange(0, ROT_DIM, 2, dtype=F32) / ROT_DIM))
    ang = positions.astype(F32)[..., None] * inv_freq
    cos, sin = jnp.cos(ang), jnp.sin(ang)
    half = ROT_DIM // 2
    rest = HEAD_DIM - ROT_DIM
    pad = lambda *parts: jnp.tile(jnp.concatenate(parts, axis=-1), (1, 1, LANES // HEAD_DIM))
    zeros = lambda n: jnp.zeros(cos.shape[:-1] + (n,), F32)
    ones = jnp.ones(cos.shape[:-1] + (rest,), F32)
    return (pad(cos, cos, ones),
            pad(-sin, zeros(half), zeros(rest)),
            pad(zeros(half), sin, zeros(rest)),
            cos.transpose(0, 2, 1), sin.transpose(0, 2, 1))


def _dsa_kernel(qt_ref, qit_ref, wt_ref, qn_ref, kn_ref, kk_ref, vt_ref, o_ref,
                key_ref, bias_ref, acc_ref, *, topk):
    tq = qt_ref.shape[1]
    tk = kk_ref.shape[1]
    i = pl.program_id(1)
    n_tiles = ((i + 1) * tq + tk - 1) // tk
    col = lax.broadcasted_iota(jnp.int32, (1, tq), 1) + i * tq
    q_lim = (col // CHUNK + 1) * CHUNK
    key_pos = lax.broadcasted_iota(jnp.int32, (tk, tq), 0)
    zeros_half = jnp.zeros((HEAD_DIM, tq), MXU_DTYPE)

    qit = qit_ref[...]
    wqi = jnp.concatenate([jnp.concatenate([zeros_half, qit[h * IDX_DIM:(h + 1) * IDX_DIM, :]], axis=0)
                           for h in range(IDX_HEADS)], axis=1)
    w = wt_ref[...] * (IDX_DIM ** -0.5 * IDX_HEADS ** -0.5)
    w_row = jnp.concatenate([w[h:h + 1, :] for h in range(IDX_HEADS)], axis=1)

    def write_keys(t, masked):
        rel = jnp.maximum(_dot(kk_ref[t], wqi), 0.0) * w_row
        s = rel[:, :tq]
        for h in range(1, IDX_HEADS):
            s = s + rel[:, h * tq:(h + 1) * tq]
        s = s + 0.0
        bits = pltpu.bitcast(s, jnp.int32)
        key = jnp.where(bits < 0, bits ^ jnp.int32(0x7FFFFFFF), bits)
        if masked:
            key = jnp.where(key_pos + t * tk < q_lim, key, jnp.int32(INT_MIN))
        key_ref[t] = key

    def score_body(t, carry):
        write_keys(t, False)
        return carry

    lax.fori_loop(0, n_tiles - 1, score_body, 0)
    write_keys(n_tiles - 1, True)

    def count(thr, strict):
        thr_b = jnp.broadcast_to(thr, (8, tq))

        def body(t, accs):
            accs = list(accs)
            for r in range(0, tk, 8):
                key = key_ref[t, r:r + 8, :]
                hit = jnp.where(key > thr_b if strict else key >= thr_b, 1.0, 0.0)
                accs[(r // 8) % COUNT_CHAINS] = accs[(r // 8) % COUNT_CHAINS] + hit
            return tuple(accs)

        accs = lax.fori_loop(0, n_tiles, body, tuple(jnp.zeros((8, tq), F32) for _ in range(COUNT_CHAINS)))
        total = accs[0]
        for a in accs[1:]:
            total = total + a
        return jnp.sum(total, axis=0, keepdims=True)

    k_f = float(topk)
    n0 = count(jnp.zeros((1, tq), jnp.int32), False)
    nonneg = n0 >= k_f
    tau0 = jnp.where(nonneg, jnp.int32(0), jnp.int32(INT_MIN))
    cnt0 = jnp.where(nonneg, n0, (n_tiles * tk).astype(F32))
    rej0 = jnp.where(nonneg, 0.0, n0)

    def refine(it, tau, cnt, rej):
        trial = tau | lax.shift_left(jnp.int32(1), 30 - it)
        c = count(trial, False)
        ok = c >= k_f
        return jnp.where(ok, trial, tau), jnp.where(ok, c, cnt), jnp.where(ok, rej, c)

    tau, cnt, rej = lax.fori_loop(0, TAU_PROBE_BIT, lambda it, st: refine(it, *st), (tau0, cnt0, rej0))
    above_probe = count(tau + 1, False)
    locked = above_probe < k_f

    def unsettled(cnt):
        return jnp.max(jnp.where(locked | (cnt == k_f), 0.0, 1.0))

    def tail_cond(carry):
        return jnp.logical_and(carry[0] < 31, carry[4] > 0.0)

    def tail_body(carry):
        it, tau, cnt, rej, _ = carry
        for step in range(TAU_TAIL_STEPS):
            tau, cnt, rej = refine(it + step, tau, cnt, rej)
        return it + TAU_TAIL_STEPS, tau, cnt, rej, unsettled(cnt)

    _, tau, cnt, rej, _ = lax.while_loop(tail_cond, tail_body,
                                         (jnp.int32(TAU_PROBE_BIT), tau, cnt, rej, unsettled(cnt)))
    above = jnp.where(locked, above_probe, rej)
    need = jnp.where(cnt == k_f, k_f, k_f - above)

    lower = (lax.broadcasted_iota(jnp.int32, (tk, tk), 1)
             <= lax.broadcasted_iota(jnp.int32, (tk, tk), 0)).astype(MXU_DTYPE)

    def bias_body(t, ties_before):
        key = key_ref[t]
        tie = jnp.where(key == tau, 1.0, 0.0)
        rank = ties_before + _dot(lower, tie.astype(MXU_DTYPE))
        sel = (key > tau) | ((key == tau) & (rank <= need))
        sel = sel & (key > jnp.int32(INT_MIN))
        bias_ref[t] = jnp.where(sel, 0.0, NEG_BIAS)
        return rank[tk - 1:tk, :]

    lax.fori_loop(0, n_tiles, bias_body, jnp.zeros((1, tq), F32))

    qt = qt_ref[...]
    wq = jnp.concatenate([jnp.concatenate([qt[h * HEAD_DIM:(h + 1) * HEAD_DIM, :], zeros_half], axis=0)
                          for h in range(N_HEADS)], axis=1)

    def logits(t):
        return _dot(kk_ref[t], wq) + jnp.concatenate([bias_ref[t]] * N_HEADS, axis=1)

    safe = jnp.max(qn_ref[...] * kn_ref[...]) <= LOGIT_SAFE * LOGIT_SAFE

    def exact_max():
        def max_body(t, m):
            s = logits(t)
            part = s[:8, :]
            for r in range(8, tk, 8):
                part = jnp.maximum(part, s[r:r + 8, :])
            return jnp.maximum(m, part)
        m = lax.fori_loop(0, n_tiles, max_body, jnp.full((8, N_HEADS * tq), NEG_BIAS, F32))
        return jnp.max(m, axis=0, keepdims=True)

    acc_ref[...] = jnp.zeros(acc_ref.shape, F32)

    def accumulate(offset):
        def att_body(t, carry):
            s = logits(t) if offset is None else logits(t) - offset
            acc_ref[...] += _dot(vt_ref[t], jnp.exp(s).astype(MXU_DTYPE))
            return carry

        lax.fori_loop(0, n_tiles, att_body, 0)

    @pl.when(safe)
    def _():
        accumulate(None)

    @pl.when(jnp.logical_not(safe))
    def _():
        accumulate(exact_max())

    acc = acc_ref[...]
    o_ref[...] = (acc[:HEAD_DIM, :] / acc[HEAD_DIM:HEAD_DIM + 1, :]).astype(o_ref.dtype)


def _dsa_call(qt, qit, wt, qn2, kn2, kk, vt):
    b, _, s = qt.shape
    tq = min(Q_TILE, s)
    tk = vt.shape[3]
    n_kt = s // tk
    n_q = s // tq
    topk = min(TOPK_MAX, s // 4)
    assert tk % tq == 0 and s % (tq * qn2.shape[1]) == 0
    q_per_norm_tile = s // qn2.shape[1] // tq
    kk_t = kk.reshape(b, n_kt, tk, LANES)
    kn2 = jnp.max(kn2, axis=1)
    per_batch = lambda shape: pl.BlockSpec((None,) + shape, lambda bi, i: (bi, 0, 0, 0))
    qcol = lambda height: pl.BlockSpec((None, height, tq), lambda bi, i: (bi, 0, i))
    out_t = pl.pallas_call(
        functools.partial(_dsa_kernel, topk=topk),
        out_shape=jax.ShapeDtypeStruct((b, n_q, HEAD_DIM, N_HEADS * tq), MXU_DTYPE),
        grid=(b, n_q),
        in_specs=[qcol(ATTN_W), qcol(IDX_HEADS * IDX_DIM), qcol(8),
                  pl.BlockSpec((None, None, 1, 1), lambda bi, i: (bi, i // q_per_norm_tile, 0, 0)),
                  pl.BlockSpec((None, 1, 1), lambda bi, i: (bi, 0, 0)),
                  per_batch((n_kt, tk, LANES)), per_batch((n_kt, VT_ROWS, tk))],
        out_specs=pl.BlockSpec((None, None, HEAD_DIM, N_HEADS * tq), lambda bi, i: (bi, i, 0, 0)),
        scratch_shapes=[pltpu.VMEM((n_kt, tk, tq), jnp.int32),
                        pltpu.VMEM((n_kt, tk, tq), F32),
                        pltpu.VMEM((VT_ROWS, N_HEADS * tq), F32)],
        compiler_params=_params("parallel", "parallel"),
        name="dsa_attention",
    )(qt, qit, wt, qn2, kn2, kk_t, vt)
    out = out_t.reshape(b, n_q, HEAD_DIM, N_HEADS, tq).transpose(0, 1, 4, 3, 2)
    return out.reshape(b, s, ATTN_W)


def _ssm_kernel(u_ref, toep_ref, win_ref, wout_ref, apow_ref, y_ref, hloc_ref, hswap_ref, hprev_ref, *, batch):
    rows, width = u_ref.shape
    sub = toep_ref.shape[0]
    n_sub = width // sub
    rot = lambda h, r: apow_ref[r:r + 1, :] * h + apow_ref[r + 1:r + 2, :] * pltpu.roll(h, SSM_STATE, 1)
    h = jnp.zeros((rows, 2 * SSM_STATE), F32)
    local = []
    for j in range(n_sub):
        h = rot(h, 0) + _dot(u_ref[:, j * sub:(j + 1) * sub], win_ref[...])
        local.append(h.astype(MXU_DTYPE))
    hloc_ref[...] = h
    hswap_ref[...] = pltpu.roll(h, SSM_STATE, 1)
    a_same, a_swap = apow_ref[2:3, :], apow_ref[3:4, :]
    a_swap_rolled = pltpu.roll(a_swap, SSM_STATE, 1)
    h = jnp.zeros((batch, 2 * SSM_STATE), F32)
    h_sw = jnp.zeros((batch, 2 * SSM_STATE), F32)
    for c in range(rows // batch):
        hprev_ref[c * batch:(c + 1) * batch, :] = h
        h, h_sw = (a_same * h + a_swap * h_sw + hloc_ref[c * batch:(c + 1) * batch, :],
                   a_same * h_sw + a_swap_rolled * h + hswap_ref[c * batch:(c + 1) * batch, :])
    hprev = hprev_ref[...].astype(MXU_DTYPE)
    for j in range(n_sub):
        acc = _dot(hprev, wout_ref[:, j * sub:(j + 1) * sub]) + _dot(u_ref[:, j * sub:(j + 1) * sub], toep_ref[...])
        if j > 0:
            acc = acc + _dot(local[j - 1], wout_ref[:, :sub])
        y_ref[:, j * sub:(j + 1) * sub] = acc.astype(y_ref.dtype)


def _ssm_call(u_g, toep, win, wout, apow, layer, batch):
    g, rows, width = u_g.shape
    sub = toep.shape[2]
    per_group = lambda *shape: pl.BlockSpec((None, None) + shape, lambda gi: (layer, gi) + (0,) * len(shape))
    return pl.pallas_call(
        functools.partial(_ssm_kernel, batch=batch),
        out_shape=jax.ShapeDtypeStruct((g, rows, width), MXU_DTYPE),
        grid=(g,),
        in_specs=[pl.BlockSpec((None, rows, width), lambda gi: (gi, 0, 0)),
                  per_group(sub, sub), per_group(sub, 2 * SSM_STATE),
                  per_group(2 * SSM_STATE, width), per_group(4, 2 * SSM_STATE)],
        out_specs=pl.BlockSpec((None, rows, width), lambda gi: (gi, 0, 0)),
        scratch_shapes=[pltpu.VMEM((rows, 2 * SSM_STATE), F32)] * 3,
        compiler_params=_params("parallel"),
        name="s5_scan",
    )(u_g, toep, win, wout, apow)


def _ssm_operators(a_re, a_im, log_dt, b_re, b_im, c_re, c_im, d_skip, chunk):
    cmul = lambda xr, xi, yr, yi: (xr * yr - xi * yi, xr * yi + xi * yr)
    dt = jnp.exp(log_dt)[..., None]
    lam_re, lam_im = a_re * dt, a_im * dt
    lags = jnp.arange(chunk + 1, dtype=F32)[None, None, :, None]
    mag = jnp.exp(lam_re[:, :, None, :] * lags)
    ang = lam_im[:, :, None, :] * lags
    ap_re, ap_im = mag * jnp.cos(ang), mag * jnp.sin(ang)
    num_re, num_im = ap_re[:, :, 1] - 1.0, ap_im[:, :, 1]
    den = a_re * a_re + a_im * a_im
    coef_re, coef_im = (num_re * a_re + num_im * a_im) / den, (num_im * a_re - num_re * a_im) / den
    bb_re, bb_im = cmul(coef_re[..., None], coef_im[..., None], b_re, b_im)
    depth, g, p = a_re.shape
    gc = b_re.shape[-1]
    sub = SSM_SUB
    ca_re, ca_im = cmul(c_re[:, :, None], c_im[:, :, None], ap_re[:, :, :sub, None, :], ap_im[:, :, :sub, None, :])
    taps = (jnp.einsum('dgtop,dgpi->dgtoi', ca_re, bb_re, precision=lax.Precision.HIGHEST)
            - jnp.einsum('dgtop,dgpi->dgtoi', ca_im, bb_im, precision=lax.Precision.HIGHEST))
    taps = taps.at[:, :, 0].add(d_skip[..., None] * jnp.eye(gc, dtype=F32))
    taps = jnp.concatenate([jnp.zeros((depth, g, sub - 1, gc, gc), F32), taps], axis=2)
    lag_idx = jnp.arange(sub)[None, :] - jnp.arange(sub)[:, None] + (sub - 1)
    toep = taps[:, :, lag_idx]
    toep = toep.transpose(0, 1, 2, 5, 3, 4).reshape(depth, g, sub * gc, sub * gc)
    rev_re, rev_im = ap_re[:, :, sub - 1::-1][:, :, :sub], ap_im[:, :, sub - 1::-1][:, :, :sub]
    wi_re, wi_im = cmul(rev_re[..., None], rev_im[..., None], bb_re[:, :, None], bb_im[:, :, None])
    w_in = jnp.concatenate([wi_re, wi_im], axis=3)
    w_in = w_in.transpose(0, 1, 2, 4, 3).reshape(depth, g, sub * gc, 2 * p)
    wo_re, wo_im = cmul(c_re[:, :, None], c_im[:, :, None], ap_re[:, :, 1:, None, :], ap_im[:, :, 1:, None, :])
    w_out = jnp.concatenate([wo_re, -wo_im], axis=4)
    w_out = w_out.transpose(0, 1, 4, 2, 3).reshape(depth, g, 2 * p, chunk * gc)
    rot_rows = lambda lag: [jnp.concatenate([ap_re[:, :, lag], ap_re[:, :, lag]], -1),
                            jnp.concatenate([-ap_im[:, :, lag], ap_im[:, :, lag]], -1)]
    apow = jnp.stack(rot_rows(sub) + rot_rows(chunk), axis=2)
    return toep.astype(MXU_DTYPE), w_in.astype(MXU_DTYPE), w_out.astype(MXU_DTYPE), apow


def _outproj_kernel(x_ref, attn_ref, y_ref, mod_ref, ag_ref, sg_ref, gw_ref, gb_ref, woa_ref, wos_ref, o_ref):
    a = _rms(attn_ref[...].astype(F32), ag_ref[...]).astype(MXU_DTYPE)
    y = y_ref[...].astype(F32)
    y = 0.5 * y * (1.0 + jnp.tanh(math.sqrt(2.0 / math.pi) * (y + 0.044715 * (y * y * y))))
    y = y * jax.nn.sigmoid(_dot(y.astype(MXU_DTYPE), gw_ref[...]) + gb_ref[...])
    y = _rms(y, sg_ref[...]).astype(MXU_DTYPE)
    mixed = _dot(a, woa_ref[...]) + _dot(y, wos_ref[...])
    o_ref[...] = x_ref[...] + mod_ref[5:6, :] * mixed


def _outproj_call(x, attn, y, mod_l, attn_gain, ssm_gain, glu_w, glu_b, wo_a, wo_s, layer):
    b, s, d = x.shape
    tm = min(ROW_TILE, s)
    aw, sw = attn.shape[2], y.shape[2]
    row = lambda w: pl.BlockSpec((None, tm, w), lambda bi, i: (bi, i, 0))
    vec = lambda w: pl.BlockSpec((None, 1, w), lambda bi, i: (layer, 0, 0))
    wres = lambda r, c: _resident((None, r, c), lambda bi, i: (layer, 0, 0))
    return pl.pallas_call(
        _outproj_kernel,
        out_shape=jax.ShapeDtypeStruct(x.shape, F32),
        grid=(b, s // tm),
        in_specs=[row(d), row(aw), row(sw),
                  pl.BlockSpec((None, N_MOD, d), lambda bi, i: (bi, 0, 0)),
                  vec(aw), vec(sw), wres(sw, sw), vec(sw), wres(aw, d), wres(sw, d)],
        out_specs=row(d),
        compiler_params=_params("parallel", "parallel"),
        name="mixer_outproj",
    )(x, attn, y, mod_l, attn_gain[:, None, :], ssm_gain[:, None, :], glu_w, glu_b[:, None, :], wo_a, wo_s)


def kernel(x, c, positions, mod_w, mod_b, norm_g, ffn1_w1, ffn1_w3, ffn1_w2, ffn2_w1, ffn2_w3, ffn2_w2, w_in, w_out, attn_gain, ssm_gain, ssm_a_re, ssm_a_im, ssm_log_dt, ssm_b_re, ssm_b_im, ssm_c_re, ssm_c_im, ssm_d, glu_w, glu_b, final_g):
    b, s, d = x.shape
    depth = mod_w.shape[0]
    ssm_w = d - ATTN_W
    groups = ssm_w // SSM_GROUP
    chunk = min(SCAN_L, s)
    n_chunks = s // chunk
    cast = lambda a: a.astype(MXU_DTYPE)

    mod = _mod_call(c, mod_w, mod_b)
    tables = _rope_tables(positions)
    ffn1 = (cast(ffn1_w1), cast(ffn1_w3), cast(ffn1_w2))
    ffn2 = (cast(ffn2_w1), cast(ffn2_w3), cast(ffn2_w2))
    o_k = ATTN_W
    o_v = o_k + HEAD_DIM
    o_qi = o_v + HEAD_DIM
    o_ki = o_qi + IDX_HEADS * IDX_DIM
    o_wi = o_ki + IDX_DIM
    o_u = o_wi + IDX_HEADS
    w_in_t = w_in.transpose(0, 2, 1)
    pad_rows = lambda a, rows: jnp.concatenate([a, jnp.zeros((depth, rows - a.shape[1], d), a.dtype)], axis=1)
    wqt = cast(w_in_t[:, :o_k])
    wqit = cast(w_in_t[:, o_qi:o_ki])
    wkk = cast(jnp.concatenate([w_in[:, :, o_k:o_v], w_in[:, :, o_ki:o_wi]], axis=2))
    wvt = cast(pad_rows(w_in_t[:, o_v:o_qi], VT_ROWS))
    wwt = cast(pad_rows(w_in_t[:, o_wi:o_u], 8))
    wu = cast(w_in[:, :, o_u:])
    wo_a, wo_s = cast(w_out[:, :ATTN_W]), cast(w_out[:, ATTN_W:])
    glu_wc = cast(glu_w)
    toep, s_win, s_wout, apow = _ssm_operators(ssm_a_re, ssm_a_im, ssm_log_dt, ssm_b_re, ssm_b_im,
                                               ssm_c_re, ssm_c_im, ssm_d, chunk)

    for l in range(depth):
        x = _ffn_call(x, mod[l], norm_g[l, 0], *ffn1, l, 0)
        qt, qit, kk, vt, wt, u, qn2, kn2 = _inproj_call(x, mod[l], norm_g[l, 1], wqt, wqit, wkk, wvt, wwt, wu,
                                                        tables, l)
        attn = _dsa_call(qt, qit, wt, qn2, kn2, kk, vt)
        u_g = u.reshape(b, n_chunks, chunk, groups, SSM_GROUP).transpose(3, 1, 0, 2, 4)
        u_g = u_g.reshape(groups, n_chunks * b, chunk * SSM_GROUP)
        y_g = _ssm_call(u_g, toep, s_win, s_wout, apow, l, b)
        y = y_g.reshape(groups, n_chunks, b, chunk, SSM_GROUP).transpose(2, 1, 3, 0, 4).reshape(b, s, ssm_w)
        x = _outproj_call(x, attn, y, mod[l], attn_gain, ssm_gain, glu_wc, glu_b, wo_a, wo_s, l)
        x = _ffn_call(x, mod[l], norm_g[l, 2], *ffn2, l, 6, final_g=final_g if l == depth - 1 else None)
    return x
```
